```python
import jax, jax.numpy as jnp
from jax import lax
import numpy as np

D_MODEL = 1024
BATCH = 4
SEQ = 4096
DEPTH = 1
DEC_BATCH = 128
DEC_SEQ = 4
PAST_LEN = 8192
PAGE_SIZE = 128

HEAD_DIM = 64
N_NSA_HEADS = 8
N_KV_HEADS = 2
GQA_RATIO = N_NSA_HEADS // N_KV_HEADS
CMP_BLOCK = 32
CMP_STRIDE = 16
N_HALF = CMP_BLOCK // CMP_STRIDE
CMP_HIDDEN = 256
SEL_BLOCK = 64
N_SEL = 16
WINDOW = 512
NSA_Q_BLOCK = 64
WIN_Q_BLOCK = 128
N_RET_HEADS = 4
RET_DK = 64
RET_DV = 128
RET_CHUNK = 128
N_EXPERTS = 32
TOP_K = 4
D_FF = D_MODEL
SWIGLU_ALPHA = 1.702
SWIGLU_LIMIT = 7.0
MOE_BLOCK = 128
ROPE_THETA = 10000.0
RMS_EPS = 1e-5
NEG_INF = -1e30
FORCED_SCORE = 1e6
INVALID_SCORE = -1e9
NSA_WIDTH = N_NSA_HEADS * HEAD_DIM
KV_WIDTH = N_KV_HEADS * HEAD_DIM
RET_QK_WIDTH = N_RET_HEADS * RET_DK
RET_WIDTH = N_RET_HEADS * RET_DV
IN_SIZES = (NSA_WIDTH, KV_WIDTH, KV_WIDTH, KV_WIDTH, KV_WIDTH, KV_WIDTH, KV_WIDTH, 3 * N_NSA_HEADS, RET_QK_WIDTH, RET_QK_WIDTH, RET_WIDTH, RET_WIDTH)
IN_TOTAL = NSA_WIDTH + 6 * KV_WIDTH + 3 * N_NSA_HEADS + 2 * RET_QK_WIDTH + 2 * RET_WIDTH
MIX_WIDTH = NSA_WIDTH + RET_WIDTH
ATTN_SCALE = HEAD_DIM ** -0.5

kernel_name = 'hymba_nsa_retnet_moe_step'


def _rmsnorm(x, g):
    xf = x.astype(jnp.float32)
    y = xf * lax.rsqrt(jnp.mean(xf * xf, axis=-1, keepdims=True) + RMS_EPS)
    return (y * g.astype(jnp.float32)).astype(x.dtype)


def _rope(x, pos):
    half = x.shape[-1] // 2
    inv = ROPE_THETA ** (-jnp.arange(half, dtype=jnp.float32) / half)
    ang = pos.astype(jnp.float32)[:, None] * inv[None, :]
    cos = jnp.cos(ang)[None, :, None, :]
    sin = jnp.sin(ang)[None, :, None, :]
    xf = x.astype(jnp.float32)
    x1, x2 = xf[..., :half], xf[..., half:]
    return jnp.concatenate([x1 * cos - x2 * sin, x2 * cos + x1 * sin], axis=-1).astype(x.dtype)


def _masked_softmax(s, mask):
    s = jnp.where(mask, s, NEG_INF)
    m = jnp.max(s, axis=-1, keepdims=True)
    p = jnp.where(mask, jnp.exp(s - m), 0.0)
    return p / jnp.maximum(jnp.sum(p, axis=-1, keepdims=True), 1e-30)


def _mixer_inputs(x, pos, g_attn, w_in):
    b, t = x.shape[0], x.shape[1]
    h = _rmsnorm(x, g_attn)
    proj = jnp.einsum('btd,dc->btc', h, w_in)
    offs = [int(o) for o in np.cumsum(np.array(IN_SIZES))[:-1]]
    q, kc, vc, ks, vs, kw, vw, gt, rq, rk, rv, rg = jnp.split(proj, offs, axis=-1)

    def heads(a, n, d):
        return a.reshape(b, t, n, d)

    def kvh(a):
        return heads(a, N_KV_HEADS, HEAD_DIM)

    kv_cmp = jnp.stack([kvh(kc), kvh(vc)], axis=2)
    kv_sel = jnp.stack([_rope(kvh(ks), pos), kvh(vs)], axis=2)
    kv_win = jnp.stack([_rope(kvh(kw), pos), kvh(vw)], axis=2)
    gates = jax.nn.sigmoid(gt.astype(jnp.float32)).reshape(b, t, N_NSA_HEADS, 3)
    r_q = _rope(heads(rq, N_RET_HEADS, RET_DK), pos)
    r_k = _rope(heads(rk, N_RET_HEADS, RET_DK), pos) * (RET_DK ** -0.5)
    r_v = heads(rv, N_RET_HEADS, RET_DV)
    return heads(q, N_NSA_HEADS, HEAD_DIM), kv_cmp, kv_sel, kv_win, gates, r_q, r_k, r_v, rg


def _segment_proj(kv_rows, w1):
    b, l = kv_rows.shape[0], kv_rows.shape[1]
    seg = kv_rows.reshape(b, l // CMP_STRIDE, CMP_STRIDE, 2, N_KV_HEADS, HEAD_DIM)
    return jnp.einsum('bsrcgd,chrdk->bscghk', seg, w1)


def _compress_blocks(p_seg, b1, w2, b2):
    n_cmp = p_seg.shape[1] - N_HALF + 1
    hid = sum(p_seg[:, h:h + n_cmp, :, :, h, :] for h in range(N_HALF)) + b1[None, None, :, None, :]
    hid = jax.nn.gelu(hid.astype(jnp.float32))
    out = jnp.einsum('bncgk,ckd->bncgd', hid, w2.astype(jnp.float32))
    return out + b2.astype(jnp.float32)[None, None, :, None, :]


def _cmp_attend(q, blocks, q_pos):
    b, nq = q.shape[0], q.shape[1]
    n_cmp = blocks.shape[1]
    qg = q.astype(jnp.float32).reshape(b, nq, N_KV_HEADS, GQA_RATIO, HEAD_DIM)
    s = jnp.einsum('bqgrd,bngd->bqgrn', qg, blocks[:, :, 0]) * ATTN_SCALE
    blk_end = jnp.arange(n_cmp) * CMP_STRIDE + (CMP_BLOCK - 1)
    mask = blk_end[None, :] <= q_pos[:, None]
    p = _masked_softmax(s, mask[None, :, None, None, :])
    o = jnp.einsum('bqgrn,bngd->bqgrd', p, blocks[:, :, 1])
    return o.reshape(b, nq, N_NSA_HEADS, HEAD_DIM), jnp.sum(p, axis=3)


def _select_blocks(p_grp, q_pos, n_sb):
    ratio = SEL_BLOCK // CMP_STRIDE
    ov = N_HALF - 1
    n_cmp = p_grp.shape[-1]
    pp = jnp.pad(p_grp, ((0, 0), (0, 0), (0, 0), (ov, ratio * n_sb - n_cmp)))
    score = sum(pp[..., o:o + ratio * n_sb:ratio] for o in range(ratio + ov))
    blk = jnp.arange(n_sb)[None, :]
    qp = q_pos[:, None]
    cur = qp // SEL_BLOCK
    forced = (blk == 0) | (blk == cur) | (blk == cur - 1)
    valid = blk * SEL_BLOCK <= qp
    score = jnp.where(forced[None, :, None, :], FORCED_SCORE, score)
    score = jnp.where(valid[None, :, None, :], score, INVALID_SCORE)
    _, idx = lax.top_k(score, min(N_SEL, n_sb))
    return idx


def _sel_attend(q_rot, kv_g, idx, q_pos):
    b, nq = q_rot.shape[0], q_rot.shape[1]
    m = idx.shape[-1] * SEL_BLOCK
    qg = q_rot.astype(jnp.float32).reshape(b, nq, N_KV_HEADS, GQA_RATIO, HEAD_DIM)
    k = kv_g[..., 0, :].astype(jnp.float32).reshape(b, nq, N_KV_HEADS, m, HEAD_DIM)
    v = kv_g[..., 1, :].astype(jnp.float32).reshape(b, nq, N_KV_HEADS, m, HEAD_DIM)
    s = jnp.einsum('bqgrd,bqgmd->bqgrm', qg, k) * ATTN_SCALE
    k_pos = (idx[..., None] * SEL_BLOCK + jnp.arange(SEL_BLOCK)).reshape(b, nq, N_KV_HEADS, m)
    mask = (k_pos <= q_pos[None, :, None, None])[:, :, :, None, :]
    p = _masked_softmax(s, mask)
    o = jnp.einsum('bqgrm,bqgmd->bqgrd', p, v)
    return o.reshape(b, nq, N_NSA_HEADS, HEAD_DIM)


def _win_prompt(q_rot, kv_win):
    b, t = q_rot.shape[0], q_rot.shape[1]
    n_qb = t // WIN_Q_BLOCK
    span = WINDOW + WIN_Q_BLOCK
    kvp = jnp.pad(kv_win, ((0, 0), (WINDOW, 0), (0, 0), (0, 0), (0, 0)))
    idx = jnp.arange(n_qb)[:, None] * WIN_Q_BLOCK + jnp.arange(span)[None, :]
    kb = kvp[:, idx].astype(jnp.float32)
    k_pos = (idx - WINDOW)[:, None, :]
    q_pos = jnp.arange(t).reshape(n_qb, WIN_Q_BLOCK)[:, :, None]
    qb = q_rot.astype(jnp.float32).reshape(b, n_qb, WIN_Q_BLOCK, N_KV_HEADS, GQA_RATIO, HEAD_DIM)
    s = jnp.einsum('bnqgrd,bnkgd->bnqgrk', qb, kb[:, :, :, 0]) * ATTN_SCALE
    mask = (k_pos <= q_pos) & (k_pos > q_pos - WINDOW) & (k_pos >= 0)
    p = _masked_softmax(s, mask[None, :, :, None, None, :])
    o = jnp.einsum('bnqgrk,bnkgd->bnqgrd', p, kb[:, :, :, 1])
    return o.reshape(b, t, N_NSA_HEADS, HEAD_DIM)


def _win_sample(q_rot, win_buf, kv_new, q_pos, past):
    b, s_len = q_rot.shape[0], q_rot.shape[1]
    wb = win_buf.shape[1]
    kv = jnp.concatenate([win_buf.astype(kv_new.dtype), kv_new], axis=1)
    k_pos = (past - wb + jnp.arange(wb + s_len))[None, :]
    qp = q_pos[:, None]
    qg = q_rot.astype(jnp.float32).reshape(b, s_len, N_KV_HEADS, GQA_RATIO, HEAD_DIM)
    s = jnp.einsum('bqgrd,bkgd->bqgrk', qg, kv[:, :, 0].astype(jnp.float32)) * ATTN_SCALE
    mask = (k_pos <= qp) & (k_pos > qp - WINDOW) & (k_pos >= 0)
    p = _masked_softmax(s, mask[None, :, None, None, :])
    o = jnp.einsum('bqgrk,bkgd->bqgrd', p, kv[:, :, 1].astype(jnp.float32))
    return o.reshape(b, s_len, N_NSA_HEADS, HEAD_DIM), kv[:, s_len:]


def _nsa_merge(gates, o_c, o_s, o_w):
    b, t = o_c.shape[0], o_c.shape[1]
    o = gates[..., 0:1] * o_c + gates[..., 1:2] * o_s + gates[..., 2:3] * o_w
    return o.reshape(b, t, NSA_WIDTH)


def _nsa_prompt(q, kv_cmp, kv_sel, kv_win, w1, b1, w2, b2, pos):
    b, t = q.shape[0], q.shape[1]
    q_rot = _rope(q, pos)
    blocks = _compress_blocks(_segment_proj(kv_cmp, w1), b1, w2, b2)
    n_sb = t // SEL_BLOCK
    kbs = kv_sel.reshape(b, n_sb, SEL_BLOCK, 2, N_KV_HEADS, HEAD_DIM).transpose(0, 4, 1, 2, 3, 5)
    bi = jnp.arange(b)[:, None, None, None]
    gi = jnp.arange(N_KV_HEADS)[None, None, :, None]
    n_qb = t // NSA_Q_BLOCK

    def q_block(args):
        qb, qrb, pb = args
        o_c, p_grp = _cmp_attend(qb, blocks, pb)
        idx = _select_blocks(p_grp, pb, n_sb)
        kv_g = kbs[bi, gi, idx]
        return o_c, _sel_attend(qrb, kv_g, idx, pb)

    def to_blocks(a):
        return a.reshape(b, n_qb, NSA_Q_BLOCK, N_NSA_HEADS, HEAD_DIM).swapaxes(0, 1)

    def from_blocks(a):
        return a.swapaxes(0, 1).reshape(b, t, N_NSA_HEADS, HEAD_DIM)

    o_c, o_s = lax.map(q_block, (to_blocks(q), to_blocks(q_rot), pos.reshape(n_qb, NSA_Q_BLOCK)))
    return from_blocks(o_c), from_blocks(o_s), _win_prompt(q_rot, kv_win)


def _nsa_sample(q, kv_cmp, kv_sel, kv_win, cache_cmp_kv, cache_sel_kv, win_buf, page_table, layer, w1, b1, w2, b2, pos):
    db, s_len = q.shape[0], q.shape[1]
    past = page_table.shape[1] * PAGE_SIZE
    q_rot = _rope(q, pos)
    past_cmp = cache_cmp_kv[page_table, layer].reshape(db, past, 2, N_KV_HEADS, HEAD_DIM)
    new_cmp = jnp.pad(kv_cmp, ((0, 0), (0, (-s_len) % CMP_STRIDE), (0, 0), (0, 0), (0, 0)))
    p_seg = jnp.concatenate([_segment_proj(past_cmp.astype(kv_cmp.dtype), w1), _segment_proj(new_cmp, w1)], axis=1)
    blocks = _compress_blocks(p_seg, b1, w2, b2)
    o_c, p_grp = _cmp_attend(q, blocks, pos)
    n_pb = past // SEL_BLOCK
    n_tail = -(-s_len // SEL_BLOCK)
    idx = _select_blocks(p_grp, pos, n_pb + n_tail)
    bi = jnp.arange(db)[:, None, None, None]
    gi = jnp.arange(N_KV_HEADS)[None, None, :, None]
    blocks_per_page = PAGE_SIZE // SEL_BLOCK
    blk_past = jnp.minimum(idx, n_pb - 1)
    page = page_table[bi, blk_past // blocks_per_page]
    row = (blk_past % blocks_per_page)[..., None] * SEL_BLOCK + jnp.arange(SEL_BLOCK)
    past_g = cache_sel_kv[page[..., None], layer, row, :, gi[..., None], :]
    tail = jnp.pad(kv_sel, ((0, 0), (0, n_tail * SEL_BLOCK - s_len), (0, 0), (0, 0), (0, 0)))
    tail = tail.reshape(db, n_tail, SEL_BLOCK, 2, N_KV_HEADS, HEAD_DIM).transpose(0, 4, 1, 2, 3, 5)
    tail_g = tail[bi, gi, jnp.clip(idx - n_pb, 0, n_tail - 1)]
    kv_g = jnp.where((idx >= n_pb)[..., None, None, None], tail_g, past_g.astype(tail_g.dtype))
    o_s = _sel_attend(q_rot, kv_g, idx, pos)
    o_w, new_win = _win_sample(q_rot, win_buf, kv_win, pos, past)
    return o_c, o_s, o_w, new_win


def _ret_log_decay():
    return jnp.log1p(-jnp.exp2(-5.0 - jnp.arange(N_RET_HEADS, dtype=jnp.float32)))


def _ret_chunk(state, q, k, v, log_g):
    c = q.shape[1]
    i = jnp.arange(c, dtype=jnp.float32)
    diff = i[:, None] - i[None, :]
    decay = jnp.where(diff[None] >= 0, jnp.exp(log_g[:, None, None] * jnp.maximum(diff, 0.0)[None]), 0.0)
    scores = jnp.einsum('bihd,bjhd->bhij', q, k) * decay[None]
    inner = jnp.einsum('bhij,bjhv->bihv', scores, v)
    q_dec = jnp.exp(log_g[None, :] * (i[:, None] + 1.0))
    cross = jnp.einsum('bihd,bhdv->bihv', q * q_dec[None, :, :, None], state)
    k_dec = jnp.exp(log_g[None, :] * (c - 1.0 - i[:, None]))
    new_state = jnp.exp(log_g * c)[None, :, None, None] * state + jnp.einsum('bjhd,bjhv->bhdv', k * k_dec[None, :, :, None], v)
    return new_state, inner + cross


def _ret_prompt(r_q, r_k, r_v, log_g):
    b, t = r_q.shape[0], r_q.shape[1]
    n_ch = t // RET_CHUNK

    def chunks(a):
        return a.astype(jnp.float32).reshape(b, n_ch, RET_CHUNK, a.shape[2], a.shape[3]).swapaxes(0, 1)

    s0 = jnp.zeros((b, N_RET_HEADS, RET_DK, RET_DV), jnp.float32)
    s_fin, o = lax.scan(lambda s, xs: _ret_chunk(s, xs[0], xs[1], xs[2], log_g), s0, (chunks(r_q), chunks(r_k), chunks(r_v)))
    return s_fin, o.swapaxes(0, 1).reshape(b, t, N_RET_HEADS, RET_DV)


def _ret_out(o, r_g):
    b, t = o.shape[0], o.shape[1]
    o = o * lax.rsqrt(jnp.mean(o * o, axis=-1, keepdims=True) + RMS_EPS)
    return o.reshape(b, t, RET_WIDTH) * jax.nn.silu(r_g.astype(jnp.float32))


def _moe(h, w_router, b_router, w_up, b_up, w_down, b_down):
    n_tok, d = h.shape
    logits = jnp.einsum('nd,de->ne', h, w_router).astype(jnp.float32) + b_router.astype(jnp.float32)
    top_val, top_idx = lax.top_k(logits, TOP_K)
    gate = jax.nn.softmax(top_val, axis=-1)
    n_assign = n_tok * TOP_K
    flat_e = top_idx.reshape(-1)
    order = jnp.argsort(flat_e)
    sorted_e = flat_e[order]
    tok = order // TOP_K
    counts = jnp.zeros((N_EXPERTS,), jnp.int32).at[flat_e].add(1)
    padded = (counts + MOE_BLOCK - 1) // MOE_BLOCK * MOE_BLOCK
    pad_end = jnp.cumsum(padded)
    pad_start = pad_end - padded
    start = jnp.cumsum(counts) - counts
    dest = pad_start[sorted_e] + jnp.arange(n_assign, dtype=jnp.int32) - start[sorted_e]
    n_blocks = -(-n_assign // MOE_BLOCK) + N_EXPERTS
    rows = jnp.zeros((n_blocks * MOE_BLOCK, d), h.dtype).at[dest].set(h[tok])
    blk_e = jnp.minimum(jnp.searchsorted(pad_end, jnp.arange(n_blocks, dtype=jnp.int32) * MOE_BLOCK, side='right'), N_EXPERTS - 1)

    def expert_block(args):
        xb, e = args
        u = xb @ w_up[e] + b_up[e]
        glu = jnp.minimum(u[:, :D_FF], SWIGLU_LIMIT)
        lin = jnp.clip(u[:, D_FF:], -SWIGLU_LIMIT, SWIGLU_LIMIT)
        a = glu * jax.nn.sigmoid(SWIGLU_ALPHA * glu) * (lin + 1.0)
        return a @ w_down[e] + b_down[e]

    out = lax.map(expert_block, (rows.reshape(n_blocks, MOE_BLOCK, d), blk_e)).reshape(-1, d)
    contrib = out[dest] * gate.reshape(-1)[order][:, None].astype(h.dtype)
    return jnp.zeros((n_tok, d), h.dtype).at[tok].add(contrib)


def _finish(x, o_nsa, o_ret, w_out, g_ffn, w_router, b_router, w_up, b_up, w_down, b_down):
    b, t, d = x.shape
    mix = jnp.concatenate([o_nsa, o_ret], axis=-1).astype(x.dtype)
    x = x + jnp.einsum('btc,cd->btd', mix, w_out)
    h = _rmsnorm(x, g_ffn).reshape(b * t, d)
    return x + _moe(h, w_router, b_router, w_up, b_up, w_down, b_down).reshape(b, t, d)


def setup_inputs(seed: int = 0) -> dict:
    key = jax.random.key(seed)
    ks = jax.random.split(key, 22)
    n_pages = PAST_LEN // PAGE_SIZE
    n_phys = (DEC_BATCH * n_pages * 5) // 4
    w_buf = min(WINDOW, PAST_LEN)
    kv_row = (2, N_KV_HEADS, HEAD_DIM)

    def nrm(k, shape, scale=1.0):
        return scale * jax.random.normal(k, shape, jnp.float32)

    page_table = jax.random.permutation(ks[0], n_phys)[:DEC_BATCH * n_pages].reshape(DEC_BATCH, n_pages).astype(jnp.int32)
    return {
        'x_prompt': nrm(ks[1], (BATCH, SEQ, D_MODEL)),
        'x_sample': nrm(ks[2], (DEC_BATCH, DEC_SEQ, D_MODEL)),
        'cache_cmp_kv': nrm(ks[3], (n_phys, DEPTH, PAGE_SIZE) + kv_row),
        'cache_sel_kv': nrm(ks[4], (n_phys, DEPTH, PAGE_SIZE) + kv_row),
        'cache_win_kv': nrm(ks[5], (DEC_BATCH, DEPTH, w_buf) + kv_row),
        'state_ret': nrm(ks[6], (DEC_BATCH, DEPTH, N_RET_HEADS, RET_DK, RET_DV), 0.5),
        'page_table': page_table,
        'g_attn': 1.0 + nrm(ks[7], (DEPTH, D_MODEL), 0.02),
        'w_in': nrm(ks[8], (DEPTH, D_MODEL, IN_TOTAL), D_MODEL ** -0.5),
        'w_cmp1': nrm(ks[9], (DEPTH, 2, N_HALF, CMP_STRIDE, HEAD_DIM, CMP_HIDDEN), (CMP_BLOCK * HEAD_DIM) ** -0.5),
        'b_cmp1': nrm(ks[10], (DEPTH, 2, CMP_HIDDEN), 0.02),
        'w_cmp2': nrm(ks[11], (DEPTH, 2, CMP_HIDDEN, HEAD_DIM), CMP_HIDDEN ** -0.5),
        'b_cmp2': nrm(ks[12], (DEPTH, 2, HEAD_DIM), 0.02),
        'w_out': nrm(ks[13], (DEPTH, MIX_WIDTH, D_MODEL), MIX_WIDTH ** -0.5),
        'g_ffn': 1.0 + nrm(ks[14], (DEPTH, D_MODEL), 0.02),
        'w_router': nrm(ks[15], (DEPTH, D_MODEL, N_EXPERTS), D_MODEL ** -0.5),
        'b_router': nrm(ks[16], (DEPTH, N_EXPERTS), 0.01),
        'w_up': nrm(ks[17], (DEPTH, N_EXPERTS, D_MODEL, 2 * D_FF), D_MODEL ** -0.5),
        'b_up': nrm(ks[18], (DEPTH, N_EXPERTS, 2 * D_FF), 0.01),
        'w_down': nrm(ks[19], (DEPTH, N_EXPERTS, D_FF, D_MODEL), D_FF ** -0.5),
        'b_down': nrm(ks[20], (DEPTH, N_EXPERTS, D_MODEL), 0.01),
        'g_final': 1.0 + nrm(ks[21], (D_MODEL,), 0.02),
    }


def reference(x_prompt, x_sample, cache_cmp_kv, cache_sel_kv, cache_win_kv, state_ret, page_table, g_attn, w_in, w_cmp1, b_cmp1, w_cmp2, b_cmp2, w_out, g_ffn, w_router, b_router, w_up, b_up, w_down, b_down, g_final):
    seq = x_prompt.shape[1]
    past = page_table.shape[1] * PAGE_SIZE
    pos_p = jnp.arange(seq, dtype=jnp.int32)
    pos_s = past + jnp.arange(x_sample.shape[1], dtype=jnp.int32)
    log_g = _ret_log_decay()
    xp, xs = x_prompt, x_sample
    cmp_p, sel_p, win_p, ret_p = [], [], [], []
    cmp_s, sel_s, win_s, ret_s = [], [], [], []
    for l in range(DEPTH):
        q, kv_cmp, kv_sel, kv_win, gates, r_q, r_k, r_v, r_g = _mixer_inputs(xp, pos_p, g_attn[l], w_in[l])
        o_c, o_s, o_w = _nsa_prompt(q, kv_cmp, kv_sel, kv_win, w_cmp1[l], b_cmp1[l], w_cmp2[l], b_cmp2[l], pos_p)
        s_fin, o_r = _ret_prompt(r_q, r_k, r_v, log_g)
        xp = _finish(xp, _nsa_merge(gates, o_c, o_s, o_w), _ret_out(o_r, r_g), w_out[l], g_ffn[l], w_router[l], b_router[l], w_up[l], b_up[l], w_down[l], b_down[l])
        cmp_p.append(kv_cmp)
        sel_p.append(kv_sel)
        win_p.append(kv_win[:, seq - min(WINDOW, seq):])
        ret_p.append(s_fin)
        q, kv_cmp, kv_sel, kv_win, gates, r_q, r_k, r_v, r_g = _mixer_inputs(xs, pos_s, g_attn[l], w_in[l])
        o_c, o_s, o_w, new_win = _nsa_sample(q, kv_cmp, kv_sel, kv_win, cache_cmp_kv, cache_sel_kv, cache_win_kv[:, l], page_table, l, w_cmp1[l], b_cmp1[l], w_cmp2[l], b_cmp2[l], pos_s)
        s_new, o_r = _ret_chunk(state_ret[:, l].astype(jnp.float32), r_q.astype(jnp.float32), r_k.astype(jnp.float32), r_v.astype(jnp.float32), log_g)
        xs = _finish(xs, _nsa_merge(gates, o_c, o_s, o_w), _ret_out(o_r, r_g), w_out[l], g_ffn[l], w_router[l], b_router[l], w_up[l], b_up[l], w_down[l], b_down[l])
        cmp_s.append(kv_cmp)
        sel_s.append(kv_sel)
        win_s.append(new_win)
        ret_s.append(s_new)
    y_prompt = _rmsnorm(xp, g_final)
    y_sample = _rmsnorm(xs, g_final)
    return (y_prompt, y_sample, jnp.stack(cmp_p, axis=1), jnp.stack(sel_p, axis=1), jnp.stack(win_p, axis=1), jnp.stack(ret_p, axis=1), jnp.stack(cmp_s, axis=1), jnp.stack(sel_s, axis=1), jnp.stack(win_s, axis=1), jnp.stack(ret_s, axis=1))
```

```python
import functools
import jax, jax.numpy as jnp
from jax import lax
import numpy as np
from jax.experimental import pallas as pl
from jax.experimental.pallas import tpu as pltpu

D_MODEL = 1024
BATCH = 4
SEQ = 4096
DEPTH = 1
DEC_BATCH = 128
DEC_SEQ = 4
PAST_LEN = 8192
PAGE_SIZE = 128

HEAD_DIM = 64
N_NSA_HEADS = 8
N_KV_HEADS = 2
GQA_RATIO = N_NSA_HEADS // N_KV_HEADS
CMP_BLOCK = 32
CMP_STRIDE = 16
N_HALF = CMP_BLOCK // CMP_STRIDE
CMP_HIDDEN = 256
SEL_BLOCK = 64
N_SEL = 16
WINDOW = 512
NSA_Q_BLOCK = 64
WIN_Q_BLOCK = 128
N_RET_HEADS = 4
RET_DK = 64
RET_DV = 128
RET_CHUNK = 128
N_EXPERTS = 32
TOP_K = 4
D_FF = D_MODEL
SWIGLU_ALPHA = 1.702
SWIGLU_LIMIT = 7.0
MOE_BLOCK = 128
ROPE_THETA = 10000.0
RMS_EPS = 1e-5
NEG_INF = -1e30
FORCED_SCORE = 1e6
INVALID_SCORE = -1e9
NSA_WIDTH = N_NSA_HEADS * HEAD_DIM
KV_WIDTH = N_KV_HEADS * HEAD_DIM
RET_QK_WIDTH = N_RET_HEADS * RET_DK
RET_WIDTH = N_RET_HEADS * RET_DV
IN_SIZES = (NSA_WIDTH, KV_WIDTH, KV_WIDTH, KV_WIDTH, KV_WIDTH, KV_WIDTH, KV_WIDTH, 3 * N_NSA_HEADS, RET_QK_WIDTH, RET_QK_WIDTH, RET_WIDTH, RET_WIDTH)
IN_TOTAL = NSA_WIDTH + 6 * KV_WIDTH + 3 * N_NSA_HEADS + 2 * RET_QK_WIDTH + 2 * RET_WIDTH
MIX_WIDTH = NSA_WIDTH + RET_WIDTH
ATTN_SCALE = HEAD_DIM ** -0.5


def _rmsnorm(x, g):
    xf = x.astype(jnp.float32)
    y = xf * lax.rsqrt(jnp.mean(xf * xf, axis=-1, keepdims=True) + RMS_EPS)
    return (y * g.astype(jnp.float32)).astype(x.dtype)


def _rope(x, pos):
    half = x.shape[-1] // 2
    inv = ROPE_THETA ** (-jnp.arange(half, dtype=jnp.float32) / half)
    ang = pos.astype(jnp.float32)[:, None] * inv[None, :]
    cos = jnp.cos(ang)[None, :, None, :]
    sin = jnp.sin(ang)[None, :, None, :]
    xf = x.astype(jnp.float32)
    x1, x2 = xf[..., :half], xf[..., half:]
    return jnp.concatenate([x1 * cos - x2 * sin, x2 * cos + x1 * sin], axis=-1).astype(x.dtype)


def _masked_softmax(s, mask):
    s = jnp.where(mask, s, NEG_INF)
    m = jnp.max(s, axis=-1, keepdims=True)
    p = jnp.where(mask, jnp.exp(s - m), 0.0)
    return p / jnp.maximum(jnp.sum(p, axis=-1, keepdims=True), 1e-30)


def _mixer_inputs(x, pos, g_attn, w_in):
    b, t = x.shape[0], x.shape[1]
    h = _rmsnorm(x, g_attn)
    proj = jnp.einsum('btd,dc->btc', h, w_in)
    offs = [int(o) for o in np.cumsum(np.array(IN_SIZES))[:-1]]
    q, kc, vc, ks, vs, kw, vw, gt, rq, rk, rv, rg = jnp.split(proj, offs, axis=-1)

    def heads(a, n, d):
        return a.reshape(b, t, n, d)

    def kvh(a):
        return heads(a, N_KV_HEADS, HEAD_DIM)

    kv_cmp = jnp.stack([kvh(kc), kvh(vc)], axis=2)
    kv_sel = jnp.stack([_rope(kvh(ks), pos), kvh(vs)], axis=2)
    kv_win = jnp.stack([_rope(kvh(kw), pos), kvh(vw)], axis=2)
    gates = jax.nn.sigmoid(gt.astype(jnp.float32)).reshape(b, t, N_NSA_HEADS, 3)
    r_q = _rope(heads(rq, N_RET_HEADS, RET_DK), pos)
    r_k = _rope(heads(rk, N_RET_HEADS, RET_DK), pos) * (RET_DK ** -0.5)
    r_v = heads(rv, N_RET_HEADS, RET_DV)
    return heads(q, N_NSA_HEADS, HEAD_DIM), kv_cmp, kv_sel, kv_win, gates, r_q, r_k, r_v, rg


def _segment_proj(kv_rows, w1):
    b, l = kv_rows.shape[0], kv_rows.shape[1]
    seg = kv_rows.reshape(b, l // CMP_STRIDE, CMP_STRIDE, 2, N_KV_HEADS, HEAD_DIM)
    return jnp.einsum('bsrcgd,chrdk->bscghk', seg, w1)


def _compress_blocks(p_seg, b1, w2, b2):
    n_cmp = p_seg.shape[1] - N_HALF + 1
    hid = sum(p_seg[:, h:h + n_cmp, :, :, h, :] for h in range(N_HALF)) + b1[None, None, :, None, :]
    hid = jax.nn.gelu(hid.astype(jnp.float32))
    out = jnp.einsum('bncgk,ckd->bncgd', hid, w2.astype(jnp.float32))
    return out + b2.astype(jnp.float32)[None, None, :, None, :]


def _cmp_attend(q, blocks, q_pos):
    b, nq = q.shape[0], q.shape[1]
    n_cmp = blocks.shape[1]
    qg = q.astype(jnp.float32).reshape(b, nq, N_KV_HEADS, GQA_RATIO, HEAD_DIM)
    s = jnp.einsum('bqgrd,bngd->bqgrn', qg, blocks[:, :, 0]) * ATTN_SCALE
    blk_end = jnp.arange(n_cmp) * CMP_STRIDE + (CMP_BLOCK - 1)
    mask = blk_end[None, :] <= q_pos[:, None]
    p = _masked_softmax(s, mask[None, :, None, None, :])
    o = jnp.einsum('bqgrn,bngd->bqgrd', p, blocks[:, :, 1])
    return o.reshape(b, nq, N_NSA_HEADS, HEAD_DIM), jnp.sum(p, axis=3)


def _select_blocks(p_grp, q_pos, n_sb):
    ratio = SEL_BLOCK // CMP_STRIDE
    ov = N_HALF - 1
    n_cmp = p_grp.shape[-1]
    pp = jnp.pad(p_grp, ((0, 0), (0, 0), (0, 0), (ov, ratio * n_sb - n_cmp)))
    score = sum(pp[..., o:o + ratio * n_sb:ratio] for o in range(ratio + ov))
    blk = jnp.arange(n_sb)[None, :]
    qp = q_pos[:, None]
    cur = qp // SEL_BLOCK
    forced = (blk == 0) | (blk == cur) | (blk == cur - 1)
    valid = blk * SEL_BLOCK <= qp
    score = jnp.where(forced[None, :, None, :], FORCED_SCORE, score)
    score = jnp.where(valid[None, :, None, :], score, INVALID_SCORE)
    _, idx = lax.top_k(score, min(N_SEL, n_sb))
    return idx


def _sel_attend(q_rot, kv_g, idx, q_pos):
    b, nq = q_rot.shape[0], q_rot.shape[1]
    m = idx.shape[-1] * SEL_BLOCK
    qg = q_rot.astype(jnp.float32).reshape(b, nq, N_KV_HEADS, GQA_RATIO, HEAD_DIM)
    k = kv_g[..., 0, :].astype(jnp.float32).reshape(b, nq, N_KV_HEADS, m, HEAD_DIM)
    v = kv_g[..., 1, :].astype(jnp.float32).reshape(b, nq, N_KV_HEADS, m, HEAD_DIM)
    s = jnp.einsum('bqgrd,bqgmd->bqgrm', qg, k) * ATTN_SCALE
    k_pos = (idx[..., None] * SEL_BLOCK + jnp.arange(SEL_BLOCK)).reshape(b, nq, N_KV_HEADS, m)
    mask = (k_pos <= q_pos[None, :, None, None])[:, :, :, None, :]
    p = _masked_softmax(s, mask)
    o = jnp.einsum('bqgrm,bqgmd->bqgrd', p, v)
    return o.reshape(b, nq, N_NSA_HEADS, HEAD_DIM)


def _win_prompt(q_rot, kv_win):
    b, t = q_rot.shape[0], q_rot.shape[1]
    n_qb = t // WIN_Q_BLOCK
    span = WINDOW + WIN_Q_BLOCK
    kvp = jnp.pad(kv_win, ((0, 0), (WINDOW, 0), (0, 0), (0, 0), (0, 0)))
    idx = jnp.arange(n_qb)[:, None] * WIN_Q_BLOCK + jnp.arange(span)[None, :]
    kb = kvp[:, idx].astype(jnp.float32)
    k_pos = (idx - WINDOW)[:, None, :]
    q_pos = jnp.arange(t).reshape(n_qb, WIN_Q_BLOCK)[:, :, None]
    qb = q_rot.astype(jnp.float32).reshape(b, n_qb, WIN_Q_BLOCK, N_KV_HEADS, GQA_RATIO, HEAD_DIM)
    s = jnp.einsum('bnqgrd,bnkgd->bnqgrk', qb, kb[:, :, :, 0]) * ATTN_SCALE
    mask = (k_pos <= q_pos) & (k_pos > q_pos - WINDOW) & (k_pos >= 0)
    p = _masked_softmax(s, mask[None, :, :, None, None, :])
    o = jnp.einsum('bnqgrk,bnkgd->bnqgrd', p, kb[:, :, :, 1])
    return o.reshape(b, t, N_NSA_HEADS, HEAD_DIM)


def _win_sample(q_rot, win_buf, kv_new, q_pos, past):
    b, s_len = q_rot.shape[0], q_rot.shape[1]
    wb = win_buf.shape[1]
    kv = jnp.concatenate([win_buf.astype(kv_new.dtype), kv_new], axis=1)
    k_pos = (past - wb + jnp.arange(wb + s_len))[None, :]
    qp = q_pos[:, None]
    qg = q_rot.astype(jnp.float32).reshape(b, s_len, N_KV_HEADS, GQA_RATIO, HEAD_DIM)
    s = jnp.einsum('bqgrd,bkgd->bqgrk', qg, kv[:, :, 0].astype(jnp.float32)) * ATTN_SCALE
    mask = (k_pos <= qp) & (k_pos > qp - WINDOW) & (k_pos >= 0)
    p = _masked_softmax(s, mask[None, :, None, None, :])
    o = jnp.einsum('bqgrk,bkgd->bqgrd', p, kv[:, :, 1].astype(jnp.float32))
    return o.reshape(b, s_len, N_NSA_HEADS, HEAD_DIM), kv[:, s_len:]


def _nsa_merge(gates, o_c, o_s, o_w):
    b, t = o_c.shape[0], o_c.shape[1]
    o = gates[..., 0:1] * o_c + gates[..., 1:2] * o_s + gates[..., 2:3] * o_w
    return o.reshape(b, t, NSA_WIDTH)


def _nsa_prompt(q, kv_cmp, kv_sel, kv_win, w1, b1, w2, b2, pos):
    b, t = q.shape[0], q.shape[1]
    q_rot = _rope(q, pos)
    blocks = _compress_blocks(_segment_proj(kv_cmp, w1), b1, w2, b2)
    n_sb = t // SEL_BLOCK
    kbs = kv_sel.reshape(b, n_sb, SEL_BLOCK, 2, N_KV_HEADS, HEAD_DIM).transpose(0, 4, 1, 2, 3, 5)
    bi = jnp.arange(b)[:, None, None, None]
    gi = jnp.arange(N_KV_HEADS)[None, None, :, None]
    n_qb = t // NSA_Q_BLOCK

    def q_block(args):
        qb, qrb, pb = args
        o_c, p_grp = _cmp_attend(qb, blocks, pb)
        idx = _select_blocks(p_grp, pb, n_sb)
        kv_g = kbs[bi, gi, idx]
        return o_c, _sel_attend(qrb, kv_g, idx, pb)

    def to_blocks(a):
        return a.reshape(b, n_qb, NSA_Q_BLOCK, N_NSA_HEADS, HEAD_DIM).swapaxes(0, 1)

    def from_blocks(a):
        return a.swapaxes(0, 1).reshape(b, t, N_NSA_HEADS, HEAD_DIM)

    o_c, o_s = lax.map(q_block, (to_blocks(q), to_blocks(q_rot), pos.reshape(n_qb, NSA_Q_BLOCK)))
    return from_blocks(o_c), from_blocks(o_s), _win_prompt(q_rot, kv_win)


def _nsa_sample(q, kv_cmp, kv_sel, kv_win, cache_cmp_kv, cache_sel_kv, win_buf, page_table, layer, w1, b1, w2, b2, pos):
    db, s_len = q.shape[0], q.shape[1]
    past = page_table.shape[1] * PAGE_SIZE
    q_rot = _rope(q, pos)
    past_cmp = cache_cmp_kv[page_table, layer].reshape(db, past, 2, N_KV_HEADS, HEAD_DIM)
    new_cmp = jnp.pad(kv_cmp, ((0, 0), (0, (-s_len) % CMP_STRIDE), (0, 0), (0, 0), (0, 0)))
    p_seg = jnp.concatenate([_segment_proj(past_cmp.astype(kv_cmp.dtype), w1), _segment_proj(new_cmp, w1)], axis=1)
    blocks = _compress_blocks(p_seg, b1, w2, b2)
    o_c, p_grp = _cmp_attend(q, blocks, pos)
    n_pb = past // SEL_BLOCK
    n_tail = -(-s_len // SEL_BLOCK)
    idx = _select_blocks(p_grp, pos, n_pb + n_tail)
    bi = jnp.arange(db)[:, None, None, None]
    gi = jnp.arange(N_KV_HEADS)[None, None, :, None]
    blocks_per_page = PAGE_SIZE // SEL_BLOCK
    blk_past = jnp.minimum(idx, n_pb - 1)
    page = page_table[bi, blk_past // blocks_per_page]
    row = (blk_past % blocks_per_page)[..., None] * SEL_BLOCK + jnp.arange(SEL_BLOCK)
    past_g = cache_sel_kv[page[..., None], layer, row, :, gi[..., None], :]
    tail = jnp.pad(kv_sel, ((0, 0), (0, n_tail * SEL_BLOCK - s_len), (0, 0), (0, 0), (0, 0)))
    tail = tail.reshape(db, n_tail, SEL_BLOCK, 2, N_KV_HEADS, HEAD_DIM).transpose(0, 4, 1, 2, 3, 5)
    tail_g = tail[bi, gi, jnp.clip(idx - n_pb, 0, n_tail - 1)]
    kv_g = jnp.where((idx >= n_pb)[..., None, None, None], tail_g, past_g.astype(tail_g.dtype))
    o_s = _sel_attend(q_rot, kv_g, idx, pos)
    o_w, new_win = _win_sample(q_rot, win_buf, kv_win, pos, past)
    return o_c, o_s, o_w, new_win


def _ret_log_decay():
    return jnp.log1p(-jnp.exp2(-5.0 - jnp.arange(N_RET_HEADS, dtype=jnp.float32)))


def _ret_chunk(state, q, k, v, log_g):
    c = q.shape[1]
    i = jnp.arange(c, dtype=jnp.float32)
    diff = i[:, None] - i[None, :]
    decay = jnp.where(diff[None] >= 0, jnp.exp(log_g[:, None, None] * jnp.maximum(diff, 0.0)[None]), 0.0)
    scores = jnp.einsum('bihd,bjhd->bhij', q, k) * decay[None]
    inner = jnp.einsum('bhij,bjhv->bihv', scores, v)
    q_dec = jnp.exp(log_g[None, :] * (i[:, None] + 1.0))
    cross = jnp.einsum('bihd,bhdv->bihv', q * q_dec[None, :, :, None], state)
    k_dec = jnp.exp(log_g[None, :] * (c - 1.0 - i[:, None]))
    new_state = jnp.exp(log_g * c)[None, :, None, None] * state + jnp.einsum('bjhd,bjhv->bhdv', k * k_dec[None, :, :, None], v)
    return new_state, inner + cross


def _ret_prompt(r_q, r_k, r_v, log_g):
    b, t = r_q.shape[0], r_q.shape[1]
    n_ch = t // RET_CHUNK

    def chunks(a):
        return a.astype(jnp.float32).reshape(b, n_ch, RET_CHUNK, a.shape[2], a.shape[3]).swapaxes(0, 1)

    s0 = jnp.zeros((b, N_RET_HEADS, RET_DK, RET_DV), jnp.float32)
    s_fin, o = lax.scan(lambda s, xs: _ret_chunk(s, xs[0], xs[1], xs[2], log_g), s0, (chunks(r_q), chunks(r_k), chunks(r_v)))
    return s_fin, o.swapaxes(0, 1).reshape(b, t, N_RET_HEADS, RET_DV)


def _ret_out(o, r_g):
    b, t = o.shape[0], o.shape[1]
    o = o * lax.rsqrt(jnp.mean(o * o, axis=-1, keepdims=True) + RMS_EPS)
    return o.reshape(b, t, RET_WIDTH) * jax.nn.silu(r_g.astype(jnp.float32))


def _moe(h, w_router, b_router, w_up, b_up, w_down, b_down):
    n_tok, d = h.shape
    logits = jnp.einsum('nd,de->ne', h, w_router).astype(jnp.float32) + b_router.astype(jnp.float32)
    top_val, top_idx = lax.top_k(logits, TOP_K)
    gate = jax.nn.softmax(top_val, axis=-1)
    n_assign = n_tok * TOP_K
    flat_e = top_idx.reshape(-1)
    order = jnp.argsort(flat_e)
    sorted_e = flat_e[order]
    tok = order // TOP_K
    counts = jnp.zeros((N_EXPERTS,), jnp.int32).at[flat_e].add(1)
    padded = (counts + MOE_BLOCK - 1) // MOE_BLOCK * MOE_BLOCK
    pad_end = jnp.cumsum(padded)
    pad_start = pad_end - padded
    start = jnp.cumsum(counts) - counts
    dest = pad_start[sorted_e] + jnp.arange(n_assign, dtype=jnp.int32) - start[sorted_e]
    n_blocks = -(-n_assign // MOE_BLOCK) + N_EXPERTS
    rows = jnp.zeros((n_blocks * MOE_BLOCK, d), h.dtype).at[dest].set(h[tok])
    blk_e = jnp.minimum(jnp.searchsorted(pad_end, jnp.arange(n_blocks, dtype=jnp.int32) * MOE_BLOCK, side='right'), N_EXPERTS - 1)

    def expert_block(args):
        xb, e = args
        u = xb @ w_up[e] + b_up[e]
        glu = jnp.minimum(u[:, :D_FF], SWIGLU_LIMIT)
        lin = jnp.clip(u[:, D_FF:], -SWIGLU_LIMIT, SWIGLU_LIMIT)
        a = glu * jax.nn.sigmoid(SWIGLU_ALPHA * glu) * (lin + 1.0)
        return a @ w_down[e] + b_down[e]

    out = lax.map(expert_block, (rows.reshape(n_blocks, MOE_BLOCK, d), blk_e)).reshape(-1, d)
    contrib = out[dest] * gate.reshape(-1)[order][:, None].astype(h.dtype)
    return jnp.zeros((n_tok, d), h.dtype).at[tok].add(contrib)


def _finish(x, o_nsa, o_ret, w_out, g_ffn, w_router, b_router, w_up, b_up, w_down, b_down):
    b, t, d = x.shape
    mix = jnp.concatenate([o_nsa, o_ret], axis=-1).astype(x.dtype)
    x = x + jnp.einsum('btc,cd->btd', mix, w_out)
    h = _rmsnorm(x, g_ffn).reshape(b * t, d)
    return x + _moe(h, w_router, b_router, w_up, b_up, w_down, b_down).reshape(b, t, d)


def _final_norm_kernel(x_ref, g_ref, o_ref):
    x = x_ref[...]
    y = x * lax.rsqrt(jnp.mean(x * x, axis=-1, keepdims=True) + RMS_EPS)
    o_ref[...] = y * g_ref[...]


def _final_norm(x, g):
    b, t, d = x.shape
    n = b * t
    tile = 512
    out = pl.pallas_call(
        _final_norm_kernel,
        grid=(n // tile,),
        in_specs=[pl.BlockSpec((tile, d), lambda i: (i, 0)), pl.BlockSpec((1, d), lambda i: (0, 0))],
        out_specs=pl.BlockSpec((tile, d), lambda i: (i, 0)),
        out_shape=jax.ShapeDtypeStruct((n, d), x.dtype),
        name="final_norm",
    )(x.reshape(n, d), g.reshape(1, d))
    return out.reshape(b, t, d)


def kernel(x_prompt, x_sample, cache_cmp_kv, cache_sel_kv, cache_win_kv, state_ret, page_table, g_attn, w_in, w_cmp1, b_cmp1, w_cmp2, b_cmp2, w_out, g_ffn, w_router, b_router, w_up, b_up, w_down, b_down, g_final):
    seq = x_prompt.shape[1]
    past = page_table.shape[1] * PAGE_SIZE
    pos_p = jnp.arange(seq, dtype=jnp.int32)
    pos_s = past + jnp.arange(x_sample.shape[1], dtype=jnp.int32)
    log_g = _ret_log_decay()
    xp, xs = x_prompt, x_sample
    cmp_p, sel_p, win_p, ret_p = [], [], [], []
    cmp_s, sel_s, win_s, ret_s = [], [], [], []
    for l in range(DEPTH):
        q, kv_cmp, kv_sel, kv_win, gates, r_q, r_k, r_v, r_g = _mixer_inputs(xp, pos_p, g_attn[l], w_in[l])
        o_c, o_s, o_w = _nsa_prompt(q, kv_cmp, kv_sel, kv_win, w_cmp1[l], b_cmp1[l], w_cmp2[l], b_cmp2[l], pos_p)
        s_fin, o_r = _ret_prompt(r_q, r_k, r_v, log_g)
        xp = _finish(xp, _nsa_merge(gates, o_c, o_s, o_w), _ret_out(o_r, r_g), w_out[l], g_ffn[l], w_router[l], b_router[l], w_up[l], b_up[l], w_down[l], b_down[l])
        cmp_p.append(kv_cmp)
        sel_p.append(kv_sel)
        win_p.append(kv_win[:, seq - min(WINDOW, seq):])
        ret_p.append(s_fin)
        q, kv_cmp, kv_sel, kv_win, gates, r_q, r_k, r_v, r_g = _mixer_inputs(xs, pos_s, g_attn[l], w_in[l])
        o_c, o_s, o_w, new_win = _nsa_sample(q, kv_cmp, kv_sel, kv_win, cache_cmp_kv, cache_sel_kv, cache_win_kv[:, l], page_table, l, w_cmp1[l], b_cmp1[l], w_cmp2[l], b_cmp2[l], pos_s)
        s_new, o_r = _ret_chunk(state_ret[:, l].astype(jnp.float32), r_q.astype(jnp.float32), r_k.astype(jnp.float32), r_v.astype(jnp.float32), log_g)
        xs = _finish(xs, _nsa_merge(gates, o_c, o_s, o_w), _ret_out(o_r, r_g), w_out[l], g_ffn[l], w_router[l], b_router[l], w_up[l], b_up[l], w_down[l], b_down[l])
        cmp_s.append(kv_cmp)
        sel_s.append(kv_sel)
        win_s.append(new_win)
        ret_s.append(s_new)
    y_prompt = _final_norm(xp, g_final)
    y_sample = _final_norm(xs, g_final)
    return (y_prompt, y_sample, jnp.stack(cmp_p, axis=1), jnp.stack(sel_p, axis=1), jnp.stack(win_p, axis=1), jnp.stack(ret_p, axis=1), jnp.stack(cmp_s, axis=1), jnp.stack(sel_s, axis=1), jnp.stack(win_s, axis=1), jnp.stack(ret_s, axis=1))
```

```python
import functools
import jax, jax.numpy as jnp
from jax import lax
import numpy as np
from jax.experimental import pallas as pl
from jax.experimental.pallas import tpu as pltpu

D_MODEL = 1024
BATCH = 4
SEQ = 4096
DEPTH = 1
DEC_BATCH = 128
DEC_SEQ = 4
PAST_LEN = 8192
PAGE_SIZE = 128

HEAD_DIM = 64
N_NSA_HEADS = 8
N_KV_HEADS = 2
GQA_RATIO = N_NSA_HEADS // N_KV_HEADS
CMP_BLOCK = 32
CMP_STRIDE = 16
N_HALF = CMP_BLOCK // CMP_STRIDE
CMP_HIDDEN = 256
SEL_BLOCK = 64
N_SEL = 16
WINDOW = 512
NSA_Q_BLOCK = 64
WIN_Q_BLOCK = 128
N_RET_HEADS = 4
RET_DK = 64
RET_DV = 128
RET_CHUNK = 128
N_EXPERTS = 32
TOP_K = 4
D_FF = D_MODEL
SWIGLU_ALPHA = 1.702
SWIGLU_LIMIT = 7.0
MOE_BLOCK = 128
ROPE_THETA = 10000.0
RMS_EPS = 1e-5
NEG_INF = -1e30
FORCED_SCORE = 1e6
INVALID_SCORE = -1e9
NSA_WIDTH = N_NSA_HEADS * HEAD_DIM
KV_WIDTH = N_KV_HEADS * HEAD_DIM
RET_QK_WIDTH = N_RET_HEADS * RET_DK
RET_WIDTH = N_RET_HEADS * RET_DV
IN_SIZES = (NSA_WIDTH, KV_WIDTH, KV_WIDTH, KV_WIDTH, KV_WIDTH, KV_WIDTH, KV_WIDTH, 3 * N_NSA_HEADS, RET_QK_WIDTH, RET_QK_WIDTH, RET_WIDTH, RET_WIDTH)
IN_TOTAL = NSA_WIDTH + 6 * KV_WIDTH + 3 * N_NSA_HEADS + 2 * RET_QK_WIDTH + 2 * RET_WIDTH
MIX_WIDTH = NSA_WIDTH + RET_WIDTH
ATTN_SCALE = HEAD_DIM ** -0.5


def _rmsnorm(x, g):
    xf = x.astype(jnp.float32)
    y = xf * lax.rsqrt(jnp.mean(xf * xf, axis=-1, keepdims=True) + RMS_EPS)
    return (y * g.astype(jnp.float32)).astype(x.dtype)


def _rope(x, pos):
    half = x.shape[-1] // 2
    inv = ROPE_THETA ** (-jnp.arange(half, dtype=jnp.float32) / half)
    ang = pos.astype(jnp.float32)[:, None] * inv[None, :]
    cos = jnp.cos(ang)[None, :, None, :]
    sin = jnp.sin(ang)[None, :, None, :]
    xf = x.astype(jnp.float32)
    x1, x2 = xf[..., :half], xf[..., half:]
    return jnp.concatenate([x1 * cos - x2 * sin, x2 * cos + x1 * sin], axis=-1).astype(x.dtype)


def _masked_softmax(s, mask):
    s = jnp.where(mask, s, NEG_INF)
    m = jnp.max(s, axis=-1, keepdims=True)
    p = jnp.where(mask, jnp.exp(s - m), 0.0)
    return p / jnp.maximum(jnp.sum(p, axis=-1, keepdims=True), 1e-30)


def _mixer_inputs(x, pos, g_attn, w_in):
    b, t = x.shape[0], x.shape[1]
    h = _rmsnorm(x, g_attn)
    proj = jnp.einsum('btd,dc->btc', h, w_in)
    offs = [int(o) for o in np.cumsum(np.array(IN_SIZES))[:-1]]
    q, kc, vc, ks, vs, kw, vw, gt, rq, rk, rv, rg = jnp.split(proj, offs, axis=-1)

    def heads(a, n, d):
        return a.reshape(b, t, n, d)

    def kvh(a):
        return heads(a, N_KV_HEADS, HEAD_DIM)

    kv_cmp = jnp.stack([kvh(kc), kvh(vc)], axis=2)
    kv_sel = jnp.stack([_rope(kvh(ks), pos), kvh(vs)], axis=2)
    kv_win = jnp.stack([_rope(kvh(kw), pos), kvh(vw)], axis=2)
    gates = jax.nn.sigmoid(gt.astype(jnp.float32)).reshape(b, t, N_NSA_HEADS, 3)
    r_q = _rope(heads(rq, N_RET_HEADS, RET_DK), pos)
    r_k = _rope(heads(rk, N_RET_HEADS, RET_DK), pos) * (RET_DK ** -0.5)
    r_v = heads(rv, N_RET_HEADS, RET_DV)
    return heads(q, N_NSA_HEADS, HEAD_DIM), kv_cmp, kv_sel, kv_win, gates, r_q, r_k, r_v, rg, gt


def _segment_proj(kv_rows, w1):
    b, l = kv_rows.shape[0], kv_rows.shape[1]
    seg = kv_rows.reshape(b, l // CMP_STRIDE, CMP_STRIDE, 2, N_KV_HEADS, HEAD_DIM)
    return jnp.einsum('bsrcgd,chrdk->bscghk', seg, w1)


def _compress_blocks(p_seg, b1, w2, b2):
    n_cmp = p_seg.shape[1] - N_HALF + 1
    hid = sum(p_seg[:, h:h + n_cmp, :, :, h, :] for h in range(N_HALF)) + b1[None, None, :, None, :]
    hid = jax.nn.gelu(hid.astype(jnp.float32))
    out = jnp.einsum('bncgk,ckd->bncgd', hid, w2.astype(jnp.float32))
    return out + b2.astype(jnp.float32)[None, None, :, None, :]


def _cmp_attend(q, blocks, q_pos):
    b, nq = q.shape[0], q.shape[1]
    n_cmp = blocks.shape[1]
    qg = q.astype(jnp.float32).reshape(b, nq, N_KV_HEADS, GQA_RATIO, HEAD_DIM)
    s = jnp.einsum('bqgrd,bngd->bqgrn', qg, blocks[:, :, 0]) * ATTN_SCALE
    blk_end = jnp.arange(n_cmp) * CMP_STRIDE + (CMP_BLOCK - 1)
    mask = blk_end[None, :] <= q_pos[:, None]
    p = _masked_softmax(s, mask[None, :, None, None, :])
    o = jnp.einsum('bqgrn,bngd->bqgrd', p, blocks[:, :, 1])
    return o.reshape(b, nq, N_NSA_HEADS, HEAD_DIM), jnp.sum(p, axis=3)


def _select_blocks(p_grp, q_pos, n_sb):
    ratio = SEL_BLOCK // CMP_STRIDE
    ov = N_HALF - 1
    n_cmp = p_grp.shape[-1]
    pp = jnp.pad(p_grp, ((0, 0), (0, 0), (0, 0), (ov, ratio * n_sb - n_cmp)))
    score = sum(pp[..., o:o + ratio * n_sb:ratio] for o in range(ratio + ov))
    blk = jnp.arange(n_sb)[None, :]
    qp = q_pos[:, None]
    cur = qp // SEL_BLOCK
    forced = (blk == 0) | (blk == cur) | (blk == cur - 1)
    valid = blk * SEL_BLOCK <= qp
    score = jnp.where(forced[None, :, None, :], FORCED_SCORE, score)
    score = jnp.where(valid[None, :, None, :], score, INVALID_SCORE)
    _, idx = lax.top_k(score, min(N_SEL, n_sb))
    return idx


def _sel_attend(q_rot, kv_g, idx, q_pos):
    b, nq = q_rot.shape[0], q_rot.shape[1]
    m = idx.shape[-1] * SEL_BLOCK
    qg = q_rot.astype(jnp.float32).reshape(b, nq, N_KV_HEADS, GQA_RATIO, HEAD_DIM)
    k = kv_g[..., 0, :].astype(jnp.float32).reshape(b, nq, N_KV_HEADS, m, HEAD_DIM)
    v = kv_g[..., 1, :].astype(jnp.float32).reshape(b, nq, N_KV_HEADS, m, HEAD_DIM)
    s = jnp.einsum('bqgrd,bqgmd->bqgrm', qg, k) * ATTN_SCALE
    k_pos = (idx[..., None] * SEL_BLOCK + jnp.arange(SEL_BLOCK)).reshape(b, nq, N_KV_HEADS, m)
    mask = (k_pos <= q_pos[None, :, None, None])[:, :, :, None, :]
    p = _masked_softmax(s, mask)
    o = jnp.einsum('bqgrm,bqgmd->bqgrd', p, v)
    return o.reshape(b, nq, N_NSA_HEADS, HEAD_DIM)


def _win_prompt(q_rot, kv_win):
    b, t = q_rot.shape[0], q_rot.shape[1]
    n_qb = t // WIN_Q_BLOCK
    span = WINDOW + WIN_Q_BLOCK
    kvp = jnp.pad(kv_win, ((0, 0), (WINDOW, 0), (0, 0), (0, 0), (0, 0)))
    idx = jnp.arange(n_qb)[:, None] * WIN_Q_BLOCK + jnp.arange(span)[None, :]
    kb = kvp[:, idx].astype(jnp.float32)
    k_pos = (idx - WINDOW)[:, None, :]
    q_pos = jnp.arange(t).reshape(n_qb, WIN_Q_BLOCK)[:, :, None]
    qb = q_rot.astype(jnp.float32).reshape(b, n_qb, WIN_Q_BLOCK, N_KV_HEADS, GQA_RATIO, HEAD_DIM)
    s = jnp.einsum('bnqgrd,bnkgd->bnqgrk', qb, kb[:, :, :, 0]) * ATTN_SCALE
    mask = (k_pos <= q_pos) & (k_pos > q_pos - WINDOW) & (k_pos >= 0)
    p = _masked_softmax(s, mask[None, :, :, None, None, :])
    o = jnp.einsum('bnqgrk,bnkgd->bnqgrd', p, kb[:, :, :, 1])
    return o.reshape(b, t, N_NSA_HEADS, HEAD_DIM)


def _win_sample(q_rot, win_buf, kv_new, q_pos, past):
    b, s_len = q_rot.shape[0], q_rot.shape[1]
    wb = win_buf.shape[1]
    kv = jnp.concatenate([win_buf.astype(kv_new.dtype), kv_new], axis=1)
    k_pos = (past - wb + jnp.arange(wb + s_len))[None, :]
    qp = q_pos[:, None]
    qg = q_rot.astype(jnp.float32).reshape(b, s_len, N_KV_HEADS, GQA_RATIO, HEAD_DIM)
    s = jnp.einsum('bqgrd,bkgd->bqgrk', qg, kv[:, :, 0].astype(jnp.float32)) * ATTN_SCALE
    mask = (k_pos <= qp) & (k_pos > qp - WINDOW) & (k_pos >= 0)
    p = _masked_softmax(s, mask[None, :, None, None, :])
    o = jnp.einsum('bqgrk,bkgd->bqgrd', p, kv[:, :, 1].astype(jnp.float32))
    return o.reshape(b, s_len, N_NSA_HEADS, HEAD_DIM), kv[:, s_len:]


def _nsa_merge(gates, o_c, o_s, o_w):
    b, t = o_c.shape[0], o_c.shape[1]
    o = gates[..., 0:1] * o_c + gates[..., 1:2] * o_s + gates[..., 2:3] * o_w
    return o.reshape(b, t, NSA_WIDTH)


def _nsa_prompt(q, kv_cmp, kv_sel, kv_win, w1, b1, w2, b2, pos):
    b, t = q.shape[0], q.shape[1]
    q_rot = _rope(q, pos)
    blocks = _compress_blocks(_segment_proj(kv_cmp, w1), b1, w2, b2)
    n_sb = t // SEL_BLOCK
    kbs = kv_sel.reshape(b, n_sb, SEL_BLOCK, 2, N_KV_HEADS, HEAD_DIM).transpose(0, 4, 1, 2, 3, 5)
    bi = jnp.arange(b)[:, None, None, None]
    gi = jnp.arange(N_KV_HEADS)[None, None, :, None]
    n_qb = t // NSA_Q_BLOCK

    def q_block(args):
        qb, qrb, pb = args
        o_c, p_grp = _cmp_attend(qb, blocks, pb)
        idx = _select_blocks(p_grp, pb, n_sb)
        kv_g = kbs[bi, gi, idx]
        return o_c, _sel_attend(qrb, kv_g, idx, pb)

    def to_blocks(a):
        return a.reshape(b, n_qb, NSA_Q_BLOCK, N_NSA_HEADS, HEAD_DIM).swapaxes(0, 1)

    def from_blocks(a):
        return a.swapaxes(0, 1).reshape(b, t, N_NSA_HEADS, HEAD_DIM)

    o_c, o_s = lax.map(q_block, (to_blocks(q), to_blocks(q_rot), pos.reshape(n_qb, NSA_Q_BLOCK)))
    return from_blocks(o_c), from_blocks(o_s), _win_prompt(q_rot, kv_win)


def _nsa_sample(q, kv_cmp, kv_sel, kv_win, cache_cmp_kv, cache_sel_kv, win_buf, page_table, layer, w1, b1, w2, b2, pos):
    db, s_len = q.shape[0], q.shape[1]
    past = page_table.shape[1] * PAGE_SIZE
    q_rot = _rope(q, pos)
    past_cmp = cache_cmp_kv[page_table, layer].reshape(db, past, 2, N_KV_HEADS, HEAD_DIM)
    new_cmp = jnp.pad(kv_cmp, ((0, 0), (0, (-s_len) % CMP_STRIDE), (0, 0), (0, 0), (0, 0)))
    p_seg = jnp.concatenate([_segment_proj(past_cmp.astype(kv_cmp.dtype), w1), _segment_proj(new_cmp, w1)], axis=1)
    blocks = _compress_blocks(p_seg, b1, w2, b2)
    o_c, p_grp = _cmp_attend(q, blocks, pos)
    n_pb = past // SEL_BLOCK
    n_tail = -(-s_len // SEL_BLOCK)
    idx = _select_blocks(p_grp, pos, n_pb + n_tail)
    bi = jnp.arange(db)[:, None, None, None]
    gi = jnp.arange(N_KV_HEADS)[None, None, :, None]
    blocks_per_page = PAGE_SIZE // SEL_BLOCK
    blk_past = jnp.minimum(idx, n_pb - 1)
    page = page_table[bi, blk_past // blocks_per_page]
    row = (blk_past % blocks_per_page)[..., None] * SEL_BLOCK + jnp.arange(SEL_BLOCK)
    past_g = cache_sel_kv[page[..., None], layer, row, :, gi[..., None], :]
    tail = jnp.pad(kv_sel, ((0, 0), (0, n_tail * SEL_BLOCK - s_len), (0, 0), (0, 0), (0, 0)))
    tail = tail.reshape(db, n_tail, SEL_BLOCK, 2, N_KV_HEADS, HEAD_DIM).transpose(0, 4, 1, 2, 3, 5)
    tail_g = tail[bi, gi, jnp.clip(idx - n_pb, 0, n_tail - 1)]
    kv_g = jnp.where((idx >= n_pb)[..., None, None, None], tail_g, past_g.astype(tail_g.dtype))
    o_s = _sel_attend(q_rot, kv_g, idx, pos)
    o_w, new_win = _win_sample(q_rot, win_buf, kv_win, pos, past)
    return o_c, o_s, o_w, new_win


def _ret_log_decay():
    return jnp.log1p(-jnp.exp2(-5.0 - jnp.arange(N_RET_HEADS, dtype=jnp.float32)))


def _ret_chunk(state, q, k, v, log_g):
    c = q.shape[1]
    i = jnp.arange(c, dtype=jnp.float32)
    diff = i[:, None] - i[None, :]
    decay = jnp.where(diff[None] >= 0, jnp.exp(log_g[:, None, None] * jnp.maximum(diff, 0.0)[None]), 0.0)
    scores = jnp.einsum('bihd,bjhd->bhij', q, k) * decay[None]
    inner = jnp.einsum('bhij,bjhv->bihv', scores, v)
    q_dec = jnp.exp(log_g[None, :] * (i[:, None] + 1.0))
    cross = jnp.einsum('bihd,bhdv->bihv', q * q_dec[None, :, :, None], state)
    k_dec = jnp.exp(log_g[None, :] * (c - 1.0 - i[:, None]))
    new_state = jnp.exp(log_g * c)[None, :, None, None] * state + jnp.einsum('bjhd,bjhv->bhdv', k * k_dec[None, :, :, None], v)
    return new_state, inner + cross


def _ret_prompt(r_q, r_k, r_v, log_g):
    b, t = r_q.shape[0], r_q.shape[1]
    n_ch = t // RET_CHUNK

    def chunks(a):
        return a.astype(jnp.float32).reshape(b, n_ch, RET_CHUNK, a.shape[2], a.shape[3]).swapaxes(0, 1)

    s0 = jnp.zeros((b, N_RET_HEADS, RET_DK, RET_DV), jnp.float32)
    s_fin, o = lax.scan(lambda s, xs: _ret_chunk(s, xs[0], xs[1], xs[2], log_g), s0, (chunks(r_q), chunks(r_k), chunks(r_v)))
    return s_fin, o.swapaxes(0, 1).reshape(b, t, N_RET_HEADS, RET_DV)


def _ret_out(o, r_g):
    b, t = o.shape[0], o.shape[1]
    o = o * lax.rsqrt(jnp.mean(o * o, axis=-1, keepdims=True) + RMS_EPS)
    return o.reshape(b, t, RET_WIDTH) * jax.nn.silu(r_g.astype(jnp.float32))


def _moe(h, w_router, b_router, w_up, b_up, w_down, b_down):
    n_tok, d = h.shape
    logits = jnp.einsum('nd,de->ne', h, w_router).astype(jnp.float32) + b_router.astype(jnp.float32)
    top_val, top_idx = lax.top_k(logits, TOP_K)
    gate = jax.nn.softmax(top_val, axis=-1)
    n_assign = n_tok * TOP_K
    flat_e = top_idx.reshape(-1)
    order = jnp.argsort(flat_e)
    sorted_e = flat_e[order]
    tok = order // TOP_K
    counts = jnp.zeros((N_EXPERTS,), jnp.int32).at[flat_e].add(1)
    padded = (counts + MOE_BLOCK - 1) // MOE_BLOCK * MOE_BLOCK
    pad_end = jnp.cumsum(padded)
    pad_start = pad_end - padded
    start = jnp.cumsum(counts) - counts
    dest = pad_start[sorted_e] + jnp.arange(n_assign, dtype=jnp.int32) - start[sorted_e]
    n_blocks = -(-n_assign // MOE_BLOCK) + N_EXPERTS
    rows = jnp.zeros((n_blocks * MOE_BLOCK, d), h.dtype).at[dest].set(h[tok])
    blk_e = jnp.minimum(jnp.searchsorted(pad_end, jnp.arange(n_blocks, dtype=jnp.int32) * MOE_BLOCK, side='right'), N_EXPERTS - 1)

    def expert_block(args):
        xb, e = args
        u = xb @ w_up[e] + b_up[e]
        glu = jnp.minimum(u[:, :D_FF], SWIGLU_LIMIT)
        lin = jnp.clip(u[:, D_FF:], -SWIGLU_LIMIT, SWIGLU_LIMIT)
        a = glu * jax.nn.sigmoid(SWIGLU_ALPHA * glu) * (lin + 1.0)
        return a @ w_down[e] + b_down[e]

    out = lax.map(expert_block, (rows.reshape(n_blocks, MOE_BLOCK, d), blk_e)).reshape(-1, d)
    contrib = out[dest] * gate.reshape(-1)[order][:, None].astype(h.dtype)
    return jnp.zeros((n_tok, d), h.dtype).at[tok].add(contrib)


def _finish(x, o_nsa, o_ret, w_out, g_ffn, w_router, b_router, w_up, b_up, w_down, b_down):
    b, t, d = x.shape
    mix = jnp.concatenate([o_nsa, o_ret], axis=-1).astype(x.dtype)
    x = x + jnp.einsum('btc,cd->btd', mix, w_out)
    h = _rmsnorm(x, g_ffn).reshape(b * t, d)
    return x + _moe(h, w_router, b_router, w_up, b_up, w_down, b_down).reshape(b, t, d)


NSA_TQ = 128
NSA_TK = 128
LANES = 128
HIGHEST = lax.Precision.HIGHEST


def _dot_nt(a, b, precision=None):
    return lax.dot_general(a, b, (((1,), (1,)), ((), ())), precision=precision, preferred_element_type=jnp.float32)


def _nsa_prompt_kernel(qc_ref, qr_ref, kc_ref, vc_ref, ks_ref, vs_ref, kw_ref, vw_ref, gt_ref, o_ref,
                       m_scr, l_scr, acc_scr, *, seq):
    tq, tk = NSA_TQ, NSA_TK
    rows = GQA_RATIO * tq
    qt = pl.program_id(1)
    q0 = qt * tq
    ncp = kc_ref.shape[1]
    n_sb = seq // SEL_BLOCK
    ratio = SEL_BLOCK // CMP_STRIDE
    n_sel = min(N_SEL, n_sb)
    lane = lax.broadcasted_iota(jnp.int32, (tq, LANES), 1)
    gates = jax.nn.sigmoid(gt_ref[0])

    qpos_k = q0 + lax.broadcasted_iota(jnp.int32, (tq, tk), 0)
    kiota = lax.broadcasted_iota(jnp.int32, (tq, tk), 1)

    def attend(q, k_ref, v_ref, lo, hi, mask_fn):
        m_scr[...] = jnp.full((rows, 1), NEG_INF, jnp.float32)
        l_scr[...] = jnp.zeros((rows, 1), jnp.float32)
        acc_scr[...] = jnp.zeros((rows, LANES), jnp.float32)

        def body(kt, carry):
            k0 = pl.multiple_of(kt * tk, tk)
            k = k_ref[0, pl.ds(k0, tk), :]
            v = v_ref[0, pl.ds(k0, tk), :]
            s = _dot_nt(q, k).reshape(GQA_RATIO, tq, tk)
            mask = mask_fn(k0)[None]
            s = jnp.where(mask, s, NEG_INF)
            m_old = m_scr[...].reshape(GQA_RATIO, tq, 1)
            m_new = jnp.maximum(m_old, jnp.max(s, axis=-1, keepdims=True))
            alpha = jnp.exp(m_old - m_new)
            p = jnp.where(mask, jnp.exp(s - m_new), 0.0)
            l_new = alpha * l_scr[...].reshape(GQA_RATIO, tq, 1) + jnp.sum(p, axis=-1, keepdims=True)
            pv = jnp.dot(p.reshape(rows, tk).astype(jnp.bfloat16), v, preferred_element_type=jnp.float32)
            acc_scr[...] = alpha.reshape(rows, 1) * acc_scr[...] + pv
            m_scr[...] = m_new.reshape(rows, 1)
            l_scr[...] = l_new.reshape(rows, 1)
            return carry

        lax.fori_loop(lo, hi, body, 0)
        return acc_scr[...] / jnp.maximum(l_scr[...], 1e-30)

    for g in range(N_KV_HEADS):
        qc = qc_ref[0, g].reshape(rows, LANES)
        s = _dot_nt(qc, kc_ref[0], HIGHEST).reshape(GQA_RATIO, tq, ncp)
        qpos_c = q0 + lax.broadcasted_iota(jnp.int32, (tq, ncp), 0)
        blk_end = lax.broadcasted_iota(jnp.int32, (tq, ncp), 1) * CMP_STRIDE + (CMP_BLOCK - 1)
        cmask = (blk_end <= qpos_c)[None]
        s = jnp.where(cmask, s, NEG_INF)
        mx = jnp.max(s, axis=-1, keepdims=True)
        p = jnp.where(cmask, jnp.exp(s - mx), 0.0)
        p = p / jnp.maximum(jnp.sum(p, axis=-1, keepdims=True), 1e-30)
        o_c = jnp.dot(p.reshape(rows, ncp).astype(jnp.bfloat16), vc_ref[0].astype(jnp.bfloat16),
                      preferred_element_type=jnp.float32)
        p_grp = jnp.sum(p, axis=0)

        jj = lax.broadcasted_iota(jnp.int32, (n_sb, ncp), 0)
        nn = lax.broadcasted_iota(jnp.int32, (n_sb, ncp), 1)
        overlap = ((nn >= ratio * jj - (N_HALF - 1)) & (nn <= ratio * jj + ratio - 1)).astype(jnp.float32)
        score = _dot_nt(overlap, p_grp, HIGHEST)
        jt = lax.broadcasted_iota(jnp.int32, (n_sb, tq), 0)
        qpt = q0 + lax.broadcasted_iota(jnp.int32, (n_sb, tq), 1)
        cur = qpt // SEL_BLOCK
        forced = (jt == 0) | (jt == cur) | (jt == cur - 1)
        valid = jt * SEL_BLOCK <= qpt
        score = jnp.where(forced, FORCED_SCORE, score)
        score = jnp.where(valid, score, INVALID_SCORE)
        jf = jt.astype(jnp.float32)
        sel_t = jnp.zeros((n_sb, tq), jnp.float32)
        for _ in range(n_sel):
            best = jnp.max(score, axis=0, keepdims=True)
            first = jnp.min(jnp.where(score == best, jf, float(n_sb)), axis=0, keepdims=True)
            hit = jf == first
            sel_t = jnp.where(hit, 1.0, sel_t)
            score = jnp.where(hit, -3e38, score)
        if n_sb < LANES:
            sel_t = jnp.concatenate([sel_t, jnp.zeros((LANES - n_sb, tq), jnp.float32)], axis=0)
        sel = sel_t.T.astype(jnp.bfloat16)

        def sel_mask(k0):
            blk_of_key = (k0 + lax.broadcasted_iota(jnp.int32, (LANES, tk), 1)) // SEL_BLOCK
            expand = (blk_of_key == lax.broadcasted_iota(jnp.int32, (LANES, tk), 0)).astype(jnp.bfloat16)
            chosen = jnp.dot(sel, expand, preferred_element_type=jnp.float32)
            return (chosen > 0.5) & (k0 + kiota <= qpos_k)

        qr = qr_ref[0, g].reshape(rows, LANES)
        o_s = attend(qr, ks_ref, vs_ref, 0, qt + 1, sel_mask)

        def win_mask(k0):
            kpos = k0 + kiota
            return (kpos <= qpos_k) & (kpos > qpos_k - WINDOW)

        o_w = attend(qr, kw_ref, vw_ref, jnp.maximum(qt - WINDOW // tk, 0), qt + 1, win_mask)

        for r in range(GQA_RATIO):
            col = (g * GQA_RATIO + r) * 3
            sl = slice(r * tq, (r + 1) * tq)
            comb = (gates[:, col:col + 1] * o_c[sl] + gates[:, col + 1:col + 2] * o_s[sl]
                    + gates[:, col + 2:col + 3] * o_w[sl])
            if g == 0:
                o_ref[0, r] = comb
            else:
                o_ref[0, r] = jnp.where(lane < HEAD_DIM, o_ref[0, r], comb)


def _nsa_prompt_pallas(q, q_rot, blocks, kv_sel, kv_win, gt):
    b, t = q.shape[0], q.shape[1]
    assert t % NSA_TQ == 0 and NSA_TQ == NSA_TK and WINDOW % NSA_TK == 0 and t // SEL_BLOCK <= LANES
    ncp = t // CMP_STRIDE
    n_qt = t // NSA_TQ

    def group_pad(a, dtype):
        a = (a * ATTN_SCALE).reshape(b, t, N_KV_HEADS, GQA_RATIO, HEAD_DIM).transpose(0, 2, 3, 1, 4)
        eye = jnp.eye(N_KV_HEADS, dtype=a.dtype)[None, :, None, None, :, None]
        return (a[:, :, :, :, None, :] * eye).reshape(b, N_KV_HEADS, GQA_RATIO, t, LANES).astype(dtype)

    qc = group_pad(q, jnp.float32)
    qr = group_pad(q_rot, jnp.bfloat16)
    blk = jnp.pad(blocks, ((0, 0), (0, ncp - blocks.shape[1]), (0, 0), (0, 0), (0, 0))).reshape(b, ncp, 2 * LANES)
    ks = kv_sel.astype(jnp.bfloat16).reshape(b, t, 2 * LANES)
    kw = kv_win.astype(jnp.bfloat16).reshape(b, t, 2 * LANES)
    gtp = jnp.pad(gt.astype(jnp.float32), ((0, 0), (0, 0), (0, LANES - gt.shape[-1])))

    q_spec = pl.BlockSpec((1, N_KV_HEADS, GQA_RATIO, NSA_TQ, LANES), lambda i, j: (i, 0, 0, j, 0))

    def kv_spec(c, n):
        return pl.BlockSpec((1, n, LANES), lambda i, j: (i, 0, c))

    rows = GQA_RATIO * NSA_TQ
    out = pl.pallas_call(
        functools.partial(_nsa_prompt_kernel, seq=t),
        grid=(b, n_qt),
        in_specs=[q_spec, q_spec, kv_spec(0, ncp), kv_spec(1, ncp), kv_spec(0, t), kv_spec(1, t),
                  kv_spec(0, t), kv_spec(1, t), pl.BlockSpec((1, NSA_TQ, LANES), lambda i, j: (i, j, 0))],
        out_specs=pl.BlockSpec((1, GQA_RATIO, NSA_TQ, LANES), lambda i, j: (i, 0, j, 0)),
        out_shape=jax.ShapeDtypeStruct((b, GQA_RATIO, t, LANES), jnp.float32),
        scratch_shapes=[pltpu.VMEM((rows, 1), jnp.float32), pltpu.VMEM((rows, 1), jnp.float32),
                        pltpu.VMEM((rows, LANES), jnp.float32)],
        compiler_params=pltpu.CompilerParams(dimension_semantics=("arbitrary", "arbitrary")),
        name="nsa_prompt",
    )(qc, qr, blk, blk, ks, ks, kw, kw, gtp)
    out = out.reshape(b, GQA_RATIO, t, N_KV_HEADS, HEAD_DIM).transpose(0, 2, 3, 1, 4)
    return out.reshape(b, t, NSA_WIDTH)


def _final_norm_kernel(x_ref, g_ref, o_ref):
    x = x_ref[...]
    y = x * lax.rsqrt(jnp.mean(x * x, axis=-1, keepdims=True) + RMS_EPS)
    o_ref[...] = y * g_ref[...]


def _final_norm(x, g):
    b, t, d = x.shape
    n = b * t
    tile = 512
    out = pl.pallas_call(
        _final_norm_kernel,
        grid=(n // tile,),
        in_specs=[pl.BlockSpec((tile, d), lambda i: (i, 0)), pl.BlockSpec((1, d), lambda i: (0, 0))],
        out_specs=pl.BlockSpec((tile, d), lambda i: (i, 0)),
        out_shape=jax.ShapeDtypeStruct((n, d), x.dtype),
        name="final_norm",
    )(x.reshape(n, d), g.reshape(1, d))
    return out.reshape(b, t, d)


def kernel(x_prompt, x_sample, cache_cmp_kv, cache_sel_kv, cache_win_kv, state_ret, page_table, g_attn, w_in, w_cmp1, b_cmp1, w_cmp2, b_cmp2, w_out, g_ffn, w_router, b_router, w_up, b_up, w_down, b_down, g_final):
    seq = x_prompt.shape[1]
    past = page_table.shape[1] * PAGE_SIZE
    pos_p = jnp.arange(seq, dtype=jnp.int32)
    pos_s = past + jnp.arange(x_sample.shape[1], dtype=jnp.int32)
    log_g = _ret_log_decay()
    xp, xs = x_prompt, x_sample
    cmp_p, sel_p, win_p, ret_p = [], [], [], []
    cmp_s, sel_s, win_s, ret_s = [], [], [], []
    for l in range(DEPTH):
        q, kv_cmp, kv_sel, kv_win, gates, r_q, r_k, r_v, r_g, gt = _mixer_inputs(xp, pos_p, g_attn[l], w_in[l])
        blocks = _compress_blocks(_segment_proj(kv_cmp, w_cmp1[l]), b_cmp1[l], w_cmp2[l], b_cmp2[l])
        o_nsa = _nsa_prompt_pallas(q, _rope(q, pos_p), blocks, kv_sel, kv_win, gt)
        s_fin, o_r = _ret_prompt(r_q, r_k, r_v, log_g)
        xp = _finish(xp, o_nsa, _ret_out(o_r, r_g), w_out[l], g_ffn[l], w_router[l], b_router[l], w_up[l], b_up[l], w_down[l], b_down[l])
        cmp_p.append(kv_cmp)
        sel_p.append(kv_sel)
        win_p.append(kv_win[:, seq - min(WINDOW, seq):])
        ret_p.append(s_fin)
        q, kv_cmp, kv_sel, kv_win, gates, r_q, r_k, r_v, r_g, _ = _mixer_inputs(xs, pos_s, g_attn[l], w_in[l])
        o_c, o_s, o_w, new_win = _nsa_sample(q, kv_cmp, kv_sel, kv_win, cache_cmp_kv, cache_sel_kv, cache_win_kv[:, l], page_table, l, w_cmp1[l], b_cmp1[l], w_cmp2[l], b_cmp2[l], pos_s)
        s_new, o_r = _ret_chunk(state_ret[:, l].astype(jnp.float32), r_q.astype(jnp.float32), r_k.astype(jnp.float32), r_v.astype(jnp.float32), log_g)
        xs = _finish(xs, _nsa_merge(gates, o_c, o_s, o_w), _ret_out(o_r, r_g), w_out[l], g_ffn[l], w_router[l], b_router[l], w_up[l], b_up[l], w_down[l], b_down[l])
        cmp_s.append(kv_cmp)
        sel_s.append(kv_sel)
        win_s.append(new_win)
        ret_s.append(s_new)
    y_prompt = _final_norm(xp, g_final)
    y_sample = _final_norm(xs, g_final)
    return (y_prompt, y_sample, jnp.stack(cmp_p, axis=1), jnp.stack(sel_p, axis=1), jnp.stack(win_p, axis=1), jnp.stack(ret_p, axis=1), jnp.stack(cmp_s, axis=1), jnp.stack(sel_s, axis=1), jnp.stack(win_s, axis=1), jnp.stack(ret_s, axis=1))
```

```python
import functools
import jax, jax.numpy as jnp
from jax import lax
import numpy as np
from jax.experimental import pallas as pl
from jax.experimental.pallas import tpu as pltpu

D_MODEL = 1024
BATCH = 4
SEQ = 4096
DEPTH = 1
DEC_BATCH = 128
DEC_SEQ = 4
PAST_LEN = 8192
PAGE_SIZE = 128

HEAD_DIM = 64
N_NSA_HEADS = 8
N_KV_HEADS = 2
GQA_RATIO = N_NSA_HEADS // N_KV_HEADS
CMP_BLOCK = 32
CMP_STRIDE = 16
N_HALF = CMP_BLOCK // CMP_STRIDE
CMP_HIDDEN = 256
SEL_BLOCK = 64
N_SEL = 16
WINDOW = 512
NSA_Q_BLOCK = 64
WIN_Q_BLOCK = 128
N_RET_HEADS = 4
RET_DK = 64
RET_DV = 128
RET_CHUNK = 128
N_EXPERTS = 32
TOP_K = 4
D_FF = D_MODEL
SWIGLU_ALPHA = 1.702
SWIGLU_LIMIT = 7.0
MOE_BLOCK = 128
ROPE_THETA = 10000.0
RMS_EPS = 1e-5
NEG_INF = -1e30
FORCED_SCORE = 1e6
INVALID_SCORE = -1e9
NSA_WIDTH = N_NSA_HEADS * HEAD_DIM
KV_WIDTH = N_KV_HEADS * HEAD_DIM
RET_QK_WIDTH = N_RET_HEADS * RET_DK
RET_WIDTH = N_RET_HEADS * RET_DV
IN_SIZES = (NSA_WIDTH, KV_WIDTH, KV_WIDTH, KV_WIDTH, KV_WIDTH, KV_WIDTH, KV_WIDTH, 3 * N_NSA_HEADS, RET_QK_WIDTH, RET_QK_WIDTH, RET_WIDTH, RET_WIDTH)
IN_TOTAL = NSA_WIDTH + 6 * KV_WIDTH + 3 * N_NSA_HEADS + 2 * RET_QK_WIDTH + 2 * RET_WIDTH
MIX_WIDTH = NSA_WIDTH + RET_WIDTH
ATTN_SCALE = HEAD_DIM ** -0.5


def _rmsnorm(x, g):
    xf = x.astype(jnp.float32)
    y = xf * lax.rsqrt(jnp.mean(xf * xf, axis=-1, keepdims=True) + RMS_EPS)
    return (y * g.astype(jnp.float32)).astype(x.dtype)


def _rope(x, pos):
    half = x.shape[-1] // 2
    inv = ROPE_THETA ** (-jnp.arange(half, dtype=jnp.float32) / half)
    ang = pos.astype(jnp.float32)[:, None] * inv[None, :]
    cos = jnp.cos(ang)[None, :, None, :]
    sin = jnp.sin(ang)[None, :, None, :]
    xf = x.astype(jnp.float32)
    x1, x2 = xf[..., :half], xf[..., half:]
    return jnp.concatenate([x1 * cos - x2 * sin, x2 * cos + x1 * sin], axis=-1).astype(x.dtype)


def _masked_softmax(s, mask):
    s = jnp.where(mask, s, NEG_INF)
    m = jnp.max(s, axis=-1, keepdims=True)
    p = jnp.where(mask, jnp.exp(s - m), 0.0)
    return p / jnp.maximum(jnp.sum(p, axis=-1, keepdims=True), 1e-30)


def _mixer_inputs(x, pos, g_attn, w_in):
    b, t = x.shape[0], x.shape[1]
    h = _rmsnorm(x, g_attn)
    proj = jnp.einsum('btd,dc->btc', h, w_in)
    offs = [int(o) for o in np.cumsum(np.array(IN_SIZES))[:-1]]
    q, kc, vc, ks, vs, kw, vw, gt, rq, rk, rv, rg = jnp.split(proj, offs, axis=-1)

    def heads(a, n, d):
        return a.reshape(b, t, n, d)

    def kvh(a):
        return heads(a, N_KV_HEADS, HEAD_DIM)

    kv_cmp = jnp.stack([kvh(kc), kvh(vc)], axis=2)
    kv_sel = jnp.stack([_rope(kvh(ks), pos), kvh(vs)], axis=2)
    kv_win = jnp.stack([_rope(kvh(kw), pos), kvh(vw)], axis=2)
    gates = jax.nn.sigmoid(gt.astype(jnp.float32)).reshape(b, t, N_NSA_HEADS, 3)
    r_q = _rope(heads(rq, N_RET_HEADS, RET_DK), pos)
    r_k = _rope(heads(rk, N_RET_HEADS, RET_DK), pos) * (RET_DK ** -0.5)
    r_v = heads(rv, N_RET_HEADS, RET_DV)
    return heads(q, N_NSA_HEADS, HEAD_DIM), kv_cmp, kv_sel, kv_win, gates, r_q, r_k, r_v, rg, gt


def _segment_proj(kv_rows, w1):
    b, l = kv_rows.shape[0], kv_rows.shape[1]
    seg = kv_rows.reshape(b, l // CMP_STRIDE, CMP_STRIDE, 2, N_KV_HEADS, HEAD_DIM)
    return jnp.einsum('bsrcgd,chrdk->bscghk', seg, w1)


def _compress_blocks(p_seg, b1, w2, b2):
    n_cmp = p_seg.shape[1] - N_HALF + 1
    hid = sum(p_seg[:, h:h + n_cmp, :, :, h, :] for h in range(N_HALF)) + b1[None, None, :, None, :]
    hid = jax.nn.gelu(hid.astype(jnp.float32))
    out = jnp.einsum('bncgk,ckd->bncgd', hid, w2.astype(jnp.float32))
    return out + b2.astype(jnp.float32)[None, None, :, None, :]


def _cmp_attend(q, blocks, q_pos):
    b, nq = q.shape[0], q.shape[1]
    n_cmp = blocks.shape[1]
    qg = q.astype(jnp.float32).reshape(b, nq, N_KV_HEADS, GQA_RATIO, HEAD_DIM)
    s = jnp.einsum('bqgrd,bngd->bqgrn', qg, blocks[:, :, 0]) * ATTN_SCALE
    blk_end = jnp.arange(n_cmp) * CMP_STRIDE + (CMP_BLOCK - 1)
    mask = blk_end[None, :] <= q_pos[:, None]
    p = _masked_softmax(s, mask[None, :, None, None, :])
    o = jnp.einsum('bqgrn,bngd->bqgrd', p, blocks[:, :, 1])
    return o.reshape(b, nq, N_NSA_HEADS, HEAD_DIM), jnp.sum(p, axis=3)


def _select_blocks(p_grp, q_pos, n_sb):
    ratio = SEL_BLOCK // CMP_STRIDE
    ov = N_HALF - 1
    n_cmp = p_grp.shape[-1]
    pp = jnp.pad(p_grp, ((0, 0), (0, 0), (0, 0), (ov, ratio * n_sb - n_cmp)))
    score = sum(pp[..., o:o + ratio * n_sb:ratio] for o in range(ratio + ov))
    blk = jnp.arange(n_sb)[None, :]
    qp = q_pos[:, None]
    cur = qp // SEL_BLOCK
    forced = (blk == 0) | (blk == cur) | (blk == cur - 1)
    valid = blk * SEL_BLOCK <= qp
    score = jnp.where(forced[None, :, None, :], FORCED_SCORE, score)
    score = jnp.where(valid[None, :, None, :], score, INVALID_SCORE)
    _, idx = lax.top_k(score, min(N_SEL, n_sb))
    return idx


def _sel_attend(q_rot, kv_g, idx, q_pos):
    b, nq = q_rot.shape[0], q_rot.shape[1]
    m = idx.shape[-1] * SEL_BLOCK
    qg = q_rot.astype(jnp.float32).reshape(b, nq, N_KV_HEADS, GQA_RATIO, HEAD_DIM)
    k = kv_g[..., 0, :].astype(jnp.float32).reshape(b, nq, N_KV_HEADS, m, HEAD_DIM)
    v = kv_g[..., 1, :].astype(jnp.float32).reshape(b, nq, N_KV_HEADS, m, HEAD_DIM)
    s = jnp.einsum('bqgrd,bqgmd->bqgrm', qg, k) * ATTN_SCALE
    k_pos = (idx[..., None] * SEL_BLOCK + jnp.arange(SEL_BLOCK)).reshape(b, nq, N_KV_HEADS, m)
    mask = (k_pos <= q_pos[None, :, None, None])[:, :, :, None, :]
    p = _masked_softmax(s, mask)
    o = jnp.einsum('bqgrm,bqgmd->bqgrd', p, v)
    return o.reshape(b, nq, N_NSA_HEADS, HEAD_DIM)


def _win_prompt(q_rot, kv_win):
    b, t = q_rot.shape[0], q_rot.shape[1]
    n_qb = t // WIN_Q_BLOCK
    span = WINDOW + WIN_Q_BLOCK
    kvp = jnp.pad(kv_win, ((0, 0), (WINDOW, 0), (0, 0), (0, 0), (0, 0)))
    idx = jnp.arange(n_qb)[:, None] * WIN_Q_BLOCK + jnp.arange(span)[None, :]
    kb = kvp[:, idx].astype(jnp.float32)
    k_pos = (idx - WINDOW)[:, None, :]
    q_pos = jnp.arange(t).reshape(n_qb, WIN_Q_BLOCK)[:, :, None]
    qb = q_rot.astype(jnp.float32).reshape(b, n_qb, WIN_Q_BLOCK, N_KV_HEADS, GQA_RATIO, HEAD_DIM)
    s = jnp.einsum('bnqgrd,bnkgd->bnqgrk', qb, kb[:, :, :, 0]) * ATTN_SCALE
    mask = (k_pos <= q_pos) & (k_pos > q_pos - WINDOW) & (k_pos >= 0)
    p = _masked_softmax(s, mask[None, :, :, None, None, :])
    o = jnp.einsum('bnqgrk,bnkgd->bnqgrd', p, kb[:, :, :, 1])
    return o.reshape(b, t, N_NSA_HEADS, HEAD_DIM)


def _win_sample(q_rot, win_buf, kv_new, q_pos, past):
    b, s_len = q_rot.shape[0], q_rot.shape[1]
    wb = win_buf.shape[1]
    kv = jnp.concatenate([win_buf.astype(kv_new.dtype), kv_new], axis=1)
    k_pos = (past - wb + jnp.arange(wb + s_len))[None, :]
    qp = q_pos[:, None]
    qg = q_rot.astype(jnp.float32).reshape(b, s_len, N_KV_HEADS, GQA_RATIO, HEAD_DIM)
    s = jnp.einsum('bqgrd,bkgd->bqgrk', qg, kv[:, :, 0].astype(jnp.float32)) * ATTN_SCALE
    mask = (k_pos <= qp) & (k_pos > qp - WINDOW) & (k_pos >= 0)
    p = _masked_softmax(s, mask[None, :, None, None, :])
    o = jnp.einsum('bqgrk,bkgd->bqgrd', p, kv[:, :, 1].astype(jnp.float32))
    return o.reshape(b, s_len, N_NSA_HEADS, HEAD_DIM), kv[:, s_len:]


def _nsa_merge(gates, o_c, o_s, o_w):
    b, t = o_c.shape[0], o_c.shape[1]
    o = gates[..., 0:1] * o_c + gates[..., 1:2] * o_s + gates[..., 2:3] * o_w
    return o.reshape(b, t, NSA_WIDTH)


def _nsa_prompt(q, kv_cmp, kv_sel, kv_win, w1, b1, w2, b2, pos):
    b, t = q.shape[0], q.shape[1]
    q_rot = _rope(q, pos)
    blocks = _compress_blocks(_segment_proj(kv_cmp, w1), b1, w2, b2)
    n_sb = t // SEL_BLOCK
    kbs = kv_sel.reshape(b, n_sb, SEL_BLOCK, 2, N_KV_HEADS, HEAD_DIM).transpose(0, 4, 1, 2, 3, 5)
    bi = jnp.arange(b)[:, None, None, None]
    gi = jnp.arange(N_KV_HEADS)[None, None, :, None]
    n_qb = t // NSA_Q_BLOCK

    def q_block(args):
        qb, qrb, pb = args
        o_c, p_grp = _cmp_attend(qb, blocks, pb)
        idx = _select_blocks(p_grp, pb, n_sb)
        kv_g = kbs[bi, gi, idx]
        return o_c, _sel_attend(qrb, kv_g, idx, pb)

    def to_blocks(a):
        return a.reshape(b, n_qb, NSA_Q_BLOCK, N_NSA_HEADS, HEAD_DIM).swapaxes(0, 1)

    def from_blocks(a):
        return a.swapaxes(0, 1).reshape(b, t, N_NSA_HEADS, HEAD_DIM)

    o_c, o_s = lax.map(q_block, (to_blocks(q), to_blocks(q_rot), pos.reshape(n_qb, NSA_Q_BLOCK)))
    return from_blocks(o_c), from_blocks(o_s), _win_prompt(q_rot, kv_win)


def _nsa_sample(q, kv_cmp, kv_sel, kv_win, cache_cmp_kv, cache_sel_kv, win_buf, page_table, layer, w1, b1, w2, b2, pos):
    db, s_len = q.shape[0], q.shape[1]
    past = page_table.shape[1] * PAGE_SIZE
    q_rot = _rope(q, pos)
    past_cmp = cache_cmp_kv[page_table, layer].reshape(db, past, 2, N_KV_HEADS, HEAD_DIM)
    new_cmp = jnp.pad(kv_cmp, ((0, 0), (0, (-s_len) % CMP_STRIDE), (0, 0), (0, 0), (0, 0)))
    p_seg = jnp.concatenate([_segment_proj(past_cmp.astype(kv_cmp.dtype), w1), _segment_proj(new_cmp, w1)], axis=1)
    blocks = _compress_blocks(p_seg, b1, w2, b2)
    o_c, p_grp = _cmp_attend(q, blocks, pos)
    n_pb = past // SEL_BLOCK
    n_tail = -(-s_len // SEL_BLOCK)
    idx = _select_blocks(p_grp, pos, n_pb + n_tail)
    bi = jnp.arange(db)[:, None, None, None]
    gi = jnp.arange(N_KV_HEADS)[None, None, :, None]
    blocks_per_page = PAGE_SIZE // SEL_BLOCK
    blk_past = jnp.minimum(idx, n_pb - 1)
    page = page_table[bi, blk_past // blocks_per_page]
    row = (blk_past % blocks_per_page)[..., None] * SEL_BLOCK + jnp.arange(SEL_BLOCK)
    past_g = cache_sel_kv[page[..., None], layer, row, :, gi[..., None], :]
    tail = jnp.pad(kv_sel, ((0, 0), (0, n_tail * SEL_BLOCK - s_len), (0, 0), (0, 0), (0, 0)))
    tail = tail.reshape(db, n_tail, SEL_BLOCK, 2, N_KV_HEADS, HEAD_DIM).transpose(0, 4, 1, 2, 3, 5)
    tail_g = tail[bi, gi, jnp.clip(idx - n_pb, 0, n_tail - 1)]
    kv_g = jnp.where((idx >= n_pb)[..., None, None, None], tail_g, past_g.astype(tail_g.dtype))
    o_s = _sel_attend(q_rot, kv_g, idx, pos)
    o_w, new_win = _win_sample(q_rot, win_buf, kv_win, pos, past)
    return o_c, o_s, o_w, new_win


def _ret_log_decay():
    return jnp.log1p(-jnp.exp2(-5.0 - jnp.arange(N_RET_HEADS, dtype=jnp.float32)))


def _ret_chunk(state, q, k, v, log_g):
    c = q.shape[1]
    i = jnp.arange(c, dtype=jnp.float32)
    diff = i[:, None] - i[None, :]
    decay = jnp.where(diff[None] >= 0, jnp.exp(log_g[:, None, None] * jnp.maximum(diff, 0.0)[None]), 0.0)
    scores = jnp.einsum('bihd,bjhd->bhij', q, k) * decay[None]
    inner = jnp.einsum('bhij,bjhv->bihv', scores, v)
    q_dec = jnp.exp(log_g[None, :] * (i[:, None] + 1.0))
    cross = jnp.einsum('bihd,bhdv->bihv', q * q_dec[None, :, :, None], state)
    k_dec = jnp.exp(log_g[None, :] * (c - 1.0 - i[:, None]))
    new_state = jnp.exp(log_g * c)[None, :, None, None] * state + jnp.einsum('bjhd,bjhv->bhdv', k * k_dec[None, :, :, None], v)
    return new_state, inner + cross


def _ret_prompt(r_q, r_k, r_v, log_g):
    b, t = r_q.shape[0], r_q.shape[1]
    n_ch = t // RET_CHUNK

    def chunks(a):
        return a.astype(jnp.float32).reshape(b, n_ch, RET_CHUNK, a.shape[2], a.shape[3]).swapaxes(0, 1)

    s0 = jnp.zeros((b, N_RET_HEADS, RET_DK, RET_DV), jnp.float32)
    s_fin, o = lax.scan(lambda s, xs: _ret_chunk(s, xs[0], xs[1], xs[2], log_g), s0, (chunks(r_q), chunks(r_k), chunks(r_v)))
    return s_fin, o.swapaxes(0, 1).reshape(b, t, N_RET_HEADS, RET_DV)


def _ret_out(o, r_g):
    b, t = o.shape[0], o.shape[1]
    o = o * lax.rsqrt(jnp.mean(o * o, axis=-1, keepdims=True) + RMS_EPS)
    return o.reshape(b, t, RET_WIDTH) * jax.nn.silu(r_g.astype(jnp.float32))


def _moe(h, w_router, b_router, w_up, b_up, w_down, b_down):
    n_tok, d = h.shape
    logits = jnp.einsum('nd,de->ne', h, w_router).astype(jnp.float32) + b_router.astype(jnp.float32)
    top_val, top_idx = lax.top_k(logits, TOP_K)
    gate = jax.nn.softmax(top_val, axis=-1)
    n_assign = n_tok * TOP_K
    flat_e = top_idx.reshape(-1)
    order = jnp.argsort(flat_e)
    sorted_e = flat_e[order]
    tok = order // TOP_K
    counts = jnp.zeros((N_EXPERTS,), jnp.int32).at[flat_e].add(1)
    padded = (counts + MOE_BLOCK - 1) // MOE_BLOCK * MOE_BLOCK
    pad_end = jnp.cumsum(padded)
    pad_start = pad_end - padded
    start = jnp.cumsum(counts) - counts
    dest = pad_start[sorted_e] + jnp.arange(n_assign, dtype=jnp.int32) - start[sorted_e]
    n_blocks = -(-n_assign // MOE_BLOCK) + N_EXPERTS
    rows = jnp.zeros((n_blocks * MOE_BLOCK, d), h.dtype).at[dest].set(h[tok])
    blk_e = jnp.minimum(jnp.searchsorted(pad_end, jnp.arange(n_blocks, dtype=jnp.int32) * MOE_BLOCK, side='right'), N_EXPERTS - 1)

    def expert_block(args):
        xb, e = args
        u = xb @ w_up[e] + b_up[e]
        glu = jnp.minimum(u[:, :D_FF], SWIGLU_LIMIT)
        lin = jnp.clip(u[:, D_FF:], -SWIGLU_LIMIT, SWIGLU_LIMIT)
        a = glu * jax.nn.sigmoid(SWIGLU_ALPHA * glu) * (lin + 1.0)
        return a @ w_down[e] + b_down[e]

    out = lax.map(expert_block, (rows.reshape(n_blocks, MOE_BLOCK, d), blk_e)).reshape(-1, d)
    contrib = out[dest] * gate.reshape(-1)[order][:, None].astype(h.dtype)
    return jnp.zeros((n_tok, d), h.dtype).at[tok].add(contrib)


def _finish(x, o_nsa, o_ret, w_out, g_ffn, w_router, b_router, w_up, b_up, w_down, b_down):
    b, t, d = x.shape
    mix = jnp.concatenate([o_nsa, o_ret], axis=-1).astype(x.dtype)
    x = x + jnp.einsum('btc,cd->btd', mix, w_out)
    h = _rmsnorm(x, g_ffn).reshape(b * t, d)
    return x + _moe(h, w_router, b_router, w_up, b_up, w_down, b_down).reshape(b, t, d)


NSA_TQ = 128
NSA_TK = 128
LANES = 128
HIGHEST = lax.Precision.HIGHEST


def _dot_nt(a, b, precision=None):
    return lax.dot_general(a, b, (((1,), (1,)), ((), ())), precision=precision, preferred_element_type=jnp.float32)


def _nsa_prompt_kernel(qc_ref, qr_ref, kc_ref, vc_ref, ks_ref, vs_ref, kw_ref, vw_ref, gt_ref, o_ref,
                       m_scr, l_scr, acc_scr, *, seq):
    tq, tk = NSA_TQ, NSA_TK
    rows = GQA_RATIO * tq
    qt = pl.program_id(1)
    q0 = qt * tq
    ncp = kc_ref.shape[1]
    n_sb = seq // SEL_BLOCK
    ratio = SEL_BLOCK // CMP_STRIDE
    n_sel = min(N_SEL, n_sb)
    lane = lax.broadcasted_iota(jnp.int32, (tq, LANES), 1)
    gates = jax.nn.sigmoid(gt_ref[0])

    qpos_k = q0 + lax.broadcasted_iota(jnp.int32, (tq, tk), 0)
    kiota = lax.broadcasted_iota(jnp.int32, (tq, tk), 1)

    def attend(q, k_ref, v_ref, lo, hi, mask_fn):
        m_scr[...] = jnp.full((rows, 1), NEG_INF, jnp.float32)
        l_scr[...] = jnp.zeros((rows, 1), jnp.float32)
        acc_scr[...] = jnp.zeros((rows, LANES), jnp.float32)

        def body(kt, carry):
            k0 = pl.multiple_of(kt * tk, tk)
            k = k_ref[0, pl.ds(k0, tk), :]
            v = v_ref[0, pl.ds(k0, tk), :]
            s = _dot_nt(q, k).reshape(GQA_RATIO, tq, tk)
            mask = mask_fn(k0)[None]
            s = jnp.where(mask, s, NEG_INF)
            m_old = m_scr[...].reshape(GQA_RATIO, tq, 1)
            m_new = jnp.maximum(m_old, jnp.max(s, axis=-1, keepdims=True))
            alpha = jnp.exp(m_old - m_new)
            p = jnp.where(mask, jnp.exp(s - m_new), 0.0)
            l_new = alpha * l_scr[...].reshape(GQA_RATIO, tq, 1) + jnp.sum(p, axis=-1, keepdims=True)
            pv = jnp.dot(p.reshape(rows, tk).astype(jnp.bfloat16), v, preferred_element_type=jnp.float32)
            acc_scr[...] = alpha.reshape(rows, 1) * acc_scr[...] + pv
            m_scr[...] = m_new.reshape(rows, 1)
            l_scr[...] = l_new.reshape(rows, 1)
            return carry

        lax.fori_loop(lo, hi, body, 0)
        return acc_scr[...] / jnp.maximum(l_scr[...], 1e-30)

    for g in range(N_KV_HEADS):
        qc = qc_ref[0, g].reshape(rows, LANES)
        s = _dot_nt(qc, kc_ref[0], HIGHEST).reshape(GQA_RATIO, tq, ncp)
        qpos_c = q0 + lax.broadcasted_iota(jnp.int32, (tq, ncp), 0)
        blk_end = lax.broadcasted_iota(jnp.int32, (tq, ncp), 1) * CMP_STRIDE + (CMP_BLOCK - 1)
        cmask = (blk_end <= qpos_c)[None]
        s = jnp.where(cmask, s, NEG_INF)
        mx = jnp.max(s, axis=-1, keepdims=True)
        p = jnp.where(cmask, jnp.exp(s - mx), 0.0)
        p = p / jnp.maximum(jnp.sum(p, axis=-1, keepdims=True), 1e-30)
        o_c = jnp.dot(p.reshape(rows, ncp).astype(jnp.bfloat16), vc_ref[0].astype(jnp.bfloat16),
                      preferred_element_type=jnp.float32)
        p_grp = jnp.sum(p, axis=0)

        jj = lax.broadcasted_iota(jnp.int32, (n_sb, ncp), 0)
        nn = lax.broadcasted_iota(jnp.int32, (n_sb, ncp), 1)
        overlap = ((nn >= ratio * jj - (N_HALF - 1)) & (nn <= ratio * jj + ratio - 1)).astype(jnp.float32)
        score = _dot_nt(overlap, p_grp, HIGHEST)
        jt = lax.broadcasted_iota(jnp.int32, (n_sb, tq), 0)
        qpt = q0 + lax.broadcasted_iota(jnp.int32, (n_sb, tq), 1)
        cur = qpt // SEL_BLOCK
        forced = (jt == 0) | (jt == cur) | (jt == cur - 1)
        valid = jt * SEL_BLOCK <= qpt
        score = jnp.where(forced, FORCED_SCORE, score)
        score = jnp.where(valid, score, INVALID_SCORE)
        jf = jt.astype(jnp.float32)
        sel_t = jnp.zeros((n_sb, tq), jnp.float32)
        for _ in range(n_sel):
            best = jnp.max(score, axis=0, keepdims=True)
            first = jnp.min(jnp.where(score == best, jf, float(n_sb)), axis=0, keepdims=True)
            hit = jf == first
            sel_t = jnp.where(hit, 1.0, sel_t)
            score = jnp.where(hit, -3e38, score)
        if n_sb < LANES:
            sel_t = jnp.concatenate([sel_t, jnp.zeros((LANES - n_sb, tq), jnp.float32)], axis=0)
        sel = sel_t.T.astype(jnp.bfloat16)

        def sel_mask(k0):
            blk_of_key = (k0 + lax.broadcasted_iota(jnp.int32, (LANES, tk), 1)) // SEL_BLOCK
            expand = (blk_of_key == lax.broadcasted_iota(jnp.int32, (LANES, tk), 0)).astype(jnp.bfloat16)
            chosen = jnp.dot(sel, expand, preferred_element_type=jnp.float32)
            return (chosen > 0.5) & (k0 + kiota <= qpos_k)

        qr = qr_ref[0, g].reshape(rows, LANES)
        o_s = attend(qr, ks_ref, vs_ref, 0, qt + 1, sel_mask)

        def win_mask(k0):
            kpos = k0 + kiota
            return (kpos <= qpos_k) & (kpos > qpos_k - WINDOW)

        o_w = attend(qr, kw_ref, vw_ref, jnp.maximum(qt - WINDOW // tk, 0), qt + 1, win_mask)

        for r in range(GQA_RATIO):
            col = (g * GQA_RATIO + r) * 3
            sl = slice(r * tq, (r + 1) * tq)
            comb = (gates[:, col:col + 1] * o_c[sl] + gates[:, col + 1:col + 2] * o_s[sl]
                    + gates[:, col + 2:col + 3] * o_w[sl])
            if g == 0:
                o_ref[0, r] = comb
            else:
                o_ref[0, r] = jnp.where(lane < HEAD_DIM, o_ref[0, r], comb)


def _nsa_prompt_pallas(q, q_rot, blocks, kv_sel, kv_win, gt):
    b, t = q.shape[0], q.shape[1]
    assert t % NSA_TQ == 0 and NSA_TQ == NSA_TK and WINDOW % NSA_TK == 0 and t // SEL_BLOCK <= LANES
    ncp = t // CMP_STRIDE
    n_qt = t // NSA_TQ

    def group_pad(a, dtype):
        a = (a * ATTN_SCALE).reshape(b, t, N_KV_HEADS, GQA_RATIO, HEAD_DIM).transpose(0, 2, 3, 1, 4)
        eye = jnp.eye(N_KV_HEADS, dtype=a.dtype)[None, :, None, None, :, None]
        return (a[:, :, :, :, None, :] * eye).reshape(b, N_KV_HEADS, GQA_RATIO, t, LANES).astype(dtype)

    qc = group_pad(q, jnp.float32)
    qr = group_pad(q_rot, jnp.bfloat16)
    blk = jnp.pad(blocks, ((0, 0), (0, ncp - blocks.shape[1]), (0, 0), (0, 0), (0, 0))).reshape(b, ncp, 2 * LANES)
    ks = kv_sel.astype(jnp.bfloat16).reshape(b, t, 2 * LANES)
    kw = kv_win.astype(jnp.bfloat16).reshape(b, t, 2 * LANES)
    gtp = jnp.pad(gt.astype(jnp.float32), ((0, 0), (0, 0), (0, LANES - gt.shape[-1])))

    q_spec = pl.BlockSpec((1, N_KV_HEADS, GQA_RATIO, NSA_TQ, LANES), lambda i, j: (i, 0, 0, j, 0))

    def kv_spec(c, n):
        return pl.BlockSpec((1, n, LANES), lambda i, j: (i, 0, c))

    rows = GQA_RATIO * NSA_TQ
    out = pl.pallas_call(
        functools.partial(_nsa_prompt_kernel, seq=t),
        grid=(b, n_qt),
        in_specs=[q_spec, q_spec, kv_spec(0, ncp), kv_spec(1, ncp), kv_spec(0, t), kv_spec(1, t),
                  kv_spec(0, t), kv_spec(1, t), pl.BlockSpec((1, NSA_TQ, LANES), lambda i, j: (i, j, 0))],
        out_specs=pl.BlockSpec((1, GQA_RATIO, NSA_TQ, LANES), lambda i, j: (i, 0, j, 0)),
        out_shape=jax.ShapeDtypeStruct((b, GQA_RATIO, t, LANES), jnp.float32),
        scratch_shapes=[pltpu.VMEM((rows, 1), jnp.float32), pltpu.VMEM((rows, 1), jnp.float32),
                        pltpu.VMEM((rows, LANES), jnp.float32)],
        compiler_params=pltpu.CompilerParams(dimension_semantics=("arbitrary", "arbitrary")),
        name="nsa_prompt",
    )(qc, qr, blk, blk, ks, ks, kw, kw, gtp)
    out = out.reshape(b, GQA_RATIO, t, N_KV_HEADS, HEAD_DIM).transpose(0, 2, 3, 1, 4)
    return out.reshape(b, t, NSA_WIDTH)


ROUTER_TT = 256
MOE_BM = 256


def _router_kernel(x_ref, mix_ref, wout_ref, g_ref, wrt_ref, br_ref,
                   x1_ref, h_ref, eidx_ref, gate_ref, rank_ref, cnt_ref, wout_bf, run_scr):
    tt = x_ref.shape[0]
    n_e = wrt_ref.shape[0]

    @pl.when(pl.program_id(0) == 0)
    def _():
        wout_bf[...] = wout_ref[...].astype(jnp.bfloat16)
        run_scr[...] = jnp.zeros_like(run_scr)

    x1 = x_ref[...] + jnp.dot(mix_ref[...].astype(jnp.bfloat16), wout_bf[...], preferred_element_type=jnp.float32)
    x1_ref[...] = x1
    hn = x1 * lax.rsqrt(jnp.mean(x1 * x1, axis=-1, keepdims=True) + RMS_EPS) * g_ref[...]
    h_ref[...] = hn.astype(jnp.bfloat16)

    score = _dot_nt(wrt_ref[...], hn, HIGHEST) + br_ref[...]
    ef = lax.broadcasted_iota(jnp.int32, (n_e, tt), 0).astype(jnp.float32)
    vals, hits = [], []
    for k in range(TOP_K):
        best = jnp.max(score, axis=0, keepdims=True)
        first = jnp.min(jnp.where(score == best, ef, float(n_e)), axis=0, keepdims=True)
        hit = ef == first
        vals.append(best)
        hits.append(hit)
        eidx_ref[k:k + 1, :] = first.astype(jnp.int32)
        score = jnp.where(hit, -3e38, score)
    exps = [jnp.exp(v - vals[0]) for v in vals]
    denom = sum(exps[1:], exps[0])
    for k in range(TOP_K):
        gate_ref[k:k + 1, :] = exps[k] / denom

    chosen = functools.reduce(jnp.logical_or, hits)
    before = (lax.broadcasted_iota(jnp.int32, (tt, tt), 0) < lax.broadcasted_iota(jnp.int32, (tt, tt), 1))
    earlier = jnp.dot(chosen.astype(jnp.bfloat16), before.astype(jnp.bfloat16), preferred_element_type=jnp.float32)
    pos = earlier + run_scr[...]
    for k in range(TOP_K):
        rank_ref[k:k + 1, :] = jnp.sum(jnp.where(hits[k], pos, 0.0), axis=0, keepdims=True).astype(jnp.int32)
    run_scr[...] = run_scr[...] + jnp.sum(chosen.astype(jnp.float32), axis=1, keepdims=True)
    cnt_ref[...] = jnp.broadcast_to(run_scr[...], cnt_ref.shape).astype(jnp.int32)


def _router_pallas(x, mix, w_out, g_ffn, w_router, b_router):
    n, d = x.shape
    c = mix.shape[1]
    n_e = w_router.shape[1]
    tt = ROUTER_TT
    assert n % tt == 0

    def row(w):
        return pl.BlockSpec((tt, w), lambda i: (i, 0))

    def full(a, b):
        return pl.BlockSpec((a, b), lambda i: (0, 0))

    k4 = pl.BlockSpec((TOP_K, tt), lambda i: (0, i))
    return pl.pallas_call(
        _router_kernel,
        grid=(n // tt,),
        in_specs=[row(d), row(c), full(c, d), full(1, d), full(n_e, d), full(n_e, 1)],
        out_specs=[row(d), row(d), k4, k4, k4, full(n_e, LANES)],
        out_shape=[jax.ShapeDtypeStruct((n, d), jnp.float32), jax.ShapeDtypeStruct((n, d), jnp.bfloat16),
                   jax.ShapeDtypeStruct((TOP_K, n), jnp.int32), jax.ShapeDtypeStruct((TOP_K, n), jnp.float32),
                   jax.ShapeDtypeStruct((TOP_K, n), jnp.int32), jax.ShapeDtypeStruct((n_e, LANES), jnp.int32)],
        scratch_shapes=[pltpu.VMEM((c, d), jnp.bfloat16), pltpu.VMEM((n_e, 1), jnp.float32)],
        compiler_params=pltpu.CompilerParams(dimension_semantics=("arbitrary",)),
        name="outproj_router",
    )(x, mix, w_out, g_ffn.reshape(1, d), w_router.T, b_router.reshape(n_e, 1))


def _expert_kernel(blk_e_ref, n_used_ref, x_ref, wup_ref, bup_ref, wdn_ref, bdn_ref, o_ref, wup_bf, wdn_bf):
    i = pl.program_id(0)
    d_ff = wdn_ref.shape[1]

    @pl.when(i < n_used_ref[0])
    def _():
        e = blk_e_ref[i]
        prev = blk_e_ref[jnp.maximum(i - 1, 0)]

        @pl.when((i == 0) | (e != prev))
        def _():
            wup_bf[...] = wup_ref[0].astype(jnp.bfloat16)
            wdn_bf[...] = wdn_ref[0].astype(jnp.bfloat16)

        u = jnp.dot(x_ref[...], wup_bf[...], preferred_element_type=jnp.float32) + bup_ref[0]
        glu = jnp.minimum(u[:, :d_ff], SWIGLU_LIMIT)
        lin = jnp.clip(u[:, d_ff:], -SWIGLU_LIMIT, SWIGLU_LIMIT)
        a = glu * jax.nn.sigmoid(SWIGLU_ALPHA * glu) * (lin + 1.0)
        o_ref[...] = jnp.dot(a.astype(jnp.bfloat16), wdn_bf[...], preferred_element_type=jnp.float32) + bdn_ref[0]

    @pl.when(i >= n_used_ref[0])
    def _():
        o_ref[...] = jnp.zeros_like(o_ref)


def _expert_vmem_bytes(bm, d, f2, d_ff):
    weights = 2 * 4 * (d * f2 + d_ff * d) + 2 * (d * f2 + d_ff * d)
    rows = 2 * bm * d * (2 + 4)
    temps = bm * f2 * 4 * 2 + bm * d_ff * (4 + 2)
    return weights + rows + temps


def _experts_pallas(xs, blk_e, n_used, w_up, b_up, w_down, b_down, bm):
    n_slots, d = xs.shape
    n_e, _, f2 = w_up.shape
    d_ff = w_down.shape[1]
    grid_spec = pltpu.PrefetchScalarGridSpec(
        num_scalar_prefetch=2,
        grid=(n_slots // bm,),
        in_specs=[pl.BlockSpec((bm, d), lambda i, be, nu: (i, 0)),
                  pl.BlockSpec((1, d, f2), lambda i, be, nu: (be[i], 0, 0)),
                  pl.BlockSpec((1, 1, f2), lambda i, be, nu: (be[i], 0, 0)),
                  pl.BlockSpec((1, d_ff, d), lambda i, be, nu: (be[i], 0, 0)),
                  pl.BlockSpec((1, 1, d), lambda i, be, nu: (be[i], 0, 0))],
        out_specs=pl.BlockSpec((bm, d), lambda i, be, nu: (i, 0)),
        scratch_shapes=[pltpu.VMEM((d, f2), jnp.bfloat16), pltpu.VMEM((d_ff, d), jnp.bfloat16)],
    )
    vmem_limit = _expert_vmem_bytes(bm, d, f2, d_ff) * 5 // 4
    return pl.pallas_call(
        _expert_kernel,
        grid_spec=grid_spec,
        out_shape=jax.ShapeDtypeStruct((n_slots, d), jnp.float32),
        compiler_params=pltpu.CompilerParams(dimension_semantics=("arbitrary",), vmem_limit_bytes=vmem_limit),
        name="expert_mlp",
    )(blk_e, n_used, xs, w_up, b_up.reshape(n_e, 1, f2), w_down, b_down.reshape(n_e, 1, d))


def _combine_norm_kernel(x1_ref, og_ref, gate_ref, g_ref, o_ref):
    y = x1_ref[...]
    gates = gate_ref[...]
    for k in range(TOP_K):
        y = y + gates[:, k:k + 1] * og_ref[k]
    o_ref[...] = y * lax.rsqrt(jnp.mean(y * y, axis=-1, keepdims=True) + RMS_EPS) * g_ref[...]


def _combine_norm_pallas(x1, og, gate_t, g_final):
    n, d = x1.shape
    tt = ROUTER_TT
    return pl.pallas_call(
        _combine_norm_kernel,
        grid=(n // tt,),
        in_specs=[pl.BlockSpec((tt, d), lambda i: (i, 0)), pl.BlockSpec((TOP_K, tt, d), lambda i: (0, i, 0)),
                  pl.BlockSpec((tt, TOP_K), lambda i: (i, 0)), pl.BlockSpec((1, d), lambda i: (0, 0))],
        out_specs=pl.BlockSpec((tt, d), lambda i: (i, 0)),
        out_shape=jax.ShapeDtypeStruct((n, d), jnp.float32),
        compiler_params=pltpu.CompilerParams(dimension_semantics=("arbitrary",)),
        name="combine_final_norm",
    )(x1, og, gate_t, g_final.reshape(1, d))


def _finish_pallas(x, mix, w_out, g_ffn, w_router, b_router, w_up, b_up, w_down, b_down, g_final):
    n, d = x.shape
    n_e = w_router.shape[1]
    bm = MOE_BM
    x1, h, eidx, gate, rank, cnt = _router_pallas(x, mix, w_out, g_ffn, w_router, b_router)
    counts = cnt[:, 0]
    padded = (counts + bm - 1) // bm * bm
    pad_end = jnp.cumsum(padded)
    gstart = pad_end - padded
    dest = gstart[eidx] + rank
    nb = -(-(n * TOP_K) // bm) + n_e
    blk_e = jnp.minimum(jnp.searchsorted(pad_end, jnp.arange(nb, dtype=jnp.int32) * bm, side='right'),
                        n_e - 1).astype(jnp.int32)
    n_used = (pad_end[-1] // bm).astype(jnp.int32).reshape(1)
    tok = jnp.broadcast_to(jnp.arange(n, dtype=jnp.int32)[None, :], (TOP_K, n))
    src = jnp.zeros((nb * bm,), jnp.int32).at[dest.reshape(-1)].set(tok.reshape(-1))
    out = _experts_pallas(h[src], blk_e, n_used, w_up, b_up, w_down, b_down, bm)
    return _combine_norm_pallas(x1, out[dest], gate.T, g_final)


def kernel(x_prompt, x_sample, cache_cmp_kv, cache_sel_kv, cache_win_kv, state_ret, page_table, g_attn, w_in, w_cmp1, b_cmp1, w_cmp2, b_cmp2, w_out, g_ffn, w_router, b_router, w_up, b_up, w_down, b_down, g_final):
    seq = x_prompt.shape[1]
    past = page_table.shape[1] * PAGE_SIZE
    pos_p = jnp.arange(seq, dtype=jnp.int32)
    pos_s = past + jnp.arange(x_sample.shape[1], dtype=jnp.int32)
    log_g = _ret_log_decay()
    assert DEPTH == 1
    l = 0
    d = x_prompt.shape[-1]
    q, kv_cmp_p, kv_sel_p, kv_win_p, _, r_q, r_k, r_v, r_g, gt = _mixer_inputs(x_prompt, pos_p, g_attn[l], w_in[l])
    blocks = _compress_blocks(_segment_proj(kv_cmp_p, w_cmp1[l]), b_cmp1[l], w_cmp2[l], b_cmp2[l])
    o_nsa = _nsa_prompt_pallas(q, _rope(q, pos_p), blocks, kv_sel_p, kv_win_p, gt)
    ret_p, o_r = _ret_prompt(r_q, r_k, r_v, log_g)
    mix_p = jnp.concatenate([o_nsa, _ret_out(o_r, r_g)], axis=-1)
    q, kv_cmp_s, kv_sel_s, kv_win_s, gates, r_q, r_k, r_v, r_g, _ = _mixer_inputs(x_sample, pos_s, g_attn[l], w_in[l])
    o_c, o_s, o_w, win_s = _nsa_sample(q, kv_cmp_s, kv_sel_s, kv_win_s, cache_cmp_kv, cache_sel_kv, cache_win_kv[:, l], page_table, l, w_cmp1[l], b_cmp1[l], w_cmp2[l], b_cmp2[l], pos_s)
    ret_s, o_r = _ret_chunk(state_ret[:, l].astype(jnp.float32), r_q.astype(jnp.float32), r_k.astype(jnp.float32), r_v.astype(jnp.float32), log_g)
    mix_s = jnp.concatenate([_nsa_merge(gates, o_c, o_s, o_w), _ret_out(o_r, r_g)], axis=-1)
    n_p = x_prompt.shape[0] * seq
    x_all = jnp.concatenate([x_prompt.reshape(n_p, d), x_sample.reshape(-1, d)], axis=0)
    mix_all = jnp.concatenate([mix_p.reshape(n_p, MIX_WIDTH), mix_s.reshape(-1, MIX_WIDTH)], axis=0)
    y = _finish_pallas(x_all, mix_all, w_out[l], g_ffn[l], w_router[l], b_router[l], w_up[l], b_up[l], w_down[l], b_down[l], g_final)
    y_prompt = y[:n_p].reshape(x_prompt.shape)
    y_sample = y[n_p:].reshape(x_sample.shape)
    win_p = kv_win_p[:, seq - min(WINDOW, seq):]
    return (y_prompt, y_sample, kv_cmp_p[:, None], kv_sel_p[:, None], win_p[:, None], ret_p[:, None],
            kv_cmp_s[:, None], kv_sel_s[:, None], win_s[:, None], ret_s[:, None])
```

```python
import functools
import jax, jax.numpy as jnp
from jax import lax
import numpy as np
from jax.experimental import pallas as pl
from jax.experimental.pallas import tpu as pltpu

D_MODEL = 1024
BATCH = 4
SEQ = 4096
DEPTH = 1
DEC_BATCH = 128
DEC_SEQ = 4
PAST_LEN = 8192
PAGE_SIZE = 128

HEAD_DIM = 64
N_NSA_HEADS = 8
N_KV_HEADS = 2
GQA_RATIO = N_NSA_HEADS // N_KV_HEADS
CMP_BLOCK = 32
CMP_STRIDE = 16
N_HALF = CMP_BLOCK // CMP_STRIDE
CMP_HIDDEN = 256
SEL_BLOCK = 64
N_SEL = 16
WINDOW = 512
NSA_Q_BLOCK = 64
WIN_Q_BLOCK = 128
N_RET_HEADS = 4
RET_DK = 64
RET_DV = 128
RET_CHUNK = 128
N_EXPERTS = 32
TOP_K = 4
D_FF = D_MODEL
SWIGLU_ALPHA = 1.702
SWIGLU_LIMIT = 7.0
MOE_BLOCK = 128
ROPE_THETA = 10000.0
RMS_EPS = 1e-5
NEG_INF = -1e30
FORCED_SCORE = 1e6
INVALID_SCORE = -1e9
NSA_WIDTH = N_NSA_HEADS * HEAD_DIM
KV_WIDTH = N_KV_HEADS * HEAD_DIM
RET_QK_WIDTH = N_RET_HEADS * RET_DK
RET_WIDTH = N_RET_HEADS * RET_DV
IN_SIZES = (NSA_WIDTH, KV_WIDTH, KV_WIDTH, KV_WIDTH, KV_WIDTH, KV_WIDTH, KV_WIDTH, 3 * N_NSA_HEADS, RET_QK_WIDTH, RET_QK_WIDTH, RET_WIDTH, RET_WIDTH)
IN_TOTAL = NSA_WIDTH + 6 * KV_WIDTH + 3 * N_NSA_HEADS + 2 * RET_QK_WIDTH + 2 * RET_WIDTH
MIX_WIDTH = NSA_WIDTH + RET_WIDTH
ATTN_SCALE = HEAD_DIM ** -0.5


def _rmsnorm(x, g):
    xf = x.astype(jnp.float32)
    y = xf * lax.rsqrt(jnp.mean(xf * xf, axis=-1, keepdims=True) + RMS_EPS)
    return (y * g.astype(jnp.float32)).astype(x.dtype)


def _rope(x, pos):
    half = x.shape[-1] // 2
    inv = ROPE_THETA ** (-jnp.arange(half, dtype=jnp.float32) / half)
    ang = pos.astype(jnp.float32)[:, None] * inv[None, :]
    cos = jnp.cos(ang)[None, :, None, :]
    sin = jnp.sin(ang)[None, :, None, :]
    xf = x.astype(jnp.float32)
    x1, x2 = xf[..., :half], xf[..., half:]
    return jnp.concatenate([x1 * cos - x2 * sin, x2 * cos + x1 * sin], axis=-1).astype(x.dtype)


def _masked_softmax(s, mask):
    s = jnp.where(mask, s, NEG_INF)
    m = jnp.max(s, axis=-1, keepdims=True)
    p = jnp.where(mask, jnp.exp(s - m), 0.0)
    return p / jnp.maximum(jnp.sum(p, axis=-1, keepdims=True), 1e-30)


def _mixer_inputs(x, pos, g_attn, w_in):
    b, t = x.shape[0], x.shape[1]
    h = _rmsnorm(x, g_attn)
    proj = jnp.einsum('btd,dc->btc', h, w_in)
    offs = [int(o) for o in np.cumsum(np.array(IN_SIZES))[:-1]]
    q, kc, vc, ks, vs, kw, vw, gt, rq, rk, rv, rg = jnp.split(proj, offs, axis=-1)

    def heads(a, n, d):
        return a.reshape(b, t, n, d)

    def kvh(a):
        return heads(a, N_KV_HEADS, HEAD_DIM)

    kv_cmp = jnp.stack([kvh(kc), kvh(vc)], axis=2)
    kv_sel = jnp.stack([_rope(kvh(ks), pos), kvh(vs)], axis=2)
    kv_win = jnp.stack([_rope(kvh(kw), pos), kvh(vw)], axis=2)
    gates = jax.nn.sigmoid(gt.astype(jnp.float32)).reshape(b, t, N_NSA_HEADS, 3)
    r_q = _rope(heads(rq, N_RET_HEADS, RET_DK), pos)
    r_k = _rope(heads(rk, N_RET_HEADS, RET_DK), pos) * (RET_DK ** -0.5)
    r_v = heads(rv, N_RET_HEADS, RET_DV)
    return heads(q, N_NSA_HEADS, HEAD_DIM), kv_cmp, kv_sel, kv_win, gates, r_q, r_k, r_v, rg, gt


def _segment_proj(kv_rows, w1):
    b, l = kv_rows.shape[0], kv_rows.shape[1]
    seg = kv_rows.reshape(b, l // CMP_STRIDE, CMP_STRIDE, 2, N_KV_HEADS, HEAD_DIM)
    return jnp.einsum('bsrcgd,chrdk->bscghk', seg, w1)


def _compress_blocks(p_seg, b1, w2, b2):
    n_cmp = p_seg.shape[1] - N_HALF + 1
    hid = sum(p_seg[:, h:h + n_cmp, :, :, h, :] for h in range(N_HALF)) + b1[None, None, :, None, :]
    hid = jax.nn.gelu(hid.astype(jnp.float32))
    out = jnp.einsum('bncgk,ckd->bncgd', hid, w2.astype(jnp.float32))
    return out + b2.astype(jnp.float32)[None, None, :, None, :]


def _cmp_attend(q, blocks, q_pos):
    b, nq = q.shape[0], q.shape[1]
    n_cmp = blocks.shape[1]
    qg = q.astype(jnp.float32).reshape(b, nq, N_KV_HEADS, GQA_RATIO, HEAD_DIM)
    s = jnp.einsum('bqgrd,bngd->bqgrn', qg, blocks[:, :, 0]) * ATTN_SCALE
    blk_end = jnp.arange(n_cmp) * CMP_STRIDE + (CMP_BLOCK - 1)
    mask = blk_end[None, :] <= q_pos[:, None]
    p = _masked_softmax(s, mask[None, :, None, None, :])
    o = jnp.einsum('bqgrn,bngd->bqgrd', p, blocks[:, :, 1])
    return o.reshape(b, nq, N_NSA_HEADS, HEAD_DIM), jnp.sum(p, axis=3)


def _select_blocks(p_grp, q_pos, n_sb):
    ratio = SEL_BLOCK // CMP_STRIDE
    ov = N_HALF - 1
    n_cmp = p_grp.shape[-1]
    pp = jnp.pad(p_grp, ((0, 0), (0, 0), (0, 0), (ov, ratio * n_sb - n_cmp)))
    score = sum(pp[..., o:o + ratio * n_sb:ratio] for o in range(ratio + ov))
    blk = jnp.arange(n_sb)[None, :]
    qp = q_pos[:, None]
    cur = qp // SEL_BLOCK
    forced = (blk == 0) | (blk == cur) | (blk == cur - 1)
    valid = blk * SEL_BLOCK <= qp
    score = jnp.where(forced[None, :, None, :], FORCED_SCORE, score)
    score = jnp.where(valid[None, :, None, :], score, INVALID_SCORE)
    _, idx = lax.top_k(score, min(N_SEL, n_sb))
    return idx


def _sel_attend(q_rot, kv_g, idx, q_pos):
    b, nq = q_rot.shape[0], q_rot.shape[1]
    m = idx.shape[-1] * SEL_BLOCK
    qg = q_rot.astype(jnp.float32).reshape(b, nq, N_KV_HEADS, GQA_RATIO, HEAD_DIM)
    k = kv_g[..., 0, :].astype(jnp.float32).reshape(b, nq, N_KV_HEADS, m, HEAD_DIM)
    v = kv_g[..., 1, :].astype(jnp.float32).reshape(b, nq, N_KV_HEADS, m, HEAD_DIM)
    s = jnp.einsum('bqgrd,bqgmd->bqgrm', qg, k) * ATTN_SCALE
    k_pos = (idx[..., None] * SEL_BLOCK + jnp.arange(SEL_BLOCK)).reshape(b, nq, N_KV_HEADS, m)
    mask = (k_pos <= q_pos[None, :, None, None])[:, :, :, None, :]
    p = _masked_softmax(s, mask)
    o = jnp.einsum('bqgrm,bqgmd->bqgrd', p, v)
    return o.reshape(b, nq, N_NSA_HEADS, HEAD_DIM)


def _win_prompt(q_rot, kv_win):
    b, t = q_rot.shape[0], q_rot.shape[1]
    n_qb = t // WIN_Q_BLOCK
    span = WINDOW + WIN_Q_BLOCK
    kvp = jnp.pad(kv_win, ((0, 0), (WINDOW, 0), (0, 0), (0, 0), (0, 0)))
    idx = jnp.arange(n_qb)[:, None] * WIN_Q_BLOCK + jnp.arange(span)[None, :]
    kb = kvp[:, idx].astype(jnp.float32)
    k_pos = (idx - WINDOW)[:, None, :]
    q_pos = jnp.arange(t).reshape(n_qb, WIN_Q_BLOCK)[:, :, None]
    qb = q_rot.astype(jnp.float32).reshape(b, n_qb, WIN_Q_BLOCK, N_KV_HEADS, GQA_RATIO, HEAD_DIM)
    s = jnp.einsum('bnqgrd,bnkgd->bnqgrk', qb, kb[:, :, :, 0]) * ATTN_SCALE
    mask = (k_pos <= q_pos) & (k_pos > q_pos - WINDOW) & (k_pos >= 0)
    p = _masked_softmax(s, mask[None, :, :, None, None, :])
    o = jnp.einsum('bnqgrk,bnkgd->bnqgrd', p, kb[:, :, :, 1])
    return o.reshape(b, t, N_NSA_HEADS, HEAD_DIM)


def _win_sample(q_rot, win_buf, kv_new, q_pos, past):
    b, s_len = q_rot.shape[0], q_rot.shape[1]
    wb = win_buf.shape[1]
    kv = jnp.concatenate([win_buf.astype(kv_new.dtype), kv_new], axis=1)
    k_pos = (past - wb + jnp.arange(wb + s_len))[None, :]
    qp = q_pos[:, None]
    qg = q_rot.astype(jnp.float32).reshape(b, s_len, N_KV_HEADS, GQA_RATIO, HEAD_DIM)
    s = jnp.einsum('bqgrd,bkgd->bqgrk', qg, kv[:, :, 0].astype(jnp.float32)) * ATTN_SCALE
    mask = (k_pos <= qp) & (k_pos > qp - WINDOW) & (k_pos >= 0)
    p = _masked_softmax(s, mask[None, :, None, None, :])
    o = jnp.einsum('bqgrk,bkgd->bqgrd', p, kv[:, :, 1].astype(jnp.float32))
    return o.reshape(b, s_len, N_NSA_HEADS, HEAD_DIM), kv[:, s_len:]


def _nsa_merge(gates, o_c, o_s, o_w):
    b, t = o_c.shape[0], o_c.shape[1]
    o = gates[..., 0:1] * o_c + gates[..., 1:2] * o_s + gates[..., 2:3] * o_w
    return o.reshape(b, t, NSA_WIDTH)


def _nsa_prompt(q, kv_cmp, kv_sel, kv_win, w1, b1, w2, b2, pos):
    b, t = q.shape[0], q.shape[1]
    q_rot = _rope(q, pos)
    blocks = _compress_blocks(_segment_proj(kv_cmp, w1), b1, w2, b2)
    n_sb = t // SEL_BLOCK
    kbs = kv_sel.reshape(b, n_sb, SEL_BLOCK, 2, N_KV_HEADS, HEAD_DIM).transpose(0, 4, 1, 2, 3, 5)
    bi = jnp.arange(b)[:, None, None, None]
    gi = jnp.arange(N_KV_HEADS)[None, None, :, None]
    n_qb = t // NSA_Q_BLOCK

    def q_block(args):
        qb, qrb, pb = args
        o_c, p_grp = _cmp_attend(qb, blocks, pb)
        idx = _select_blocks(p_grp, pb, n_sb)
        kv_g = kbs[bi, gi, idx]
        return o_c, _sel_attend(qrb, kv_g, idx, pb)

    def to_blocks(a):
        return a.reshape(b, n_qb, NSA_Q_BLOCK, N_NSA_HEADS, HEAD_DIM).swapaxes(0, 1)

    def from_blocks(a):
        return a.swapaxes(0, 1).reshape(b, t, N_NSA_HEADS, HEAD_DIM)

    o_c, o_s = lax.map(q_block, (to_blocks(q), to_blocks(q_rot), pos.reshape(n_qb, NSA_Q_BLOCK)))
    return from_blocks(o_c), from_blocks(o_s), _win_prompt(q_rot, kv_win)


def _nsa_sample(q, kv_cmp, kv_sel, kv_win, cache_cmp_kv, cache_sel_kv, win_buf, page_table, layer, w1, b1, w2, b2, pos):
    db, s_len = q.shape[0], q.shape[1]
    past = page_table.shape[1] * PAGE_SIZE
    q_rot = _rope(q, pos)
    past_cmp = cache_cmp_kv[page_table, layer].reshape(db, past, 2, N_KV_HEADS, HEAD_DIM)
    new_cmp = jnp.pad(kv_cmp, ((0, 0), (0, (-s_len) % CMP_STRIDE), (0, 0), (0, 0), (0, 0)))
    p_seg = jnp.concatenate([_segment_proj(past_cmp.astype(kv_cmp.dtype), w1), _segment_proj(new_cmp, w1)], axis=1)
    blocks = _compress_blocks(p_seg, b1, w2, b2)
    o_c, p_grp = _cmp_attend(q, blocks, pos)
    n_pb = past // SEL_BLOCK
    n_tail = -(-s_len // SEL_BLOCK)
    idx = _select_blocks(p_grp, pos, n_pb + n_tail)
    bi = jnp.arange(db)[:, None, None, None]
    gi = jnp.arange(N_KV_HEADS)[None, None, :, None]
    blocks_per_page = PAGE_SIZE // SEL_BLOCK
    blk_past = jnp.minimum(idx, n_pb - 1)
    page = page_table[bi, blk_past // blocks_per_page]
    row = (blk_past % blocks_per_page)[..., None] * SEL_BLOCK + jnp.arange(SEL_BLOCK)
    past_g = cache_sel_kv[page[..., None], layer, row, :, gi[..., None], :]
    tail = jnp.pad(kv_sel, ((0, 0), (0, n_tail * SEL_BLOCK - s_len), (0, 0), (0, 0), (0, 0)))
    tail = tail.reshape(db, n_tail, SEL_BLOCK, 2, N_KV_HEADS, HEAD_DIM).transpose(0, 4, 1, 2, 3, 5)
    tail_g = tail[bi, gi, jnp.clip(idx - n_pb, 0, n_tail - 1)]
    kv_g = jnp.where((idx >= n_pb)[..., None, None, None], tail_g, past_g.astype(tail_g.dtype))
    o_s = _sel_attend(q_rot, kv_g, idx, pos)
    o_w, new_win = _win_sample(q_rot, win_buf, kv_win, pos, past)
    return o_c, o_s, o_w, new_win


def _ret_log_decay():
    return jnp.log1p(-jnp.exp2(-5.0 - jnp.arange(N_RET_HEADS, dtype=jnp.float32)))


def _ret_chunk(state, q, k, v, log_g):
    c = q.shape[1]
    i = jnp.arange(c, dtype=jnp.float32)
    diff = i[:, None] - i[None, :]
    decay = jnp.where(diff[None] >= 0, jnp.exp(log_g[:, None, None] * jnp.maximum(diff, 0.0)[None]), 0.0)
    scores = jnp.einsum('bihd,bjhd->bhij', q, k) * decay[None]
    inner = jnp.einsum('bhij,bjhv->bihv', scores, v)
    q_dec = jnp.exp(log_g[None, :] * (i[:, None] + 1.0))
    cross = jnp.einsum('bihd,bhdv->bihv', q * q_dec[None, :, :, None], state)
    k_dec = jnp.exp(log_g[None, :] * (c - 1.0 - i[:, None]))
    new_state = jnp.exp(log_g * c)[None, :, None, None] * state + jnp.einsum('bjhd,bjhv->bhdv', k * k_dec[None, :, :, None], v)
    return new_state, inner + cross


def _ret_prompt(r_q, r_k, r_v, log_g):
    b, t = r_q.shape[0], r_q.shape[1]
    n_ch = t // RET_CHUNK

    def chunks(a):
        return a.astype(jnp.float32).reshape(b, n_ch, RET_CHUNK, a.shape[2], a.shape[3]).swapaxes(0, 1)

    s0 = jnp.zeros((b, N_RET_HEADS, RET_DK, RET_DV), jnp.float32)
    s_fin, o = lax.scan(lambda s, xs: _ret_chunk(s, xs[0], xs[1], xs[2], log_g), s0, (chunks(r_q), chunks(r_k), chunks(r_v)))
    return s_fin, o.swapaxes(0, 1).reshape(b, t, N_RET_HEADS, RET_DV)


def _ret_out(o, r_g):
    b, t = o.shape[0], o.shape[1]
    o = o * lax.rsqrt(jnp.mean(o * o, axis=-1, keepdims=True) + RMS_EPS)
    return o.reshape(b, t, RET_WIDTH) * jax.nn.silu(r_g.astype(jnp.float32))


def _moe(h, w_router, b_router, w_up, b_up, w_down, b_down):
    n_tok, d = h.shape
    logits = jnp.einsum('nd,de->ne', h, w_router).astype(jnp.float32) + b_router.astype(jnp.float32)
    top_val, top_idx = lax.top_k(logits, TOP_K)
    gate = jax.nn.softmax(top_val, axis=-1)
    n_assign = n_tok * TOP_K
    flat_e = top_idx.reshape(-1)
    order = jnp.argsort(flat_e)
    sorted_e = flat_e[order]
    tok = order // TOP_K
    counts = jnp.zeros((N_EXPERTS,), jnp.int32).at[flat_e].add(1)
    padded = (counts + MOE_BLOCK - 1) // MOE_BLOCK * MOE_BLOCK
    pad_end = jnp.cumsum(padded)
    pad_start = pad_end - padded
    start = jnp.cumsum(counts) - counts
    dest = pad_start[sorted_e] + jnp.arange(n_assign, dtype=jnp.int32) - start[sorted_e]
    n_blocks = -(-n_assign // MOE_BLOCK) + N_EXPERTS
    rows = jnp.zeros((n_blocks * MOE_BLOCK, d), h.dtype).at[dest].set(h[tok])
    blk_e = jnp.minimum(jnp.searchsorted(pad_end, jnp.arange(n_blocks, dtype=jnp.int32) * MOE_BLOCK, side='right'), N_EXPERTS - 1)

    def expert_block(args):
        xb, e = args
        u = xb @ w_up[e] + b_up[e]
        glu = jnp.minimum(u[:, :D_FF], SWIGLU_LIMIT)
        lin = jnp.clip(u[:, D_FF:], -SWIGLU_LIMIT, SWIGLU_LIMIT)
        a = glu * jax.nn.sigmoid(SWIGLU_ALPHA * glu) * (lin + 1.0)
        return a @ w_down[e] + b_down[e]

    out = lax.map(expert_block, (rows.reshape(n_blocks, MOE_BLOCK, d), blk_e)).reshape(-1, d)
    contrib = out[dest] * gate.reshape(-1)[order][:, None].astype(h.dtype)
    return jnp.zeros((n_tok, d), h.dtype).at[tok].add(contrib)


def _finish(x, o_nsa, o_ret, w_out, g_ffn, w_router, b_router, w_up, b_up, w_down, b_down):
    b, t, d = x.shape
    mix = jnp.concatenate([o_nsa, o_ret], axis=-1).astype(x.dtype)
    x = x + jnp.einsum('btc,cd->btd', mix, w_out)
    h = _rmsnorm(x, g_ffn).reshape(b * t, d)
    return x + _moe(h, w_router, b_router, w_up, b_up, w_down, b_down).reshape(b, t, d)


NSA_TQ = 128
NSA_TK = 128
LANES = 128
HIGHEST = lax.Precision.HIGHEST


def _dot_nt(a, b, precision=None):
    return lax.dot_general(a, b, (((1,), (1,)), ((), ())), precision=precision, preferred_element_type=jnp.float32)


def _nsa_prompt_kernel(qc_ref, qr_ref, kc_ref, vc_ref, ks_ref, vs_ref, kw_ref, vw_ref, gt_ref, o_ref,
                       m_scr, l_scr, acc_scr, *, seq):
    tq, tk = NSA_TQ, NSA_TK
    rows = GQA_RATIO * tq
    qt = pl.program_id(1)
    q0 = qt * tq
    ncp = kc_ref.shape[1]
    n_sb = seq // SEL_BLOCK
    ratio = SEL_BLOCK // CMP_STRIDE
    n_sel = min(N_SEL, n_sb)
    lane = lax.broadcasted_iota(jnp.int32, (tq, LANES), 1)
    gates = jax.nn.sigmoid(gt_ref[0])

    qpos_k = q0 + lax.broadcasted_iota(jnp.int32, (tq, tk), 0)
    kiota = lax.broadcasted_iota(jnp.int32, (tq, tk), 1)

    def attend(q, k_ref, v_ref, lo, hi, mask_fn):
        m_scr[...] = jnp.full((rows, 1), NEG_INF, jnp.float32)
        l_scr[...] = jnp.zeros((rows, 1), jnp.float32)
        acc_scr[...] = jnp.zeros((rows, LANES), jnp.float32)

        def body(kt, carry):
            k0 = pl.multiple_of(kt * tk, tk)
            k = k_ref[0, pl.ds(k0, tk), :]
            v = v_ref[0, pl.ds(k0, tk), :]
            s = _dot_nt(q, k).reshape(GQA_RATIO, tq, tk)
            mask = mask_fn(k0)[None]
            s = jnp.where(mask, s, NEG_INF)
            m_old = m_scr[...].reshape(GQA_RATIO, tq, 1)
            m_new = jnp.maximum(m_old, jnp.max(s, axis=-1, keepdims=True))
            alpha = jnp.exp(m_old - m_new)
            p = jnp.where(mask, jnp.exp(s - m_new), 0.0)
            l_new = alpha * l_scr[...].reshape(GQA_RATIO, tq, 1) + jnp.sum(p, axis=-1, keepdims=True)
            pv = jnp.dot(p.reshape(rows, tk).astype(jnp.bfloat16), v, preferred_element_type=jnp.float32)
            acc_scr[...] = alpha.reshape(rows, 1) * acc_scr[...] + pv
            m_scr[...] = m_new.reshape(rows, 1)
            l_scr[...] = l_new.reshape(rows, 1)
            return carry

        lax.fori_loop(lo, hi, body, 0)
        return acc_scr[...] / jnp.maximum(l_scr[...], 1e-30)

    for g in range(N_KV_HEADS):
        qc = qc_ref[0, g].reshape(rows, LANES)
        s = _dot_nt(qc, kc_ref[0], HIGHEST).reshape(GQA_RATIO, tq, ncp)
        qpos_c = q0 + lax.broadcasted_iota(jnp.int32, (tq, ncp), 0)
        blk_end = lax.broadcasted_iota(jnp.int32, (tq, ncp), 1) * CMP_STRIDE + (CMP_BLOCK - 1)
        cmask = (blk_end <= qpos_c)[None]
        s = jnp.where(cmask, s, NEG_INF)
        mx = jnp.max(s, axis=-1, keepdims=True)
        p = jnp.where(cmask, jnp.exp(s - mx), 0.0)
        p = p / jnp.maximum(jnp.sum(p, axis=-1, keepdims=True), 1e-30)
        o_c = jnp.dot(p.reshape(rows, ncp).astype(jnp.bfloat16), vc_ref[0].astype(jnp.bfloat16),
                      preferred_element_type=jnp.float32)
        p_grp = jnp.sum(p, axis=0)

        jj = lax.broadcasted_iota(jnp.int32, (n_sb, ncp), 0)
        nn = lax.broadcasted_iota(jnp.int32, (n_sb, ncp), 1)
        overlap = ((nn >= ratio * jj - (N_HALF - 1)) & (nn <= ratio * jj + ratio - 1)).astype(jnp.float32)
        score = _dot_nt(overlap, p_grp, HIGHEST)
        jt = lax.broadcasted_iota(jnp.int32, (n_sb, tq), 0)
        qpt = q0 + lax.broadcasted_iota(jnp.int32, (n_sb, tq), 1)
        cur = qpt // SEL_BLOCK
        forced = (jt == 0) | (jt == cur) | (jt == cur - 1)
        valid = jt * SEL_BLOCK <= qpt
        score = jnp.where(forced, FORCED_SCORE, score)
        score = jnp.where(valid, score, INVALID_SCORE)
        jf = jt.astype(jnp.float32)
        sel_t = jnp.zeros((n_sb, tq), jnp.float32)
        for _ in range(n_sel):
            best = jnp.max(score, axis=0, keepdims=True)
            first = jnp.min(jnp.where(score == best, jf, float(n_sb)), axis=0, keepdims=True)
            hit = jf == first
            sel_t = jnp.where(hit, 1.0, sel_t)
            score = jnp.where(hit, -3e38, score)
        if n_sb < LANES:
            sel_t = jnp.concatenate([sel_t, jnp.zeros((LANES - n_sb, tq), jnp.float32)], axis=0)
        sel = sel_t.T.astype(jnp.bfloat16)

        def sel_mask(k0):
            blk_of_key = (k0 + lax.broadcasted_iota(jnp.int32, (LANES, tk), 1)) // SEL_BLOCK
            expand = (blk_of_key == lax.broadcasted_iota(jnp.int32, (LANES, tk), 0)).astype(jnp.bfloat16)
            chosen = jnp.dot(sel, expand, preferred_element_type=jnp.float32)
            return (chosen > 0.5) & (k0 + kiota <= qpos_k)

        qr = qr_ref[0, g].reshape(rows, LANES)
        o_s = attend(qr, ks_ref, vs_ref, 0, qt + 1, sel_mask)

        def win_mask(k0):
            kpos = k0 + kiota
            return (kpos <= qpos_k) & (kpos > qpos_k - WINDOW)

        o_w = attend(qr, kw_ref, vw_ref, jnp.maximum(qt - WINDOW // tk, 0), qt + 1, win_mask)

        for r in range(GQA_RATIO):
            col = (g * GQA_RATIO + r) * 3
            sl = slice(r * tq, (r + 1) * tq)
            comb = (gates[:, col:col + 1] * o_c[sl] + gates[:, col + 1:col + 2] * o_s[sl]
                    + gates[:, col + 2:col + 3] * o_w[sl])
            if g == 0:
                o_ref[0, r] = comb
            else:
                o_ref[0, r] = jnp.where(lane < HEAD_DIM, o_ref[0, r], comb)


def _nsa_prompt_pallas(q, q_rot, blocks, kv_sel, kv_win, gt):
    b, t = q.shape[0], q.shape[1]
    assert t % NSA_TQ == 0 and NSA_TQ == NSA_TK and WINDOW % NSA_TK == 0 and t // SEL_BLOCK <= LANES
    ncp = t // CMP_STRIDE
    n_qt = t // NSA_TQ

    def group_pad(a, dtype):
        a = (a * ATTN_SCALE).reshape(b, t, N_KV_HEADS, GQA_RATIO, HEAD_DIM).transpose(0, 2, 3, 1, 4)
        eye = jnp.eye(N_KV_HEADS, dtype=a.dtype)[None, :, None, None, :, None]
        return (a[:, :, :, :, None, :] * eye).reshape(b, N_KV_HEADS, GQA_RATIO, t, LANES).astype(dtype)

    qc = group_pad(q, jnp.float32)
    qr = group_pad(q_rot, jnp.bfloat16)
    blk = jnp.pad(blocks, ((0, 0), (0, ncp - blocks.shape[1]), (0, 0), (0, 0), (0, 0))).reshape(b, ncp, 2 * LANES)
    ks = kv_sel.astype(jnp.bfloat16).reshape(b, t, 2 * LANES)
    kw = kv_win.astype(jnp.bfloat16).reshape(b, t, 2 * LANES)
    gtp = jnp.pad(gt.astype(jnp.float32), ((0, 0), (0, 0), (0, LANES - gt.shape[-1])))

    q_spec = pl.BlockSpec((1, N_KV_HEADS, GQA_RATIO, NSA_TQ, LANES), lambda i, j: (i, 0, 0, j, 0))

    def kv_spec(c, n):
        return pl.BlockSpec((1, n, LANES), lambda i, j: (i, 0, c))

    rows = GQA_RATIO * NSA_TQ
    out = pl.pallas_call(
        functools.partial(_nsa_prompt_kernel, seq=t),
        grid=(b, n_qt),
        in_specs=[q_spec, q_spec, kv_spec(0, ncp), kv_spec(1, ncp), kv_spec(0, t), kv_spec(1, t),
                  kv_spec(0, t), kv_spec(1, t), pl.BlockSpec((1, NSA_TQ, LANES), lambda i, j: (i, j, 0))],
        out_specs=pl.BlockSpec((1, GQA_RATIO, NSA_TQ, LANES), lambda i, j: (i, 0, j, 0)),
        out_shape=jax.ShapeDtypeStruct((b, GQA_RATIO, t, LANES), jnp.float32),
        scratch_shapes=[pltpu.VMEM((rows, 1), jnp.float32), pltpu.VMEM((rows, 1), jnp.float32),
                        pltpu.VMEM((rows, LANES), jnp.float32)],
        compiler_params=pltpu.CompilerParams(dimension_semantics=("arbitrary", "arbitrary")),
        name="nsa_prompt",
    )(qc, qr, blk, blk, ks, ks, kw, kw, gtp)
    out = out.reshape(b, GQA_RATIO, t, N_KV_HEADS, HEAD_DIM).transpose(0, 2, 3, 1, 4)
    return out.reshape(b, t, NSA_WIDTH)


KV_ROW = 2 * N_KV_HEADS * HEAD_DIM
SEG_W = CMP_STRIDE * KV_ROW
SAMPLE_PG = 32


def _topk_rows(score, n_valid, k):
    rows, width = score.shape
    jf = lax.broadcasted_iota(jnp.int32, (rows, width), 1).astype(jnp.float32)
    score = jnp.where(jf < n_valid, score, -3e38)
    sel = jnp.zeros((rows, width), jnp.float32)
    for _ in range(k):
        best = jnp.max(score, axis=1, keepdims=True)
        first = jnp.min(jnp.where(score == best, jf, float(width)), axis=1, keepdims=True)
        hit = jf == first
        sel = jnp.where(hit, 1.0, sel)
        score = jnp.where(hit, -3e38, score)
    return sel


def _sample_cmp_kernel(pt_ref, *refs, pg, n_pages, s_len):
    page_refs = refs[:pg]
    (new_ref, w1_ref, b1_ref, w2_ref, b2_ref, qc_ref, oc_ref, sel_ref, x_scr, pseg_scr) = refs[pg:]
    j = pl.program_id(1)
    n_steps = n_pages // pg
    seg_per_page = PAGE_SIZE // CMP_STRIDE
    m_rows = pg * seg_per_page
    past = n_pages * PAGE_SIZE
    n_seg = past // CMP_STRIDE
    rows = GQA_RATIO * s_len

    for i in range(pg):
        x_scr[i * seg_per_page:(i + 1) * seg_per_page, :] = page_refs[i][0]
    x_scr[m_rows:m_rows + 8, :] = new_ref[0]
    for c in range(2):
        xc = jnp.concatenate([x_scr[:, r * KV_ROW + c * LANES:r * KV_ROW + (c + 1) * LANES]
                              for r in range(CMP_STRIDE)], axis=1).astype(jnp.bfloat16)
        pseg_scr[c, pl.ds(pl.multiple_of(j * m_rows, m_rows), m_rows + 8), :] = jnp.dot(
            xc, w1_ref[c], preferred_element_type=jnp.float32)

    @pl.when(j == n_steps - 1)
    def _():
        kv = []
        for c in range(2):
            acc = jnp.zeros((n_seg, LANES), jnp.float32) + b2_ref[c]
            for g in range(N_KV_HEADS):
                lo = g * N_HALF * CMP_HIDDEN
                hid = (pseg_scr[c, 0:n_seg, lo:lo + CMP_HIDDEN]
                       + pseg_scr[c, 1:n_seg + 1, lo + CMP_HIDDEN:lo + 2 * CMP_HIDDEN] + b1_ref[c])
                hid = jax.nn.gelu(hid)
                acc = acc + jnp.dot(hid.astype(jnp.bfloat16), w2_ref[c, g], preferred_element_type=jnp.float32)
            kv.append(acc)
        k_c, v_c = kv
        n_sb = past // SEL_BLOCK + -(-s_len // SEL_BLOCK)
        width = sel_ref.shape[-1]
        ratio = SEL_BLOCK // CMP_STRIDE
        tok = lax.broadcasted_iota(jnp.int32, (rows, n_seg), 0) % s_len
        blk_end = lax.broadcasted_iota(jnp.int32, (rows, n_seg), 1) * CMP_STRIDE + (CMP_BLOCK - 1)
        cmask = blk_end <= past + tok
        same_tok = (lax.broadcasted_iota(jnp.int32, (rows, rows), 0) % s_len
                    == lax.broadcasted_iota(jnp.int32, (rows, rows), 1) % s_len).astype(jnp.float32)
        nn = lax.broadcasted_iota(jnp.int32, (n_seg, width), 0)
        jj = lax.broadcasted_iota(jnp.int32, (n_seg, width), 1)
        overlap = ((nn >= ratio * jj - (N_HALF - 1)) & (nn <= ratio * jj + ratio - 1)).astype(jnp.float32)
        jb = lax.broadcasted_iota(jnp.int32, (rows, width), 1)
        qp = past + lax.broadcasted_iota(jnp.int32, (rows, width), 0) % s_len
        cur = qp // SEL_BLOCK
        forced = (jb == 0) | (jb == cur) | (jb == cur - 1)
        valid = jb * SEL_BLOCK <= qp
        for g in range(N_KV_HEADS):
            s = _dot_nt(qc_ref[0, g], k_c, HIGHEST)
            s = jnp.where(cmask, s, NEG_INF)
            mx = jnp.max(s, axis=-1, keepdims=True)
            p = jnp.where(cmask, jnp.exp(s - mx), 0.0)
            p = p / jnp.maximum(jnp.sum(p, axis=-1, keepdims=True), 1e-30)
            oc_ref[0, g] = jnp.dot(p.astype(jnp.bfloat16), v_c.astype(jnp.bfloat16),
                                   preferred_element_type=jnp.float32)
            p_grp = jnp.dot(same_tok, p, precision=HIGHEST, preferred_element_type=jnp.float32)
            score = jnp.dot(p_grp, overlap, precision=HIGHEST, preferred_element_type=jnp.float32)
            score = jnp.where(forced, FORCED_SCORE, score)
            score = jnp.where(valid, score, INVALID_SCORE)
            sel_ref[0, g] = _topk_rows(score, n_sb, min(N_SEL, n_sb))


def _sample_attn_kernel(pt_ref, *refs, pg, n_pages, s_len):
    page_refs = refs[:pg]
    (qr_ref, sel_ref, tail_ref, win_ref, wnew_ref, oc_ref, gt_ref, o_ref,
     k_scr, v_scr, m_scr, l_scr, acc_scr) = refs[pg:]
    j = pl.program_id(1)
    n_steps = n_pages // pg
    rows = GQA_RATIO * s_len
    keys = pg * PAGE_SIZE
    past = n_pages * PAGE_SIZE
    width = sel_ref.shape[-1]

    @pl.when(j == 0)
    def _():
        m_scr[...] = jnp.full(m_scr.shape, NEG_INF, jnp.float32)
        l_scr[...] = jnp.zeros(l_scr.shape, jnp.float32)
        acc_scr[...] = jnp.zeros(acc_scr.shape, jnp.float32)

    for i in range(pg):
        k_scr[i * PAGE_SIZE:(i + 1) * PAGE_SIZE, :] = page_refs[i][0, :, 0:LANES].astype(jnp.bfloat16)
        v_scr[i * PAGE_SIZE:(i + 1) * PAGE_SIZE, :] = page_refs[i][0, :, LANES:2 * LANES].astype(jnp.bfloat16)
    blk_of_key = (j * keys + lax.broadcasted_iota(jnp.int32, (width, keys), 1)) // SEL_BLOCK
    expand = (blk_of_key == lax.broadcasted_iota(jnp.int32, (width, keys), 0)).astype(jnp.bfloat16)
    for g in range(N_KV_HEADS):
        q = qr_ref[0, g]
        s = _dot_nt(q, k_scr[...])
        mask = jnp.dot(sel_ref[0, g].astype(jnp.bfloat16), expand, preferred_element_type=jnp.float32) > 0.5
        s = jnp.where(mask, s, NEG_INF)
        m_old = m_scr[g]
        m_new = jnp.maximum(m_old, jnp.max(s, axis=-1, keepdims=True))
        alpha = jnp.exp(m_old - m_new)
        p = jnp.where(mask, jnp.exp(s - m_new), 0.0)
        l_scr[g] = alpha * l_scr[g] + jnp.sum(p, axis=-1, keepdims=True)
        acc_scr[g] = alpha * acc_scr[g] + jnp.dot(p.astype(jnp.bfloat16), v_scr[...],
                                                  preferred_element_type=jnp.float32)
        m_scr[g] = m_new

    @pl.when(j == n_steps - 1)
    def _():
        tok8 = lax.broadcasted_iota(jnp.int32, (rows, 8), 0) % s_len
        new_ok = lax.broadcasted_iota(jnp.int32, (rows, 8), 1) <= tok8
        wb = win_ref.shape[1]
        tokw = lax.broadcasted_iota(jnp.int32, (rows, wb), 0) % s_len
        kpos = past - wb + lax.broadcasted_iota(jnp.int32, (rows, wb), 1)
        win_ok = (kpos > past + tokw - WINDOW) & (kpos >= 0)
        gates = jax.nn.sigmoid(gt_ref[0])
        for g in range(N_KV_HEADS):
            q = qr_ref[0, g]
            tail_sel = sel_ref[0, g][:, past // SEL_BLOCK:past // SEL_BLOCK + 1] > 0.5
            t_mask = new_ok & tail_sel
            s_t = jnp.where(t_mask, _dot_nt(q, tail_ref[0, :, 0:LANES].astype(jnp.bfloat16)), NEG_INF)
            m_old = m_scr[g]
            m_new = jnp.maximum(m_old, jnp.max(s_t, axis=-1, keepdims=True))
            alpha = jnp.exp(m_old - m_new)
            p_t = jnp.where(t_mask, jnp.exp(s_t - m_new), 0.0)
            l_s = alpha * l_scr[g] + jnp.sum(p_t, axis=-1, keepdims=True)
            o_s = (alpha * acc_scr[g] + jnp.dot(p_t.astype(jnp.bfloat16),
                                                tail_ref[0, :, LANES:2 * LANES].astype(jnp.bfloat16),
                                                preferred_element_type=jnp.float32)) / jnp.maximum(l_s, 1e-30)
            s_w = jnp.where(win_ok, _dot_nt(q, win_ref[0, :, 0:LANES].astype(jnp.bfloat16)), NEG_INF)
            s_n = jnp.where(new_ok, _dot_nt(q, wnew_ref[0, :, 0:LANES].astype(jnp.bfloat16)), NEG_INF)
            mw = jnp.maximum(jnp.max(s_w, axis=-1, keepdims=True), jnp.max(s_n, axis=-1, keepdims=True))
            p_w = jnp.where(win_ok, jnp.exp(s_w - mw), 0.0)
            p_n = jnp.where(new_ok, jnp.exp(s_n - mw), 0.0)
            l_w = jnp.sum(p_w, axis=-1, keepdims=True) + jnp.sum(p_n, axis=-1, keepdims=True)
            o_w = (jnp.dot(p_w.astype(jnp.bfloat16), win_ref[0, :, LANES:2 * LANES].astype(jnp.bfloat16),
                           preferred_element_type=jnp.float32)
                   + jnp.dot(p_n.astype(jnp.bfloat16), wnew_ref[0, :, LANES:2 * LANES].astype(jnp.bfloat16),
                             preferred_element_type=jnp.float32)) / jnp.maximum(l_w, 1e-30)
            gl = gates[g]
            o_ref[0, g] = gl[:, 0:1] * oc_ref[0, g] + gl[:, 1:2] * o_s + gl[:, 2:3] * o_w


def _nsa_sample_pallas(q, q_rot, kv_cmp, kv_sel, kv_win, gt, cache_cmp, cache_sel, win_buf, page_table,
                       w1, b1, w2, b2):
    pg = SAMPLE_PG
    db, s_len = q.shape[0], q.shape[1]
    n_phys = cache_cmp.shape[0]
    n_pages = page_table.shape[1]
    assert n_pages % pg == 0 and s_len <= 8 and s_len <= CMP_STRIDE
    past = n_pages * PAGE_SIZE
    n_seg = past // CMP_STRIDE
    n_sb = past // SEL_BLOCK + 1
    width = -(-n_sb // LANES) * LANES
    rows = GQA_RATIO * s_len
    seg_per_page = PAGE_SIZE // CMP_STRIDE

    def group_rows(a, dtype):
        a = (a * ATTN_SCALE).reshape(db, s_len, N_KV_HEADS, GQA_RATIO, HEAD_DIM).transpose(0, 2, 3, 1, 4)
        eye = jnp.eye(N_KV_HEADS, dtype=a.dtype)[None, :, None, None, :, None]
        return (a[:, :, :, :, None, :] * eye).reshape(db, N_KV_HEADS, rows, LANES).astype(dtype)

    def pad8(a):
        return jnp.pad(a.reshape(db, s_len, KV_ROW), ((0, 0), (0, 8 - s_len), (0, 0)))

    qc = group_rows(q, jnp.float32)
    qr = group_rows(q_rot, jnp.bfloat16)
    new_seg = jnp.pad(kv_cmp.reshape(db, 1, s_len * KV_ROW), ((0, 0), (0, 7), (0, SEG_W - s_len * KV_ROW)))
    eye_g = jnp.eye(N_KV_HEADS, dtype=w1.dtype)
    w1t = w1.transpose(0, 2, 3, 1, 4)
    w1_bd = (w1t[:, :, None, :, None, :, :] * eye_g[None, None, :, None, :, None, None]).reshape(
        2, CMP_STRIDE * N_KV_HEADS * HEAD_DIM, N_KV_HEADS * N_HALF * CMP_HIDDEN).astype(jnp.bfloat16)
    w2_g = (w2[:, None, :, None, :] * eye_g[None, :, None, :, None]).reshape(
        2, N_KV_HEADS, CMP_HIDDEN, LANES).astype(jnp.bfloat16)
    b2_t = jnp.tile(b2, (1, N_KV_HEADS)).reshape(2, 1, LANES)
    b1_r = b1.reshape(2, 1, CMP_HIDDEN)

    def page_specs(block):
        return [pl.BlockSpec(block, functools.partial(lambda b, j, pt, i: (pt[b, j * pg + i], 0, 0), i=i))
                for i in range(pg)]

    def per_seq(shape):
        nd = len(shape)
        return pl.BlockSpec((1,) + shape, lambda b, j, pt: (b,) + (0,) * nd)

    def const(shape):
        nd = len(shape)
        return pl.BlockSpec(shape, lambda b, j, pt: (0,) * nd)

    m_rows = pg * seg_per_page
    o_c, sel = pl.pallas_call(
        functools.partial(_sample_cmp_kernel, pg=pg, n_pages=n_pages, s_len=s_len),
        grid_spec=pltpu.PrefetchScalarGridSpec(
            num_scalar_prefetch=1, grid=(db, n_pages // pg),
            in_specs=page_specs((1, seg_per_page, SEG_W)) + [
                per_seq((8, SEG_W)), const(w1_bd.shape), const(b1_r.shape), const(w2_g.shape), const(b2_t.shape),
                per_seq((N_KV_HEADS, rows, LANES))],
            out_specs=[per_seq((N_KV_HEADS, rows, LANES)), per_seq((N_KV_HEADS, rows, width))],
            scratch_shapes=[pltpu.VMEM((m_rows + 8, SEG_W), jnp.float32),
                            pltpu.VMEM((2, n_seg + 8, N_KV_HEADS * N_HALF * CMP_HIDDEN), jnp.float32)]),
        out_shape=[jax.ShapeDtypeStruct((db, N_KV_HEADS, rows, LANES), jnp.float32),
                   jax.ShapeDtypeStruct((db, N_KV_HEADS, rows, width), jnp.float32)],
        compiler_params=pltpu.CompilerParams(dimension_semantics=("arbitrary", "arbitrary"),
                                             vmem_limit_bytes=56 * 1024 * 1024),
        name="sample_cmp_select",
    )(page_table, *([cache_cmp.reshape(n_phys, seg_per_page, SEG_W)] * pg), new_seg, w1_bd, b1_r, w2_g, b2_t, qc)

    gl = jnp.pad(gt.astype(jnp.float32).reshape(db, s_len, N_KV_HEADS, GQA_RATIO, 3).transpose(0, 2, 3, 1, 4)
                 .reshape(db, N_KV_HEADS, rows, 3), ((0, 0), (0, 0), (0, 0), (0, LANES - 3)))
    wb = win_buf.shape[1]
    out = pl.pallas_call(
        functools.partial(_sample_attn_kernel, pg=pg, n_pages=n_pages, s_len=s_len),
        grid_spec=pltpu.PrefetchScalarGridSpec(
            num_scalar_prefetch=1, grid=(db, n_pages // pg),
            in_specs=page_specs((1, PAGE_SIZE, KV_ROW)) + [
                per_seq((N_KV_HEADS, rows, LANES)), per_seq((N_KV_HEADS, rows, width)), per_seq((8, KV_ROW)),
                per_seq((wb, KV_ROW)), per_seq((8, KV_ROW)), per_seq((N_KV_HEADS, rows, LANES)),
                per_seq((N_KV_HEADS, rows, LANES))],
            out_specs=per_seq((N_KV_HEADS, rows, LANES)),
            scratch_shapes=[pltpu.VMEM((pg * PAGE_SIZE, LANES), jnp.bfloat16),
                            pltpu.VMEM((pg * PAGE_SIZE, LANES), jnp.bfloat16),
                            pltpu.VMEM((N_KV_HEADS, rows, 1), jnp.float32),
                            pltpu.VMEM((N_KV_HEADS, rows, 1), jnp.float32),
                            pltpu.VMEM((N_KV_HEADS, rows, LANES), jnp.float32)]),
        out_shape=jax.ShapeDtypeStruct((db, N_KV_HEADS, rows, LANES), jnp.float32),
        compiler_params=pltpu.CompilerParams(dimension_semantics=("arbitrary", "arbitrary")),
        name="sample_sel_win_attn",
    )(page_table, *([cache_sel.reshape(n_phys, PAGE_SIZE, KV_ROW)] * pg), qr, sel, pad8(kv_sel),
      win_buf.reshape(db, wb, KV_ROW), pad8(kv_win), o_c, gl)
    out = out.reshape(db, N_KV_HEADS, GQA_RATIO, s_len, N_KV_HEADS, HEAD_DIM)
    out = jnp.stack([out[:, g, :, :, g, :] for g in range(N_KV_HEADS)], axis=1)
    return out.transpose(0, 3, 1, 2, 4).reshape(db, s_len, NSA_WIDTH)


ROUTER_TT = 256
MOE_BM = 256


def _router_kernel(x_ref, mix_ref, wout_ref, g_ref, wrt_ref, br_ref,
                   x1_ref, h_ref, eidx_ref, gate_ref, rank_ref, cnt_ref, wout_bf, run_scr):
    tt = x_ref.shape[0]
    n_e = wrt_ref.shape[0]

    @pl.when(pl.program_id(0) == 0)
    def _():
        wout_bf[...] = wout_ref[...].astype(jnp.bfloat16)
        run_scr[...] = jnp.zeros_like(run_scr)

    x1 = x_ref[...] + jnp.dot(mix_ref[...].astype(jnp.bfloat16), wout_bf[...], preferred_element_type=jnp.float32)
    x1_ref[...] = x1
    hn = x1 * lax.rsqrt(jnp.mean(x1 * x1, axis=-1, keepdims=True) + RMS_EPS) * g_ref[...]
    h_ref[...] = hn.astype(jnp.bfloat16)

    score = _dot_nt(wrt_ref[...], hn, HIGHEST) + br_ref[...]
    ef = lax.broadcasted_iota(jnp.int32, (n_e, tt), 0).astype(jnp.float32)
    vals, hits = [], []
    for k in range(TOP_K):
        best = jnp.max(score, axis=0, keepdims=True)
        first = jnp.min(jnp.where(score == best, ef, float(n_e)), axis=0, keepdims=True)
        hit = ef == first
        vals.append(best)
        hits.append(hit)
        eidx_ref[k:k + 1, :] = first.astype(jnp.int32)
        score = jnp.where(hit, -3e38, score)
    exps = [jnp.exp(v - vals[0]) for v in vals]
    denom = sum(exps[1:], exps[0])
    for k in range(TOP_K):
        gate_ref[k:k + 1, :] = exps[k] / denom

    chosen = functools.reduce(jnp.logical_or, hits)
    before = (lax.broadcasted_iota(jnp.int32, (tt, tt), 0) < lax.broadcasted_iota(jnp.int32, (tt, tt), 1))
    earlier = jnp.dot(chosen.astype(jnp.bfloat16), before.astype(jnp.bfloat16), preferred_element_type=jnp.float32)
    pos = earlier + run_scr[...]
    for k in range(TOP_K):
        rank_ref[k:k + 1, :] = jnp.sum(jnp.where(hits[k], pos, 0.0), axis=0, keepdims=True).astype(jnp.int32)
    run_scr[...] = run_scr[...] + jnp.sum(chosen.astype(jnp.float32), axis=1, keepdims=True)
    cnt_ref[...] = jnp.broadcast_to(run_scr[...], cnt_ref.shape).astype(jnp.int32)


def _router_pallas(x, mix, w_out, g_ffn, w_router, b_router):
    n, d = x.shape
    c = mix.shape[1]
    n_e = w_router.shape[1]
    tt = ROUTER_TT
    assert n % tt == 0

    def row(w):
        return pl.BlockSpec((tt, w), lambda i: (i, 0))

    def full(a, b):
        return pl.BlockSpec((a, b), lambda i: (0, 0))

    k4 = pl.BlockSpec((TOP_K, tt), lambda i: (0, i))
    return pl.pallas_call(
        _router_kernel,
        grid=(n // tt,),
        in_specs=[row(d), row(c), full(c, d), full(1, d), full(n_e, d), full(n_e, 1)],
        out_specs=[row(d), row(d), k4, k4, k4, full(n_e, LANES)],
        out_shape=[jax.ShapeDtypeStruct((n, d), jnp.float32), jax.ShapeDtypeStruct((n, d), jnp.bfloat16),
                   jax.ShapeDtypeStruct((TOP_K, n), jnp.int32), jax.ShapeDtypeStruct((TOP_K, n), jnp.float32),
                   jax.ShapeDtypeStruct((TOP_K, n), jnp.int32), jax.ShapeDtypeStruct((n_e, LANES), jnp.int32)],
        scratch_shapes=[pltpu.VMEM((c, d), jnp.bfloat16), pltpu.VMEM((n_e, 1), jnp.float32)],
        compiler_params=pltpu.CompilerParams(dimension_semantics=("arbitrary",)),
        name="outproj_router",
    )(x, mix, w_out, g_ffn.reshape(1, d), w_router.T, b_router.reshape(n_e, 1))


def _expert_kernel(blk_e_ref, n_used_ref, x_ref, wup_ref, bup_ref, wdn_ref, bdn_ref, o_ref, wup_bf, wdn_bf):
    i = pl.program_id(0)
    d_ff = wdn_ref.shape[1]

    @pl.when(i < n_used_ref[0])
    def _():
        e = blk_e_ref[i]
        prev = blk_e_ref[jnp.maximum(i - 1, 0)]

        @pl.when((i == 0) | (e != prev))
        def _():
            wup_bf[...] = wup_ref[0].astype(jnp.bfloat16)
            wdn_bf[...] = wdn_ref[0].astype(jnp.bfloat16)

        u = jnp.dot(x_ref[...], wup_bf[...], preferred_element_type=jnp.float32) + bup_ref[0]
        glu = jnp.minimum(u[:, :d_ff], SWIGLU_LIMIT)
        lin = jnp.clip(u[:, d_ff:], -SWIGLU_LIMIT, SWIGLU_LIMIT)
        a = glu * jax.nn.sigmoid(SWIGLU_ALPHA * glu) * (lin + 1.0)
        o_ref[...] = jnp.dot(a.astype(jnp.bfloat16), wdn_bf[...], preferred_element_type=jnp.float32) + bdn_ref[0]

    @pl.when(i >= n_used_ref[0])
    def _():
        o_ref[...] = jnp.zeros_like(o_ref)


def _expert_vmem_bytes(bm, d, f2, d_ff):
    weights = 2 * 4 * (d * f2 + d_ff * d) + 2 * (d * f2 + d_ff * d)
    rows = 2 * bm * d * (2 + 4)
    temps = bm * f2 * 4 * 2 + bm * d_ff * (4 + 2)
    return weights + rows + temps


def _experts_pallas(xs, blk_e, n_used, w_up, b_up, w_down, b_down, bm):
    n_slots, d = xs.shape
    n_e, _, f2 = w_up.shape
    d_ff = w_down.shape[1]
    grid_spec = pltpu.PrefetchScalarGridSpec(
        num_scalar_prefetch=2,
        grid=(n_slots // bm,),
        in_specs=[pl.BlockSpec((bm, d), lambda i, be, nu: (i, 0)),
                  pl.BlockSpec((1, d, f2), lambda i, be, nu: (be[i], 0, 0)),
                  pl.BlockSpec((1, 1, f2), lambda i, be, nu: (be[i], 0, 0)),
                  pl.BlockSpec((1, d_ff, d), lambda i, be, nu: (be[i], 0, 0)),
                  pl.BlockSpec((1, 1, d), lambda i, be, nu: (be[i], 0, 0))],
        out_specs=pl.BlockSpec((bm, d), lambda i, be, nu: (i, 0)),
        scratch_shapes=[pltpu.VMEM((d, f2), jnp.bfloat16), pltpu.VMEM((d_ff, d), jnp.bfloat16)],
    )
    vmem_limit = _expert_vmem_bytes(bm, d, f2, d_ff) * 5 // 4
    return pl.pallas_call(
        _expert_kernel,
        grid_spec=grid_spec,
        out_shape=jax.ShapeDtypeStruct((n_slots, d), jnp.float32),
        compiler_params=pltpu.CompilerParams(dimension_semantics=("arbitrary",), vmem_limit_bytes=vmem_limit),
        name="expert_mlp",
    )(blk_e, n_used, xs, w_up, b_up.reshape(n_e, 1, f2), w_down, b_down.reshape(n_e, 1, d))


def _combine_norm_kernel(x1_ref, og_ref, gate_ref, g_ref, o_ref):
    y = x1_ref[...]
    gates = gate_ref[...]
    for k in range(TOP_K):
        y = y + gates[:, k:k + 1] * og_ref[k]
    o_ref[...] = y * lax.rsqrt(jnp.mean(y * y, axis=-1, keepdims=True) + RMS_EPS) * g_ref[...]


def _combine_norm_pallas(x1, og, gate_t, g_final):
    n, d = x1.shape
    tt = ROUTER_TT
    return pl.pallas_call(
        _combine_norm_kernel,
        grid=(n // tt,),
        in_specs=[pl.BlockSpec((tt, d), lambda i: (i, 0)), pl.BlockSpec((TOP_K, tt, d), lambda i: (0, i, 0)),
                  pl.BlockSpec((tt, TOP_K), lambda i: (i, 0)), pl.BlockSpec((1, d), lambda i: (0, 0))],
        out_specs=pl.BlockSpec((tt, d), lambda i: (i, 0)),
        out_shape=jax.ShapeDtypeStruct((n, d), jnp.float32),
        compiler_params=pltpu.CompilerParams(dimension_semantics=("arbitrary",)),
        name="combine_final_norm",
    )(x1, og, gate_t, g_final.reshape(1, d))


def _finish_pallas(x, mix, w_out, g_ffn, w_router, b_router, w_up, b_up, w_down, b_down, g_final):
    n, d = x.shape
    n_e = w_router.shape[1]
    bm = MOE_BM
    x1, h, eidx, gate, rank, cnt = _router_pallas(x, mix, w_out, g_ffn, w_router, b_router)
    counts = cnt[:, 0]
    padded = (counts + bm - 1) // bm * bm
    pad_end = jnp.cumsum(padded)
    gstart = pad_end - padded
    dest = gstart[eidx] + rank
    nb = -(-(n * TOP_K) // bm) + n_e
    blk_e = jnp.minimum(jnp.searchsorted(pad_end, jnp.arange(nb, dtype=jnp.int32) * bm, side='right'),
                        n_e - 1).astype(jnp.int32)
    n_used = (pad_end[-1] // bm).astype(jnp.int32).reshape(1)
    tok = jnp.broadcast_to(jnp.arange(n, dtype=jnp.int32)[None, :], (TOP_K, n))
    src = jnp.zeros((nb * bm,), jnp.int32).at[dest.reshape(-1)].set(tok.reshape(-1))
    out = _experts_pallas(h[src], blk_e, n_used, w_up, b_up, w_down, b_down, bm)
    return _combine_norm_pallas(x1, out[dest], gate.T, g_final)


def kernel(x_prompt, x_sample, cache_cmp_kv, cache_sel_kv, cache_win_kv, state_ret, page_table, g_attn, w_in, w_cmp1, b_cmp1, w_cmp2, b_cmp2, w_out, g_ffn, w_router, b_router, w_up, b_up, w_down, b_down, g_final):
    seq = x_prompt.shape[1]
    past = page_table.shape[1] * PAGE_SIZE
    pos_p = jnp.arange(seq, dtype=jnp.int32)
    pos_s = past + jnp.arange(x_sample.shape[1], dtype=jnp.int32)
    log_g = _ret_log_decay()
    assert DEPTH == 1
    l = 0
    d = x_prompt.shape[-1]
    q, kv_cmp_p, kv_sel_p, kv_win_p, _, r_q, r_k, r_v, r_g, gt = _mixer_inputs(x_prompt, pos_p, g_attn[l], w_in[l])
    blocks = _compress_blocks(_segment_proj(kv_cmp_p, w_cmp1[l]), b_cmp1[l], w_cmp2[l], b_cmp2[l])
    o_nsa = _nsa_prompt_pallas(q, _rope(q, pos_p), blocks, kv_sel_p, kv_win_p, gt)
    ret_p, o_r = _ret_prompt(r_q, r_k, r_v, log_g)
    mix_p = jnp.concatenate([o_nsa, _ret_out(o_r, r_g)], axis=-1)
    q, kv_cmp_s, kv_sel_s, kv_win_s, _, r_q, r_k, r_v, r_g, gt = _mixer_inputs(x_sample, pos_s, g_attn[l], w_in[l])
    o_nsa_s = _nsa_sample_pallas(q, _rope(q, pos_s), kv_cmp_s, kv_sel_s, kv_win_s, gt, cache_cmp_kv[:, l],
                                 cache_sel_kv[:, l], cache_win_kv[:, l], page_table,
                                 w_cmp1[l], b_cmp1[l], w_cmp2[l], b_cmp2[l])
    win_s = jnp.concatenate([cache_win_kv[:, l], kv_win_s], axis=1)[:, x_sample.shape[1]:]
    ret_s, o_r = _ret_chunk(state_ret[:, l].astype(jnp.float32), r_q.astype(jnp.float32), r_k.astype(jnp.float32), r_v.astype(jnp.float32), log_g)
    mix_s = jnp.concatenate([o_nsa_s, _ret_out(o_r, r_g)], axis=-1)
    n_p = x_prompt.shape[0] * seq
    x_all = jnp.concatenate([x_prompt.reshape(n_p, d), x_sample.reshape(-1, d)], axis=0)
    mix_all = jnp.concatenate([mix_p.reshape(n_p, MIX_WIDTH), mix_s.reshape(-1, MIX_WIDTH)], axis=0)
    y = _finish_pallas(x_all, mix_all, w_out[l], g_ffn[l], w_router[l], b_router[l], w_up[l], b_up[l], w_down[l], b_down[l], g_final)
    y_prompt = y[:n_p].reshape(x_prompt.shape)
    y_sample = y[n_p:].reshape(x_sample.shape)
    win_p = kv_win_p[:, seq - min(WINDOW, seq):]
    return (y_prompt, y_sample, kv_cmp_p[:, None], kv_sel_p[:, None], win_p[:, None], ret_p[:, None],
            kv_cmp_s[:, None], kv_sel_s[:, None], win_s[:, None], ret_s[:, None])
```

```python
import functools
import jax, jax.numpy as jnp
from jax import lax
import numpy as np
from jax.experimental import pallas as pl
from jax.experimental.pallas import tpu as pltpu

D_MODEL = 1024
BATCH = 4
SEQ = 4096
DEPTH = 1
DEC_BATCH = 128
DEC_SEQ = 4
PAST_LEN = 8192
PAGE_SIZE = 128

HEAD_DIM = 64
N_NSA_HEADS = 8
N_KV_HEADS = 2
GQA_RATIO = N_NSA_HEADS // N_KV_HEADS
CMP_BLOCK = 32
CMP_STRIDE = 16
N_HALF = CMP_BLOCK // CMP_STRIDE
CMP_HIDDEN = 256
SEL_BLOCK = 64
N_SEL = 16
WINDOW = 512
NSA_Q_BLOCK = 64
WIN_Q_BLOCK = 128
N_RET_HEADS = 4
RET_DK = 64
RET_DV = 128
RET_CHUNK = 128
N_EXPERTS = 32
TOP_K = 4
D_FF = D_MODEL
SWIGLU_ALPHA = 1.702
SWIGLU_LIMIT = 7.0
MOE_BLOCK = 128
ROPE_THETA = 10000.0
RMS_EPS = 1e-5
NEG_INF = -1e30
FORCED_SCORE = 1e6
INVALID_SCORE = -1e9
NSA_WIDTH = N_NSA_HEADS * HEAD_DIM
KV_WIDTH = N_KV_HEADS * HEAD_DIM
RET_QK_WIDTH = N_RET_HEADS * RET_DK
RET_WIDTH = N_RET_HEADS * RET_DV
IN_SIZES = (NSA_WIDTH, KV_WIDTH, KV_WIDTH, KV_WIDTH, KV_WIDTH, KV_WIDTH, KV_WIDTH, 3 * N_NSA_HEADS, RET_QK_WIDTH, RET_QK_WIDTH, RET_WIDTH, RET_WIDTH)
IN_TOTAL = NSA_WIDTH + 6 * KV_WIDTH + 3 * N_NSA_HEADS + 2 * RET_QK_WIDTH + 2 * RET_WIDTH
MIX_WIDTH = NSA_WIDTH + RET_WIDTH
ATTN_SCALE = HEAD_DIM ** -0.5


def _rmsnorm(x, g):
    xf = x.astype(jnp.float32)
    y = xf * lax.rsqrt(jnp.mean(xf * xf, axis=-1, keepdims=True) + RMS_EPS)
    return (y * g.astype(jnp.float32)).astype(x.dtype)


def _rope(x, pos):
    half = x.shape[-1] // 2
    inv = ROPE_THETA ** (-jnp.arange(half, dtype=jnp.float32) / half)
    ang = pos.astype(jnp.float32)[:, None] * inv[None, :]
    cos = jnp.cos(ang)[None, :, None, :]
    sin = jnp.sin(ang)[None, :, None, :]
    xf = x.astype(jnp.float32)
    x1, x2 = xf[..., :half], xf[..., half:]
    return jnp.concatenate([x1 * cos - x2 * sin, x2 * cos + x1 * sin], axis=-1).astype(x.dtype)


def _masked_softmax(s, mask):
    s = jnp.where(mask, s, NEG_INF)
    m = jnp.max(s, axis=-1, keepdims=True)
    p = jnp.where(mask, jnp.exp(s - m), 0.0)
    return p / jnp.maximum(jnp.sum(p, axis=-1, keepdims=True), 1e-30)


def _mixer_inputs(x, pos, g_attn, w_in):
    b, t = x.shape[0], x.shape[1]
    h = _rmsnorm(x, g_attn)
    proj = jnp.einsum('btd,dc->btc', h, w_in)
    offs = [int(o) for o in np.cumsum(np.array(IN_SIZES))[:-1]]
    q, kc, vc, ks, vs, kw, vw, gt, rq, rk, rv, rg = jnp.split(proj, offs, axis=-1)

    def heads(a, n, d):
        return a.reshape(b, t, n, d)

    def kvh(a):
        return heads(a, N_KV_HEADS, HEAD_DIM)

    kv_cmp = jnp.stack([kvh(kc), kvh(vc)], axis=2)
    kv_sel = jnp.stack([_rope(kvh(ks), pos), kvh(vs)], axis=2)
    kv_win = jnp.stack([_rope(kvh(kw), pos), kvh(vw)], axis=2)
    gates = jax.nn.sigmoid(gt.astype(jnp.float32)).reshape(b, t, N_NSA_HEADS, 3)
    r_q = _rope(heads(rq, N_RET_HEADS, RET_DK), pos)
    r_k = _rope(heads(rk, N_RET_HEADS, RET_DK), pos) * (RET_DK ** -0.5)
    r_v = heads(rv, N_RET_HEADS, RET_DV)
    return heads(q, N_NSA_HEADS, HEAD_DIM), kv_cmp, kv_sel, kv_win, gates, r_q, r_k, r_v, rg, gt


def _segment_proj(kv_rows, w1):
    b, l = kv_rows.shape[0], kv_rows.shape[1]
    seg = kv_rows.reshape(b, l // CMP_STRIDE, CMP_STRIDE, 2, N_KV_HEADS, HEAD_DIM)
    return jnp.einsum('bsrcgd,chrdk->bscghk', seg, w1)


def _compress_blocks(p_seg, b1, w2, b2):
    n_cmp = p_seg.shape[1] - N_HALF + 1
    hid = sum(p_seg[:, h:h + n_cmp, :, :, h, :] for h in range(N_HALF)) + b1[None, None, :, None, :]
    hid = jax.nn.gelu(hid.astype(jnp.float32))
    out = jnp.einsum('bncgk,ckd->bncgd', hid, w2.astype(jnp.float32))
    return out + b2.astype(jnp.float32)[None, None, :, None, :]


def _cmp_attend(q, blocks, q_pos):
    b, nq = q.shape[0], q.shape[1]
    n_cmp = blocks.shape[1]
    qg = q.astype(jnp.float32).reshape(b, nq, N_KV_HEADS, GQA_RATIO, HEAD_DIM)
    s = jnp.einsum('bqgrd,bngd->bqgrn', qg, blocks[:, :, 0]) * ATTN_SCALE
    blk_end = jnp.arange(n_cmp) * CMP_STRIDE + (CMP_BLOCK - 1)
    mask = blk_end[None, :] <= q_pos[:, None]
    p = _masked_softmax(s, mask[None, :, None, None, :])
    o = jnp.einsum('bqgrn,bngd->bqgrd', p, blocks[:, :, 1])
    return o.reshape(b, nq, N_NSA_HEADS, HEAD_DIM), jnp.sum(p, axis=3)


def _select_blocks(p_grp, q_pos, n_sb):
    ratio = SEL_BLOCK // CMP_STRIDE
    ov = N_HALF - 1
    n_cmp = p_grp.shape[-1]
    pp = jnp.pad(p_grp, ((0, 0), (0, 0), (0, 0), (ov, ratio * n_sb - n_cmp)))
    score = sum(pp[..., o:o + ratio * n_sb:ratio] for o in range(ratio + ov))
    blk = jnp.arange(n_sb)[None, :]
    qp = q_pos[:, None]
    cur = qp // SEL_BLOCK
    forced = (blk == 0) | (blk == cur) | (blk == cur - 1)
    valid = blk * SEL_BLOCK <= qp
    score = jnp.where(forced[None, :, None, :], FORCED_SCORE, score)
    score = jnp.where(valid[None, :, None, :], score, INVALID_SCORE)
    _, idx = lax.top_k(score, min(N_SEL, n_sb))
    return idx


def _sel_attend(q_rot, kv_g, idx, q_pos):
    b, nq = q_rot.shape[0], q_rot.shape[1]
    m = idx.shape[-1] * SEL_BLOCK
    qg = q_rot.astype(jnp.float32).reshape(b, nq, N_KV_HEADS, GQA_RATIO, HEAD_DIM)
    k = kv_g[..., 0, :].astype(jnp.float32).reshape(b, nq, N_KV_HEADS, m, HEAD_DIM)
    v = kv_g[..., 1, :].astype(jnp.float32).reshape(b, nq, N_KV_HEADS, m, HEAD_DIM)
    s = jnp.einsum('bqgrd,bqgmd->bqgrm', qg, k) * ATTN_SCALE
    k_pos = (idx[..., None] * SEL_BLOCK + jnp.arange(SEL_BLOCK)).reshape(b, nq, N_KV_HEADS, m)
    mask = (k_pos <= q_pos[None, :, None, None])[:, :, :, None, :]
    p = _masked_softmax(s, mask)
    o = jnp.einsum('bqgrm,bqgmd->bqgrd', p, v)
    return o.reshape(b, nq, N_NSA_HEADS, HEAD_DIM)


def _win_prompt(q_rot, kv_win):
    b, t = q_rot.shape[0], q_rot.shape[1]
    n_qb = t // WIN_Q_BLOCK
    span = WINDOW + WIN_Q_BLOCK
    kvp = jnp.pad(kv_win, ((0, 0), (WINDOW, 0), (0, 0), (0, 0), (0, 0)))
    idx = jnp.arange(n_qb)[:, None] * WIN_Q_BLOCK + jnp.arange(span)[None, :]
    kb = kvp[:, idx].astype(jnp.float32)
    k_pos = (idx - WINDOW)[:, None, :]
    q_pos = jnp.arange(t).reshape(n_qb, WIN_Q_BLOCK)[:, :, None]
    qb = q_rot.astype(jnp.float32).reshape(b, n_qb, WIN_Q_BLOCK, N_KV_HEADS, GQA_RATIO, HEAD_DIM)
    s = jnp.einsum('bnqgrd,bnkgd->bnqgrk', qb, kb[:, :, :, 0]) * ATTN_SCALE
    mask = (k_pos <= q_pos) & (k_pos > q_pos - WINDOW) & (k_pos >= 0)
    p = _masked_softmax(s, mask[None, :, :, None, None, :])
    o = jnp.einsum('bnqgrk,bnkgd->bnqgrd', p, kb[:, :, :, 1])
    return o.reshape(b, t, N_NSA_HEADS, HEAD_DIM)


def _win_sample(q_rot, win_buf, kv_new, q_pos, past):
    b, s_len = q_rot.shape[0], q_rot.shape[1]
    wb = win_buf.shape[1]
    kv = jnp.concatenate([win_buf.astype(kv_new.dtype), kv_new], axis=1)
    k_pos = (past - wb + jnp.arange(wb + s_len))[None, :]
    qp = q_pos[:, None]
    qg = q_rot.astype(jnp.float32).reshape(b, s_len, N_KV_HEADS, GQA_RATIO, HEAD_DIM)
    s = jnp.einsum('bqgrd,bkgd->bqgrk', qg, kv[:, :, 0].astype(jnp.float32)) * ATTN_SCALE
    mask = (k_pos <= qp) & (k_pos > qp - WINDOW) & (k_pos >= 0)
    p = _masked_softmax(s, mask[None, :, None, None, :])
    o = jnp.einsum('bqgrk,bkgd->bqgrd', p, kv[:, :, 1].astype(jnp.float32))
    return o.reshape(b, s_len, N_NSA_HEADS, HEAD_DIM), kv[:, s_len:]


def _nsa_merge(gates, o_c, o_s, o_w):
    b, t = o_c.shape[0], o_c.shape[1]
    o = gates[..., 0:1] * o_c + gates[..., 1:2] * o_s + gates[..., 2:3] * o_w
    return o.reshape(b, t, NSA_WIDTH)


def _nsa_prompt(q, kv_cmp, kv_sel, kv_win, w1, b1, w2, b2, pos):
    b, t = q.shape[0], q.shape[1]
    q_rot = _rope(q, pos)
    blocks = _compress_blocks(_segment_proj(kv_cmp, w1), b1, w2, b2)
    n_sb = t // SEL_BLOCK
    kbs = kv_sel.reshape(b, n_sb, SEL_BLOCK, 2, N_KV_HEADS, HEAD_DIM).transpose(0, 4, 1, 2, 3, 5)
    bi = jnp.arange(b)[:, None, None, None]
    gi = jnp.arange(N_KV_HEADS)[None, None, :, None]
    n_qb = t // NSA_Q_BLOCK

    def q_block(args):
        qb, qrb, pb = args
        o_c, p_grp = _cmp_attend(qb, blocks, pb)
        idx = _select_blocks(p_grp, pb, n_sb)
        kv_g = kbs[bi, gi, idx]
        return o_c, _sel_attend(qrb, kv_g, idx, pb)

    def to_blocks(a):
        return a.reshape(b, n_qb, NSA_Q_BLOCK, N_NSA_HEADS, HEAD_DIM).swapaxes(0, 1)

    def from_blocks(a):
        return a.swapaxes(0, 1).reshape(b, t, N_NSA_HEADS, HEAD_DIM)

    o_c, o_s = lax.map(q_block, (to_blocks(q), to_blocks(q_rot), pos.reshape(n_qb, NSA_Q_BLOCK)))
    return from_blocks(o_c), from_blocks(o_s), _win_prompt(q_rot, kv_win)


def _nsa_sample(q, kv_cmp, kv_sel, kv_win, cache_cmp_kv, cache_sel_kv, win_buf, page_table, layer, w1, b1, w2, b2, pos):
    db, s_len = q.shape[0], q.shape[1]
    past = page_table.shape[1] * PAGE_SIZE
    q_rot = _rope(q, pos)
    past_cmp = cache_cmp_kv[page_table, layer].reshape(db, past, 2, N_KV_HEADS, HEAD_DIM)
    new_cmp = jnp.pad(kv_cmp, ((0, 0), (0, (-s_len) % CMP_STRIDE), (0, 0), (0, 0), (0, 0)))
    p_seg = jnp.concatenate([_segment_proj(past_cmp.astype(kv_cmp.dtype), w1), _segment_proj(new_cmp, w1)], axis=1)
    blocks = _compress_blocks(p_seg, b1, w2, b2)
    o_c, p_grp = _cmp_attend(q, blocks, pos)
    n_pb = past // SEL_BLOCK
    n_tail = -(-s_len // SEL_BLOCK)
    idx = _select_blocks(p_grp, pos, n_pb + n_tail)
    bi = jnp.arange(db)[:, None, None, None]
    gi = jnp.arange(N_KV_HEADS)[None, None, :, None]
    blocks_per_page = PAGE_SIZE // SEL_BLOCK
    blk_past = jnp.minimum(idx, n_pb - 1)
    page = page_table[bi, blk_past // blocks_per_page]
    row = (blk_past % blocks_per_page)[..., None] * SEL_BLOCK + jnp.arange(SEL_BLOCK)
    past_g = cache_sel_kv[page[..., None], layer, row, :, gi[..., None], :]
    tail = jnp.pad(kv_sel, ((0, 0), (0, n_tail * SEL_BLOCK - s_len), (0, 0), (0, 0), (0, 0)))
    tail = tail.reshape(db, n_tail, SEL_BLOCK, 2, N_KV_HEADS, HEAD_DIM).transpose(0, 4, 1, 2, 3, 5)
    tail_g = tail[bi, gi, jnp.clip(idx - n_pb, 0, n_tail - 1)]
    kv_g = jnp.where((idx >= n_pb)[..., None, None, None], tail_g, past_g.astype(tail_g.dtype))
    o_s = _sel_attend(q_rot, kv_g, idx, pos)
    o_w, new_win = _win_sample(q_rot, win_buf, kv_win, pos, past)
    return o_c, o_s, o_w, new_win


def _ret_log_decay():
    return jnp.log1p(-jnp.exp2(-5.0 - jnp.arange(N_RET_HEADS, dtype=jnp.float32)))


def _ret_chunk(state, q, k, v, log_g):
    c = q.shape[1]
    i = jnp.arange(c, dtype=jnp.float32)
    diff = i[:, None] - i[None, :]
    decay = jnp.where(diff[None] >= 0, jnp.exp(log_g[:, None, None] * jnp.maximum(diff, 0.0)[None]), 0.0)
    scores = jnp.einsum('bihd,bjhd->bhij', q, k) * decay[None]
    inner = jnp.einsum('bhij,bjhv->bihv', scores, v)
    q_dec = jnp.exp(log_g[None, :] * (i[:, None] + 1.0))
    cross = jnp.einsum('bihd,bhdv->bihv', q * q_dec[None, :, :, None], state)
    k_dec = jnp.exp(log_g[None, :] * (c - 1.0 - i[:, None]))
    new_state = jnp.exp(log_g * c)[None, :, None, None] * state + jnp.einsum('bjhd,bjhv->bhdv', k * k_dec[None, :, :, None], v)
    return new_state, inner + cross


def _ret_prompt(r_q, r_k, r_v, log_g):
    b, t = r_q.shape[0], r_q.shape[1]
    n_ch = t // RET_CHUNK

    def chunks(a):
        return a.astype(jnp.float32).reshape(b, n_ch, RET_CHUNK, a.shape[2], a.shape[3]).swapaxes(0, 1)

    s0 = jnp.zeros((b, N_RET_HEADS, RET_DK, RET_DV), jnp.float32)
    s_fin, o = lax.scan(lambda s, xs: _ret_chunk(s, xs[0], xs[1], xs[2], log_g), s0, (chunks(r_q), chunks(r_k), chunks(r_v)))
    return s_fin, o.swapaxes(0, 1).reshape(b, t, N_RET_HEADS, RET_DV)


def _ret_out(o, r_g):
    b, t = o.shape[0], o.shape[1]
    o = o * lax.rsqrt(jnp.mean(o * o, axis=-1, keepdims=True) + RMS_EPS)
    return o.reshape(b, t, RET_WIDTH) * jax.nn.silu(r_g.astype(jnp.float32))


def _moe(h, w_router, b_router, w_up, b_up, w_down, b_down):
    n_tok, d = h.shape
    logits = jnp.einsum('nd,de->ne', h, w_router).astype(jnp.float32) + b_router.astype(jnp.float32)
    top_val, top_idx = lax.top_k(logits, TOP_K)
    gate = jax.nn.softmax(top_val, axis=-1)
    n_assign = n_tok * TOP_K
    flat_e = top_idx.reshape(-1)
    order = jnp.argsort(flat_e)
    sorted_e = flat_e[order]
    tok = order // TOP_K
    counts = jnp.zeros((N_EXPERTS,), jnp.int32).at[flat_e].add(1)
    padded = (counts + MOE_BLOCK - 1) // MOE_BLOCK * MOE_BLOCK
    pad_end = jnp.cumsum(padded)
    pad_start = pad_end - padded
    start = jnp.cumsum(counts) - counts
    dest = pad_start[sorted_e] + jnp.arange(n_assign, dtype=jnp.int32) - start[sorted_e]
    n_blocks = -(-n_assign // MOE_BLOCK) + N_EXPERTS
    rows = jnp.zeros((n_blocks * MOE_BLOCK, d), h.dtype).at[dest].set(h[tok])
    blk_e = jnp.minimum(jnp.searchsorted(pad_end, jnp.arange(n_blocks, dtype=jnp.int32) * MOE_BLOCK, side='right'), N_EXPERTS - 1)

    def expert_block(args):
        xb, e = args
        u = xb @ w_up[e] + b_up[e]
        glu = jnp.minimum(u[:, :D_FF], SWIGLU_LIMIT)
        lin = jnp.clip(u[:, D_FF:], -SWIGLU_LIMIT, SWIGLU_LIMIT)
        a = glu * jax.nn.sigmoid(SWIGLU_ALPHA * glu) * (lin + 1.0)
        return a @ w_down[e] + b_down[e]

    out = lax.map(expert_block, (rows.reshape(n_blocks, MOE_BLOCK, d), blk_e)).reshape(-1, d)
    contrib = out[dest] * gate.reshape(-1)[order][:, None].astype(h.dtype)
    return jnp.zeros((n_tok, d), h.dtype).at[tok].add(contrib)


def _finish(x, o_nsa, o_ret, w_out, g_ffn, w_router, b_router, w_up, b_up, w_down, b_down):
    b, t, d = x.shape
    mix = jnp.concatenate([o_nsa, o_ret], axis=-1).astype(x.dtype)
    x = x + jnp.einsum('btc,cd->btd', mix, w_out)
    h = _rmsnorm(x, g_ffn).reshape(b * t, d)
    return x + _moe(h, w_router, b_router, w_up, b_up, w_down, b_down).reshape(b, t, d)


NSA_TQ = 128
NSA_TK = 128
LANES = 128
HIGHEST = lax.Precision.HIGHEST


def _dot_nt(a, b, precision=None):
    return lax.dot_general(a, b, (((1,), (1,)), ((), ())), precision=precision, preferred_element_type=jnp.float32)


def _nsa_prompt_kernel(qc_ref, qr_ref, kc_ref, vc_ref, ks_ref, vs_ref, kw_ref, vw_ref, gt_ref, o_ref,
                       m_scr, l_scr, acc_scr, *, seq):
    tq, tk = NSA_TQ, NSA_TK
    rows = GQA_RATIO * tq
    qt = pl.program_id(1)
    q0 = qt * tq
    ncp = kc_ref.shape[1]
    n_sb = seq // SEL_BLOCK
    ratio = SEL_BLOCK // CMP_STRIDE
    n_sel = min(N_SEL, n_sb)
    lane = lax.broadcasted_iota(jnp.int32, (tq, LANES), 1)
    gates = jax.nn.sigmoid(gt_ref[0])

    qpos_k = q0 + lax.broadcasted_iota(jnp.int32, (tq, tk), 0)
    kiota = lax.broadcasted_iota(jnp.int32, (tq, tk), 1)

    def attend(q, k_ref, v_ref, lo, hi, mask_fn):
        m_scr[...] = jnp.full((rows, 1), NEG_INF, jnp.float32)
        l_scr[...] = jnp.zeros((rows, 1), jnp.float32)
        acc_scr[...] = jnp.zeros((rows, LANES), jnp.float32)

        def body(kt, carry):
            k0 = pl.multiple_of(kt * tk, tk)
            k = k_ref[0, pl.ds(k0, tk), :]
            v = v_ref[0, pl.ds(k0, tk), :]
            s = _dot_nt(q, k).reshape(GQA_RATIO, tq, tk)
            mask = mask_fn(k0)[None]
            s = jnp.where(mask, s, NEG_INF)
            m_old = m_scr[...].reshape(GQA_RATIO, tq, 1)
            m_new = jnp.maximum(m_old, jnp.max(s, axis=-1, keepdims=True))
            alpha = jnp.exp(m_old - m_new)
            p = jnp.where(mask, jnp.exp(s - m_new), 0.0)
            l_new = alpha * l_scr[...].reshape(GQA_RATIO, tq, 1) + jnp.sum(p, axis=-1, keepdims=True)
            pv = jnp.dot(p.reshape(rows, tk).astype(jnp.bfloat16), v, preferred_element_type=jnp.float32)
            acc_scr[...] = alpha.reshape(rows, 1) * acc_scr[...] + pv
            m_scr[...] = m_new.reshape(rows, 1)
            l_scr[...] = l_new.reshape(rows, 1)
            return carry

        lax.fori_loop(lo, hi, body, 0)
        return acc_scr[...] / jnp.maximum(l_scr[...], 1e-30)

    for g in range(N_KV_HEADS):
        qc = qc_ref[0, g].reshape(rows, LANES)
        s = _dot_nt(qc, kc_ref[0], HIGHEST).reshape(GQA_RATIO, tq, ncp)
        qpos_c = q0 + lax.broadcasted_iota(jnp.int32, (tq, ncp), 0)
        blk_end = lax.broadcasted_iota(jnp.int32, (tq, ncp), 1) * CMP_STRIDE + (CMP_BLOCK - 1)
        cmask = (blk_end <= qpos_c)[None]
        s = jnp.where(cmask, s, NEG_INF)
        mx = jnp.max(s, axis=-1, keepdims=True)
        p = jnp.where(cmask, jnp.exp(s - mx), 0.0)
        p = p / jnp.maximum(jnp.sum(p, axis=-1, keepdims=True), 1e-30)
        o_c = jnp.dot(p.reshape(rows, ncp).astype(jnp.bfloat16), vc_ref[0].astype(jnp.bfloat16),
                      preferred_element_type=jnp.float32)
        p_grp = jnp.sum(p, axis=0)

        jj = lax.broadcasted_iota(jnp.int32, (n_sb, ncp), 0)
        nn = lax.broadcasted_iota(jnp.int32, (n_sb, ncp), 1)
        overlap = ((nn >= ratio * jj - (N_HALF - 1)) & (nn <= ratio * jj + ratio - 1)).astype(jnp.float32)
        score = _dot_nt(overlap, p_grp, HIGHEST)
        jt = lax.broadcasted_iota(jnp.int32, (n_sb, tq), 0)
        qpt = q0 + lax.broadcasted_iota(jnp.int32, (n_sb, tq), 1)
        cur = qpt // SEL_BLOCK
        forced = (jt == 0) | (jt == cur) | (jt == cur - 1)
        valid = jt * SEL_BLOCK <= qpt
        score = jnp.where(forced, FORCED_SCORE, score)
        score = jnp.where(valid, score, INVALID_SCORE)
        jf = jt.astype(jnp.float32)
        sel_t = jnp.zeros((n_sb, tq), jnp.float32)
        for _ in range(n_sel):
            best = jnp.max(score, axis=0, keepdims=True)
            first = jnp.min(jnp.where(score == best, jf, float(n_sb)), axis=0, keepdims=True)
            hit = jf == first
            sel_t = jnp.where(hit, 1.0, sel_t)
            score = jnp.where(hit, -3e38, score)
        if n_sb < LANES:
            sel_t = jnp.concatenate([sel_t, jnp.zeros((LANES - n_sb, tq), jnp.float32)], axis=0)
        sel = sel_t.T.astype(jnp.bfloat16)

        def sel_mask(k0):
            blk_of_key = (k0 + lax.broadcasted_iota(jnp.int32, (LANES, tk), 1)) // SEL_BLOCK
            expand = (blk_of_key == lax.broadcasted_iota(jnp.int32, (LANES, tk), 0)).astype(jnp.bfloat16)
            chosen = jnp.dot(sel, expand, preferred_element_type=jnp.float32)
            return (chosen > 0.5) & (k0 + kiota <= qpos_k)

        qr = qr_ref[0, g].reshape(rows, LANES)
        o_s = attend(qr, ks_ref, vs_ref, 0, qt + 1, sel_mask)

        def win_mask(k0):
            kpos = k0 + kiota
            return (kpos <= qpos_k) & (kpos > qpos_k - WINDOW)

        o_w = attend(qr, kw_ref, vw_ref, jnp.maximum(qt - WINDOW // tk, 0), qt + 1, win_mask)

        for r in range(GQA_RATIO):
            col = (g * GQA_RATIO + r) * 3
            sl = slice(r * tq, (r + 1) * tq)
            comb = (gates[:, col:col + 1] * o_c[sl] + gates[:, col + 1:col + 2] * o_s[sl]
                    + gates[:, col + 2:col + 3] * o_w[sl])
            if g == 0:
                o_ref[0, r] = comb
            else:
                o_ref[0, r] = jnp.where(lane < HEAD_DIM, o_ref[0, r], comb)


def _nsa_prompt_pallas(q, q_rot, blocks, kv_sel, kv_win, gt):
    b, t = q.shape[0], q.shape[1]
    assert t % NSA_TQ == 0 and NSA_TQ == NSA_TK and WINDOW % NSA_TK == 0 and t // SEL_BLOCK <= LANES
    ncp = t // CMP_STRIDE
    n_qt = t // NSA_TQ

    def group_pad(a, dtype):
        a = (a * ATTN_SCALE).reshape(b, t, N_KV_HEADS, GQA_RATIO, HEAD_DIM).transpose(0, 2, 3, 1, 4)
        eye = jnp.eye(N_KV_HEADS, dtype=a.dtype)[None, :, None, None, :, None]
        return (a[:, :, :, :, None, :] * eye).reshape(b, N_KV_HEADS, GQA_RATIO, t, LANES).astype(dtype)

    qc = group_pad(q, jnp.float32)
    qr = group_pad(q_rot, jnp.bfloat16)
    blk = jnp.pad(blocks, ((0, 0), (0, ncp - blocks.shape[1]), (0, 0), (0, 0), (0, 0))).reshape(b, ncp, 2 * LANES)
    ks = kv_sel.astype(jnp.bfloat16).reshape(b, t, 2 * LANES)
    kw = kv_win.astype(jnp.bfloat16).reshape(b, t, 2 * LANES)
    gtp = jnp.pad(gt.astype(jnp.float32), ((0, 0), (0, 0), (0, LANES - gt.shape[-1])))

    q_spec = pl.BlockSpec((1, N_KV_HEADS, GQA_RATIO, NSA_TQ, LANES), lambda i, j: (i, 0, 0, j, 0))

    def kv_spec(c, n):
        return pl.BlockSpec((1, n, LANES), lambda i, j: (i, 0, c))

    rows = GQA_RATIO * NSA_TQ
    out = pl.pallas_call(
        functools.partial(_nsa_prompt_kernel, seq=t),
        grid=(b, n_qt),
        in_specs=[q_spec, q_spec, kv_spec(0, ncp), kv_spec(1, ncp), kv_spec(0, t), kv_spec(1, t),
                  kv_spec(0, t), kv_spec(1, t), pl.BlockSpec((1, NSA_TQ, LANES), lambda i, j: (i, j, 0))],
        out_specs=pl.BlockSpec((1, GQA_RATIO, NSA_TQ, LANES), lambda i, j: (i, 0, j, 0)),
        out_shape=jax.ShapeDtypeStruct((b, GQA_RATIO, t, LANES), jnp.float32),
        scratch_shapes=[pltpu.VMEM((rows, 1), jnp.float32), pltpu.VMEM((rows, 1), jnp.float32),
                        pltpu.VMEM((rows, LANES), jnp.float32)],
        compiler_params=pltpu.CompilerParams(dimension_semantics=("arbitrary", "arbitrary")),
        name="nsa_prompt",
    )(qc, qr, blk, blk, ks, ks, kw, kw, gtp)
    out = out.reshape(b, GQA_RATIO, t, N_KV_HEADS, HEAD_DIM).transpose(0, 2, 3, 1, 4)
    return out.reshape(b, t, NSA_WIDTH)


KV_ROW = 2 * N_KV_HEADS * HEAD_DIM
SEG_W = CMP_STRIDE * KV_ROW
SAMPLE_PG = 32


def _topk_rows(score, n_valid, k):
    rows, width = score.shape
    jf = lax.broadcasted_iota(jnp.int32, (rows, width), 1).astype(jnp.float32)
    score = jnp.where(jf < n_valid, score, -3e38)
    sel = jnp.zeros((rows, width), jnp.float32)
    for _ in range(k):
        best = jnp.max(score, axis=1, keepdims=True)
        first = jnp.min(jnp.where(score == best, jf, float(width)), axis=1, keepdims=True)
        hit = jf == first
        sel = jnp.where(hit, 1.0, sel)
        score = jnp.where(hit, -3e38, score)
    return sel


def _sample_cmp_kernel(pt_ref, *refs, pg, n_pages, s_len):
    page_refs = refs[:pg]
    (new_ref, w1_ref, b1_ref, w2_ref, b2_ref, qc_ref, oc_ref, sel_ref, t_scr, pseg_scr) = refs[pg:]
    j = pl.program_id(1)
    n_steps = n_pages // pg
    seg_per_page = PAGE_SIZE // CMP_STRIDE
    m_rows = pg * seg_per_page
    past = n_pages * PAGE_SIZE
    n_seg = past // CMP_STRIDE
    rows = GQA_RATIO * s_len

    for i in range(pg):
        for c in range(2):
            t_scr[c, i * PAGE_SIZE:(i + 1) * PAGE_SIZE, :] = page_refs[i][0, c * LANES:(c + 1) * LANES, :].T
    for c in range(2):
        xc = jnp.concatenate([t_scr[c, pl.ds(r, m_rows, stride=CMP_STRIDE), :]
                              for r in range(CMP_STRIDE)], axis=1)
        xn = jnp.concatenate([new_ref[0, :, r * KV_ROW + c * LANES:r * KV_ROW + (c + 1) * LANES]
                              for r in range(CMP_STRIDE)], axis=1)
        xc = jnp.concatenate([xc, xn], axis=0).astype(jnp.bfloat16)
        pseg_scr[c, pl.ds(pl.multiple_of(j * m_rows, m_rows), m_rows + 8), :] = jnp.dot(
            xc, w1_ref[c], preferred_element_type=jnp.float32)

    @pl.when(j == n_steps - 1)
    def _():
        kv = []
        for c in range(2):
            acc = jnp.zeros((n_seg, LANES), jnp.float32) + b2_ref[c]
            for g in range(N_KV_HEADS):
                lo = g * N_HALF * CMP_HIDDEN
                hid = (pseg_scr[c, 0:n_seg, lo:lo + CMP_HIDDEN]
                       + pseg_scr[c, 1:n_seg + 1, lo + CMP_HIDDEN:lo + 2 * CMP_HIDDEN] + b1_ref[c])
                hid = jax.nn.gelu(hid)
                acc = acc + jnp.dot(hid.astype(jnp.bfloat16), w2_ref[c, g], preferred_element_type=jnp.float32)
            kv.append(acc)
        k_c, v_c = kv
        n_sb = past // SEL_BLOCK + -(-s_len // SEL_BLOCK)
        width = sel_ref.shape[-1]
        ratio = SEL_BLOCK // CMP_STRIDE
        tok = lax.broadcasted_iota(jnp.int32, (rows, n_seg), 0) % s_len
        blk_end = lax.broadcasted_iota(jnp.int32, (rows, n_seg), 1) * CMP_STRIDE + (CMP_BLOCK - 1)
        cmask = blk_end <= past + tok
        same_tok = (lax.broadcasted_iota(jnp.int32, (rows, rows), 0) % s_len
                    == lax.broadcasted_iota(jnp.int32, (rows, rows), 1) % s_len).astype(jnp.float32)
        nn = lax.broadcasted_iota(jnp.int32, (n_seg, width), 0)
        jj = lax.broadcasted_iota(jnp.int32, (n_seg, width), 1)
        overlap = ((nn >= ratio * jj - (N_HALF - 1)) & (nn <= ratio * jj + ratio - 1)).astype(jnp.float32)
        jb = lax.broadcasted_iota(jnp.int32, (rows, width), 1)
        qp = past + lax.broadcasted_iota(jnp.int32, (rows, width), 0) % s_len
        cur = qp // SEL_BLOCK
        forced = (jb == 0) | (jb == cur) | (jb == cur - 1)
        valid = jb * SEL_BLOCK <= qp
        for g in range(N_KV_HEADS):
            s = _dot_nt(qc_ref[0, g], k_c, HIGHEST)
            s = jnp.where(cmask, s, NEG_INF)
            mx = jnp.max(s, axis=-1, keepdims=True)
            p = jnp.where(cmask, jnp.exp(s - mx), 0.0)
            p = p / jnp.maximum(jnp.sum(p, axis=-1, keepdims=True), 1e-30)
            oc_ref[0, g] = jnp.dot(p.astype(jnp.bfloat16), v_c.astype(jnp.bfloat16),
                                   preferred_element_type=jnp.float32)
            p_grp = jnp.dot(same_tok, p, precision=HIGHEST, preferred_element_type=jnp.float32)
            score = jnp.dot(p_grp, overlap, precision=HIGHEST, preferred_element_type=jnp.float32)
            score = jnp.where(forced, FORCED_SCORE, score)
            score = jnp.where(valid, score, INVALID_SCORE)
            sel_ref[0, g] = _topk_rows(score, n_sb, min(N_SEL, n_sb))


def _sample_attn_kernel(pt_ref, *refs, pg, n_pages, s_len):
    page_refs = refs[:pg]
    (qr_ref, sel_ref, tail_ref, win_ref, wnew_ref, oc_ref, gt_ref, o_ref,
     k_scr, v_scr, m_scr, l_scr, acc_scr) = refs[pg:]
    j = pl.program_id(1)
    n_steps = n_pages // pg
    rows = GQA_RATIO * s_len
    keys = pg * PAGE_SIZE
    past = n_pages * PAGE_SIZE
    width = sel_ref.shape[-1]

    @pl.when(j == 0)
    def _():
        m_scr[...] = jnp.full(m_scr.shape, NEG_INF, jnp.float32)
        l_scr[...] = jnp.zeros(l_scr.shape, jnp.float32)
        acc_scr[...] = jnp.zeros(acc_scr.shape, jnp.float32)

    for i in range(pg):
        k_scr[:, i * PAGE_SIZE:(i + 1) * PAGE_SIZE] = page_refs[i][0, 0:LANES, :].astype(jnp.bfloat16)
        v_scr[:, i * PAGE_SIZE:(i + 1) * PAGE_SIZE] = page_refs[i][0, LANES:2 * LANES, :].astype(jnp.bfloat16)
    blk_of_key = (j * keys + lax.broadcasted_iota(jnp.int32, (width, keys), 1)) // SEL_BLOCK
    expand = (blk_of_key == lax.broadcasted_iota(jnp.int32, (width, keys), 0)).astype(jnp.bfloat16)
    for g in range(N_KV_HEADS):
        q = qr_ref[0, g]
        s = jnp.dot(q, k_scr[...], preferred_element_type=jnp.float32)
        mask = jnp.dot(sel_ref[0, g].astype(jnp.bfloat16), expand, preferred_element_type=jnp.float32) > 0.5
        s = jnp.where(mask, s, NEG_INF)
        m_old = m_scr[g]
        m_new = jnp.maximum(m_old, jnp.max(s, axis=-1, keepdims=True))
        alpha = jnp.exp(m_old - m_new)
        p = jnp.where(mask, jnp.exp(s - m_new), 0.0)
        l_scr[g] = alpha * l_scr[g] + jnp.sum(p, axis=-1, keepdims=True)
        acc_scr[g] = alpha * acc_scr[g] + _dot_nt(p.astype(jnp.bfloat16), v_scr[...])
        m_scr[g] = m_new

    @pl.when(j == n_steps - 1)
    def _():
        tok8 = lax.broadcasted_iota(jnp.int32, (rows, 8), 0) % s_len
        new_ok = lax.broadcasted_iota(jnp.int32, (rows, 8), 1) <= tok8
        wb = win_ref.shape[2]
        tokw = lax.broadcasted_iota(jnp.int32, (rows, wb), 0) % s_len
        kpos = past - wb + lax.broadcasted_iota(jnp.int32, (rows, wb), 1)
        win_ok = (kpos > past + tokw - WINDOW) & (kpos >= 0)
        gates = jax.nn.sigmoid(gt_ref[0])
        for g in range(N_KV_HEADS):
            q = qr_ref[0, g]
            tail_sel = sel_ref[0, g][:, past // SEL_BLOCK:past // SEL_BLOCK + 1] > 0.5
            t_mask = new_ok & tail_sel
            s_t = jnp.where(t_mask, _dot_nt(q, tail_ref[0, :, 0:LANES].astype(jnp.bfloat16)), NEG_INF)
            m_old = m_scr[g]
            m_new = jnp.maximum(m_old, jnp.max(s_t, axis=-1, keepdims=True))
            alpha = jnp.exp(m_old - m_new)
            p_t = jnp.where(t_mask, jnp.exp(s_t - m_new), 0.0)
            l_s = alpha * l_scr[g] + jnp.sum(p_t, axis=-1, keepdims=True)
            o_s = (alpha * acc_scr[g] + jnp.dot(p_t.astype(jnp.bfloat16),
                                                tail_ref[0, :, LANES:2 * LANES].astype(jnp.bfloat16),
                                                preferred_element_type=jnp.float32)) / jnp.maximum(l_s, 1e-30)
            s_w = jnp.where(win_ok, jnp.dot(q, win_ref[0, 0:LANES, :].astype(jnp.bfloat16),
                                            preferred_element_type=jnp.float32), NEG_INF)
            s_n = jnp.where(new_ok, _dot_nt(q, wnew_ref[0, :, 0:LANES].astype(jnp.bfloat16)), NEG_INF)
            mw = jnp.maximum(jnp.max(s_w, axis=-1, keepdims=True), jnp.max(s_n, axis=-1, keepdims=True))
            p_w = jnp.where(win_ok, jnp.exp(s_w - mw), 0.0)
            p_n = jnp.where(new_ok, jnp.exp(s_n - mw), 0.0)
            l_w = jnp.sum(p_w, axis=-1, keepdims=True) + jnp.sum(p_n, axis=-1, keepdims=True)
            o_w = (_dot_nt(p_w.astype(jnp.bfloat16), win_ref[0, LANES:2 * LANES, :].astype(jnp.bfloat16))
                   + jnp.dot(p_n.astype(jnp.bfloat16), wnew_ref[0, :, LANES:2 * LANES].astype(jnp.bfloat16),
                             preferred_element_type=jnp.float32)) / jnp.maximum(l_w, 1e-30)
            gl = gates[g]
            o_ref[0, g] = gl[:, 0:1] * oc_ref[0, g] + gl[:, 1:2] * o_s + gl[:, 2:3] * o_w


def _nsa_sample_pallas(q, q_rot, kv_cmp, kv_sel, kv_win, gt, cache_cmp, cache_sel, win_buf, page_table,
                       w1, b1, w2, b2):
    pg = SAMPLE_PG
    db, s_len = q.shape[0], q.shape[1]
    n_phys = cache_cmp.shape[0]
    n_pages = page_table.shape[1]
    assert n_pages % pg == 0 and s_len <= 8 and s_len <= CMP_STRIDE
    past = n_pages * PAGE_SIZE
    n_seg = past // CMP_STRIDE
    n_sb = past // SEL_BLOCK + 1
    width = -(-n_sb // LANES) * LANES
    rows = GQA_RATIO * s_len
    seg_per_page = PAGE_SIZE // CMP_STRIDE

    def group_rows(a, dtype):
        a = (a * ATTN_SCALE).reshape(db, s_len, N_KV_HEADS, GQA_RATIO, HEAD_DIM).transpose(0, 2, 3, 1, 4)
        eye = jnp.eye(N_KV_HEADS, dtype=a.dtype)[None, :, None, None, :, None]
        return (a[:, :, :, :, None, :] * eye).reshape(db, N_KV_HEADS, rows, LANES).astype(dtype)

    def pad8(a):
        return jnp.pad(a.reshape(db, s_len, KV_ROW), ((0, 0), (0, 8 - s_len), (0, 0)))

    qc = group_rows(q, jnp.float32)
    qr = group_rows(q_rot, jnp.bfloat16)
    new_seg = jnp.pad(kv_cmp.reshape(db, 1, s_len * KV_ROW), ((0, 0), (0, 7), (0, SEG_W - s_len * KV_ROW)))
    eye_g = jnp.eye(N_KV_HEADS, dtype=w1.dtype)
    w1t = w1.transpose(0, 2, 3, 1, 4)
    w1_bd = (w1t[:, :, None, :, None, :, :] * eye_g[None, None, :, None, :, None, None]).reshape(
        2, CMP_STRIDE * N_KV_HEADS * HEAD_DIM, N_KV_HEADS * N_HALF * CMP_HIDDEN).astype(jnp.bfloat16)
    w2_g = (w2[:, None, :, None, :] * eye_g[None, :, None, :, None]).reshape(
        2, N_KV_HEADS, CMP_HIDDEN, LANES).astype(jnp.bfloat16)
    b2_t = jnp.tile(b2, (1, N_KV_HEADS)).reshape(2, 1, LANES)
    b1_r = b1.reshape(2, 1, CMP_HIDDEN)

    def page_specs(block):
        return [pl.BlockSpec(block, functools.partial(lambda b, j, pt, i: (pt[b, j * pg + i], 0, 0), i=i))
                for i in range(pg)]

    def per_seq(shape):
        nd = len(shape)
        return pl.BlockSpec((1,) + shape, lambda b, j, pt: (b,) + (0,) * nd)

    def const(shape):
        nd = len(shape)
        return pl.BlockSpec(shape, lambda b, j, pt: (0,) * nd)

    def feature_major(a):
        return a.transpose(0, 2, 3, 4, 1).reshape(a.shape[0], KV_ROW, a.shape[1])

    page_block = (1, KV_ROW, PAGE_SIZE)
    m_rows = pg * seg_per_page
    o_c, sel = pl.pallas_call(
        functools.partial(_sample_cmp_kernel, pg=pg, n_pages=n_pages, s_len=s_len),
        grid_spec=pltpu.PrefetchScalarGridSpec(
            num_scalar_prefetch=1, grid=(db, n_pages // pg),
            in_specs=page_specs(page_block) + [
                per_seq((8, SEG_W)), const(w1_bd.shape), const(b1_r.shape), const(w2_g.shape), const(b2_t.shape),
                per_seq((N_KV_HEADS, rows, LANES))],
            out_specs=[per_seq((N_KV_HEADS, rows, LANES)), per_seq((N_KV_HEADS, rows, width))],
            scratch_shapes=[pltpu.VMEM((2, pg * PAGE_SIZE, LANES), jnp.float32),
                            pltpu.VMEM((2, n_seg + 8, N_KV_HEADS * N_HALF * CMP_HIDDEN), jnp.float32)]),
        out_shape=[jax.ShapeDtypeStruct((db, N_KV_HEADS, rows, LANES), jnp.float32),
                   jax.ShapeDtypeStruct((db, N_KV_HEADS, rows, width), jnp.float32)],
        compiler_params=pltpu.CompilerParams(dimension_semantics=("arbitrary", "arbitrary"),
                                             vmem_limit_bytes=56 * 1024 * 1024),
        name="sample_cmp_select",
    )(page_table, *([feature_major(cache_cmp)] * pg), new_seg, w1_bd, b1_r, w2_g, b2_t, qc)

    gl = jnp.pad(gt.astype(jnp.float32).reshape(db, s_len, N_KV_HEADS, GQA_RATIO, 3).transpose(0, 2, 3, 1, 4)
                 .reshape(db, N_KV_HEADS, rows, 3), ((0, 0), (0, 0), (0, 0), (0, LANES - 3)))
    wb = win_buf.shape[1]
    out = pl.pallas_call(
        functools.partial(_sample_attn_kernel, pg=pg, n_pages=n_pages, s_len=s_len),
        grid_spec=pltpu.PrefetchScalarGridSpec(
            num_scalar_prefetch=1, grid=(db, n_pages // pg),
            in_specs=page_specs(page_block) + [
                per_seq((N_KV_HEADS, rows, LANES)), per_seq((N_KV_HEADS, rows, width)), per_seq((8, KV_ROW)),
                per_seq((KV_ROW, wb)), per_seq((8, KV_ROW)), per_seq((N_KV_HEADS, rows, LANES)),
                per_seq((N_KV_HEADS, rows, LANES))],
            out_specs=per_seq((N_KV_HEADS, rows, LANES)),
            scratch_shapes=[pltpu.VMEM((LANES, pg * PAGE_SIZE), jnp.bfloat16),
                            pltpu.VMEM((LANES, pg * PAGE_SIZE), jnp.bfloat16),
                            pltpu.VMEM((N_KV_HEADS, rows, 1), jnp.float32),
                            pltpu.VMEM((N_KV_HEADS, rows, 1), jnp.float32),
                            pltpu.VMEM((N_KV_HEADS, rows, LANES), jnp.float32)]),
        out_shape=jax.ShapeDtypeStruct((db, N_KV_HEADS, rows, LANES), jnp.float32),
        compiler_params=pltpu.CompilerParams(dimension_semantics=("arbitrary", "arbitrary")),
        name="sample_sel_win_attn",
    )(page_table, *([feature_major(cache_sel)] * pg), qr, sel, pad8(kv_sel),
      feature_major(win_buf), pad8(kv_win), o_c, gl)
    out = out.reshape(db, N_KV_HEADS, GQA_RATIO, s_len, N_KV_HEADS, HEAD_DIM)
    out = jnp.stack([out[:, g, :, :, g, :] for g in range(N_KV_HEADS)], axis=1)
    return out.transpose(0, 3, 1, 2, 4).reshape(db, s_len, NSA_WIDTH)


ROUTER_TT = 256
MOE_BM = 256


def _router_kernel(x_ref, mix_ref, wout_ref, g_ref, wrt_ref, br_ref,
                   x1_ref, h_ref, eidx_ref, gate_ref, rank_ref, cnt_ref, wout_bf, run_scr):
    tt = x_ref.shape[0]
    n_e = wrt_ref.shape[0]

    @pl.when(pl.program_id(0) == 0)
    def _():
        wout_bf[...] = wout_ref[...].astype(jnp.bfloat16)
        run_scr[...] = jnp.zeros_like(run_scr)

    x1 = x_ref[...] + jnp.dot(mix_ref[...].astype(jnp.bfloat16), wout_bf[...], preferred_element_type=jnp.float32)
    x1_ref[...] = x1
    hn = x1 * lax.rsqrt(jnp.mean(x1 * x1, axis=-1, keepdims=True) + RMS_EPS) * g_ref[...]
    h_ref[...] = hn

    score = _dot_nt(wrt_ref[...], hn, HIGHEST) + br_ref[...]
    ef = lax.broadcasted_iota(jnp.int32, (n_e, tt), 0).astype(jnp.float32)
    vals, hits = [], []
    for k in range(TOP_K):
        best = jnp.max(score, axis=0, keepdims=True)
        first = jnp.min(jnp.where(score == best, ef, float(n_e)), axis=0, keepdims=True)
        hit = ef == first
        vals.append(best)
        hits.append(hit)
        eidx_ref[k:k + 1, :] = first.astype(jnp.int32)
        score = jnp.where(hit, -3e38, score)
    exps = [jnp.exp(v - vals[0]) for v in vals]
    denom = sum(exps[1:], exps[0])
    for k in range(TOP_K):
        gate_ref[k:k + 1, :] = exps[k] / denom

    chosen = functools.reduce(jnp.logical_or, hits)
    before = (lax.broadcasted_iota(jnp.int32, (tt, tt), 0) < lax.broadcasted_iota(jnp.int32, (tt, tt), 1))
    earlier = jnp.dot(chosen.astype(jnp.bfloat16), before.astype(jnp.bfloat16), preferred_element_type=jnp.float32)
    pos = earlier + run_scr[...]
    for k in range(TOP_K):
        rank_ref[k:k + 1, :] = jnp.sum(jnp.where(hits[k], pos, 0.0), axis=0, keepdims=True).astype(jnp.int32)
    run_scr[...] = run_scr[...] + jnp.sum(chosen.astype(jnp.float32), axis=1, keepdims=True)
    cnt_ref[...] = jnp.broadcast_to(run_scr[...], cnt_ref.shape).astype(jnp.int32)


def _router_pallas(x, mix, w_out, g_ffn, w_router, b_router):
    n, d = x.shape
    c = mix.shape[1]
    n_e = w_router.shape[1]
    tt = ROUTER_TT
    assert n % tt == 0

    def row(w):
        return pl.BlockSpec((tt, w), lambda i: (i, 0))

    def full(a, b):
        return pl.BlockSpec((a, b), lambda i: (0, 0))

    k4 = pl.BlockSpec((TOP_K, tt), lambda i: (0, i))
    return pl.pallas_call(
        _router_kernel,
        grid=(n // tt,),
        in_specs=[row(d), row(c), full(c, d), full(1, d), full(n_e, d), full(n_e, 1)],
        out_specs=[row(d), row(d), k4, k4, k4, full(n_e, LANES)],
        out_shape=[jax.ShapeDtypeStruct((n, d), jnp.float32), jax.ShapeDtypeStruct((n, d), jnp.float32),
                   jax.ShapeDtypeStruct((TOP_K, n), jnp.int32), jax.ShapeDtypeStruct((TOP_K, n), jnp.float32),
                   jax.ShapeDtypeStruct((TOP_K, n), jnp.int32), jax.ShapeDtypeStruct((n_e, LANES), jnp.int32)],
        scratch_shapes=[pltpu.VMEM((c, d), jnp.bfloat16), pltpu.VMEM((n_e, 1), jnp.float32)],
        compiler_params=pltpu.CompilerParams(dimension_semantics=("arbitrary",)),
        name="outproj_router",
    )(x, mix, w_out, g_ffn.reshape(1, d), w_router.T, b_router.reshape(n_e, 1))


def _expert_kernel(blk_e_ref, n_used_ref, x_ref, wup_ref, bup_ref, wdn_ref, bdn_ref, o_ref, wup_bf, wdn_bf):
    i = pl.program_id(0)
    d_ff = wdn_ref.shape[1]

    @pl.when(i < n_used_ref[0])
    def _():
        e = blk_e_ref[i]
        prev = blk_e_ref[jnp.maximum(i - 1, 0)]

        @pl.when((i == 0) | (e != prev))
        def _():
            wup_bf[...] = wup_ref[0].astype(jnp.bfloat16)
            wdn_bf[...] = wdn_ref[0].astype(jnp.bfloat16)

        u = jnp.dot(x_ref[...].astype(jnp.bfloat16), wup_bf[...], preferred_element_type=jnp.float32) + bup_ref[0]
        glu = jnp.minimum(u[:, :d_ff], SWIGLU_LIMIT)
        lin = jnp.clip(u[:, d_ff:], -SWIGLU_LIMIT, SWIGLU_LIMIT)
        a = glu * jax.nn.sigmoid(SWIGLU_ALPHA * glu) * (lin + 1.0)
        o_ref[...] = jnp.dot(a.astype(jnp.bfloat16), wdn_bf[...], preferred_element_type=jnp.float32) + bdn_ref[0]

    @pl.when(i >= n_used_ref[0])
    def _():
        o_ref[...] = jnp.zeros_like(o_ref)


def _expert_vmem_bytes(bm, d, f2, d_ff):
    weights = 2 * 4 * (d * f2 + d_ff * d) + 2 * (d * f2 + d_ff * d)
    rows = 2 * bm * d * (4 + 4)
    temps = bm * f2 * 4 * 2 + bm * d_ff * (4 + 2)
    return weights + rows + temps


def _experts_pallas(xs, blk_e, n_used, w_up, b_up, w_down, b_down, bm):
    n_slots, d = xs.shape
    n_e, _, f2 = w_up.shape
    d_ff = w_down.shape[1]
    grid_spec = pltpu.PrefetchScalarGridSpec(
        num_scalar_prefetch=2,
        grid=(n_slots // bm,),
        in_specs=[pl.BlockSpec((bm, d), lambda i, be, nu: (i, 0)),
                  pl.BlockSpec((1, d, f2), lambda i, be, nu: (be[i], 0, 0)),
                  pl.BlockSpec((1, 1, f2), lambda i, be, nu: (be[i], 0, 0)),
                  pl.BlockSpec((1, d_ff, d), lambda i, be, nu: (be[i], 0, 0)),
                  pl.BlockSpec((1, 1, d), lambda i, be, nu: (be[i], 0, 0))],
        out_specs=pl.BlockSpec((bm, d), lambda i, be, nu: (i, 0)),
        scratch_shapes=[pltpu.VMEM((d, f2), jnp.bfloat16), pltpu.VMEM((d_ff, d), jnp.bfloat16)],
    )
    vmem_limit = _expert_vmem_bytes(bm, d, f2, d_ff) * 5 // 4
    return pl.pallas_call(
        _expert_kernel,
        grid_spec=grid_spec,
        out_shape=jax.ShapeDtypeStruct((n_slots, d), jnp.float32),
        compiler_params=pltpu.CompilerParams(dimension_semantics=("arbitrary",), vmem_limit_bytes=vmem_limit),
        name="expert_mlp",
    )(blk_e, n_used, xs, w_up, b_up.reshape(n_e, 1, f2), w_down, b_down.reshape(n_e, 1, d))


def _combine_norm_kernel(x1_ref, og_ref, gate_ref, g_ref, o_ref):
    y = x1_ref[...]
    gates = gate_ref[...]
    for k in range(TOP_K):
        y = y + gates[:, k:k + 1] * og_ref[k]
    o_ref[...] = y * lax.rsqrt(jnp.mean(y * y, axis=-1, keepdims=True) + RMS_EPS) * g_ref[...]


def _combine_norm_pallas(x1, og, gate_t, g_final):
    n, d = x1.shape
    tt = ROUTER_TT
    return pl.pallas_call(
        _combine_norm_kernel,
        grid=(n // tt,),
        in_specs=[pl.BlockSpec((tt, d), lambda i: (i, 0)), pl.BlockSpec((TOP_K, tt, d), lambda i: (0, i, 0)),
                  pl.BlockSpec((tt, TOP_K), lambda i: (i, 0)), pl.BlockSpec((1, d), lambda i: (0, 0))],
        out_specs=pl.BlockSpec((tt, d), lambda i: (i, 0)),
        out_shape=jax.ShapeDtypeStruct((n, d), jnp.float32),
        compiler_params=pltpu.CompilerParams(dimension_semantics=("arbitrary",)),
        name="combine_final_norm",
    )(x1, og, gate_t, g_final.reshape(1, d))


def _finish_pallas(x, mix, w_out, g_ffn, w_router, b_router, w_up, b_up, w_down, b_down, g_final):
    n, d = x.shape
    n_e = w_router.shape[1]
    bm = MOE_BM
    x1, h, eidx, gate, rank, cnt = _router_pallas(x, mix, w_out, g_ffn, w_router, b_router)
    counts = cnt[:, 0]
    padded = (counts + bm - 1) // bm * bm
    pad_end = jnp.cumsum(padded)
    gstart = pad_end - padded
    dest = gstart[eidx] + rank
    nb = -(-(n * TOP_K) // bm) + n_e
    blk_e = jnp.minimum(jnp.searchsorted(pad_end, jnp.arange(nb, dtype=jnp.int32) * bm, side='right'),
                        n_e - 1).astype(jnp.int32)
    n_used = (pad_end[-1] // bm).astype(jnp.int32).reshape(1)
    tok = jnp.broadcast_to(jnp.arange(n, dtype=jnp.int32)[None, :], (TOP_K, n))
    src = jnp.zeros((nb * bm,), jnp.int32).at[dest.reshape(-1)].set(tok.reshape(-1))
    out = _experts_pallas(h[src], blk_e, n_used, w_up, b_up, w_down, b_down, bm)
    return _combine_norm_pallas(x1, out[dest], gate.T, g_final)


def kernel(x_prompt, x_sample, cache_cmp_kv, cache_sel_kv, cache_win_kv, state_ret, page_table, g_attn, w_in, w_cmp1, b_cmp1, w_cmp2, b_cmp2, w_out, g_ffn, w_router, b_router, w_up, b_up, w_down, b_down, g_final):
    seq = x_prompt.shape[1]
    past = page_table.shape[1] * PAGE_SIZE
    pos_p = jnp.arange(seq, dtype=jnp.int32)
    pos_s = past + jnp.arange(x_sample.shape[1], dtype=jnp.int32)
    log_g = _ret_log_decay()
    assert DEPTH == 1
    l = 0
    d = x_prompt.shape[-1]
    q, kv_cmp_p, kv_sel_p, kv_win_p, _, r_q, r_k, r_v, r_g, gt = _mixer_inputs(x_prompt, pos_p, g_attn[l], w_in[l])
    blocks = _compress_blocks(_segment_proj(kv_cmp_p, w_cmp1[l]), b_cmp1[l], w_cmp2[l], b_cmp2[l])
    o_nsa = _nsa_prompt_pallas(q, _rope(q, pos_p), blocks, kv_sel_p, kv_win_p, gt)
    ret_p, o_r = _ret_prompt(r_q, r_k, r_v, log_g)
    mix_p = jnp.concatenate([o_nsa, _ret_out(o_r, r_g)], axis=-1)
    q, kv_cmp_s, kv_sel_s, kv_win_s, _, r_q, r_k, r_v, r_g, gt = _mixer_inputs(x_sample, pos_s, g_attn[l], w_in[l])
    o_nsa_s = _nsa_sample_pallas(q, _rope(q, pos_s), kv_cmp_s, kv_sel_s, kv_win_s, gt, cache_cmp_kv[:, l],
                                 cache_sel_kv[:, l], cache_win_kv[:, l], page_table,
                                 w_cmp1[l], b_cmp1[l], w_cmp2[l], b_cmp2[l])
    win_s = jnp.concatenate([cache_win_kv[:, l], kv_win_s], axis=1)[:, x_sample.shape[1]:]
    ret_s, o_r = _ret_chunk(state_ret[:, l].astype(jnp.float32), r_q.astype(jnp.float32), r_k.astype(jnp.float32), r_v.astype(jnp.float32), log_g)
    mix_s = jnp.concatenate([o_nsa_s, _ret_out(o_r, r_g)], axis=-1)
    n_p = x_prompt.shape[0] * seq
    x_all = jnp.concatenate([x_prompt.reshape(n_p, d), x_sample.reshape(-1, d)], axis=0)
    mix_all = jnp.concatenate([mix_p.reshape(n_p, MIX_WIDTH), mix_s.reshape(-1, MIX_WIDTH)], axis=0)
    y = _finish_pallas(x_all, mix_all, w_out[l], g_ffn[l], w_router[l], b_router[l], w_up[l], b_up[l], w_down[l], b_down[l], g_final)
    y_prompt = y[:n_p].reshape(x_prompt.shape)
    y_sample = y[n_p:].reshape(x_sample.shape)
    win_p = kv_win_p[:, seq - min(WINDOW, seq):]
    return (y_prompt, y_sample, kv_cmp_p[:, None], kv_sel_p[:, None], win_p[:, None], ret_p[:, None],
            kv_cmp_s[:, None], kv_sel_s[:, None], win_s[:, None], ret_s[:, None])
```

```python
import functools
import jax, jax.numpy as jnp
from jax import lax
import numpy as np
from jax.experimental import pallas as pl
from jax.experimental.pallas import tpu as pltpu

D_MODEL = 1024
BATCH = 4
SEQ = 4096
DEPTH = 1
DEC_BATCH = 128
DEC_SEQ = 4
PAST_LEN = 8192
PAGE_SIZE = 128

HEAD_DIM = 64
N_NSA_HEADS = 8
N_KV_HEADS = 2
GQA_RATIO = N_NSA_HEADS // N_KV_HEADS
CMP_BLOCK = 32
CMP_STRIDE = 16
N_HALF = CMP_BLOCK // CMP_STRIDE
CMP_HIDDEN = 256
SEL_BLOCK = 64
N_SEL = 16
WINDOW = 512
NSA_Q_BLOCK = 64
WIN_Q_BLOCK = 128
N_RET_HEADS = 4
RET_DK = 64
RET_DV = 128
RET_CHUNK = 128
N_EXPERTS = 32
TOP_K = 4
D_FF = D_MODEL
SWIGLU_ALPHA = 1.702
SWIGLU_LIMIT = 7.0
MOE_BLOCK = 128
ROPE_THETA = 10000.0
RMS_EPS = 1e-5
NEG_INF = -1e30
FORCED_SCORE = 1e6
INVALID_SCORE = -1e9
NSA_WIDTH = N_NSA_HEADS * HEAD_DIM
KV_WIDTH = N_KV_HEADS * HEAD_DIM
RET_QK_WIDTH = N_RET_HEADS * RET_DK
RET_WIDTH = N_RET_HEADS * RET_DV
IN_SIZES = (NSA_WIDTH, KV_WIDTH, KV_WIDTH, KV_WIDTH, KV_WIDTH, KV_WIDTH, KV_WIDTH, 3 * N_NSA_HEADS, RET_QK_WIDTH, RET_QK_WIDTH, RET_WIDTH, RET_WIDTH)
IN_TOTAL = NSA_WIDTH + 6 * KV_WIDTH + 3 * N_NSA_HEADS + 2 * RET_QK_WIDTH + 2 * RET_WIDTH
MIX_WIDTH = NSA_WIDTH + RET_WIDTH
ATTN_SCALE = HEAD_DIM ** -0.5


def _rmsnorm(x, g):
    xf = x.astype(jnp.float32)
    y = xf * lax.rsqrt(jnp.mean(xf * xf, axis=-1, keepdims=True) + RMS_EPS)
    return (y * g.astype(jnp.float32)).astype(x.dtype)


def _rope(x, pos):
    half = x.shape[-1] // 2
    inv = ROPE_THETA ** (-jnp.arange(half, dtype=jnp.float32) / half)
    ang = pos.astype(jnp.float32)[:, None] * inv[None, :]
    cos = jnp.cos(ang)[None, :, None, :]
    sin = jnp.sin(ang)[None, :, None, :]
    xf = x.astype(jnp.float32)
    x1, x2 = xf[..., :half], xf[..., half:]
    return jnp.concatenate([x1 * cos - x2 * sin, x2 * cos + x1 * sin], axis=-1).astype(x.dtype)


def _masked_softmax(s, mask):
    s = jnp.where(mask, s, NEG_INF)
    m = jnp.max(s, axis=-1, keepdims=True)
    p = jnp.where(mask, jnp.exp(s - m), 0.0)
    return p / jnp.maximum(jnp.sum(p, axis=-1, keepdims=True), 1e-30)


def _mixer_inputs(x, pos, g_attn, w_in):
    b, t = x.shape[0], x.shape[1]
    h = _rmsnorm(x, g_attn)
    proj = jnp.einsum('btd,dc->btc', h, w_in)
    offs = [int(o) for o in np.cumsum(np.array(IN_SIZES))[:-1]]
    q, kc, vc, ks, vs, kw, vw, gt, rq, rk, rv, rg = jnp.split(proj, offs, axis=-1)

    def heads(a, n, d):
        return a.reshape(b, t, n, d)

    def kvh(a):
        return heads(a, N_KV_HEADS, HEAD_DIM)

    kv_cmp = jnp.stack([kvh(kc), kvh(vc)], axis=2)
    kv_sel = jnp.stack([_rope(kvh(ks), pos), kvh(vs)], axis=2)
    kv_win = jnp.stack([_rope(kvh(kw), pos), kvh(vw)], axis=2)
    gates = jax.nn.sigmoid(gt.astype(jnp.float32)).reshape(b, t, N_NSA_HEADS, 3)
    r_q = _rope(heads(rq, N_RET_HEADS, RET_DK), pos)
    r_k = _rope(heads(rk, N_RET_HEADS, RET_DK), pos) * (RET_DK ** -0.5)
    r_v = heads(rv, N_RET_HEADS, RET_DV)
    return heads(q, N_NSA_HEADS, HEAD_DIM), kv_cmp, kv_sel, kv_win, gates, r_q, r_k, r_v, rg, gt


def _segment_proj(kv_rows, w1):
    b, l = kv_rows.shape[0], kv_rows.shape[1]
    seg = kv_rows.reshape(b, l // CMP_STRIDE, CMP_STRIDE, 2, N_KV_HEADS, HEAD_DIM)
    return jnp.einsum('bsrcgd,chrdk->bscghk', seg, w1)


def _compress_blocks(p_seg, b1, w2, b2):
    n_cmp = p_seg.shape[1] - N_HALF + 1
    hid = sum(p_seg[:, h:h + n_cmp, :, :, h, :] for h in range(N_HALF)) + b1[None, None, :, None, :]
    hid = jax.nn.gelu(hid.astype(jnp.float32))
    out = jnp.einsum('bncgk,ckd->bncgd', hid, w2.astype(jnp.float32))
    return out + b2.astype(jnp.float32)[None, None, :, None, :]


def _cmp_attend(q, blocks, q_pos):
    b, nq = q.shape[0], q.shape[1]
    n_cmp = blocks.shape[1]
    qg = q.astype(jnp.float32).reshape(b, nq, N_KV_HEADS, GQA_RATIO, HEAD_DIM)
    s = jnp.einsum('bqgrd,bngd->bqgrn', qg, blocks[:, :, 0]) * ATTN_SCALE
    blk_end = jnp.arange(n_cmp) * CMP_STRIDE + (CMP_BLOCK - 1)
    mask = blk_end[None, :] <= q_pos[:, None]
    p = _masked_softmax(s, mask[None, :, None, None, :])
    o = jnp.einsum('bqgrn,bngd->bqgrd', p, blocks[:, :, 1])
    return o.reshape(b, nq, N_NSA_HEADS, HEAD_DIM), jnp.sum(p, axis=3)


def _select_blocks(p_grp, q_pos, n_sb):
    ratio = SEL_BLOCK // CMP_STRIDE
    ov = N_HALF - 1
    n_cmp = p_grp.shape[-1]
    pp = jnp.pad(p_grp, ((0, 0), (0, 0), (0, 0), (ov, ratio * n_sb - n_cmp)))
    score = sum(pp[..., o:o + ratio * n_sb:ratio] for o in range(ratio + ov))
    blk = jnp.arange(n_sb)[None, :]
    qp = q_pos[:, None]
    cur = qp // SEL_BLOCK
    forced = (blk == 0) | (blk == cur) | (blk == cur - 1)
    valid = blk * SEL_BLOCK <= qp
    score = jnp.where(forced[None, :, None, :], FORCED_SCORE, score)
    score = jnp.where(valid[None, :, None, :], score, INVALID_SCORE)
    _, idx = lax.top_k(score, min(N_SEL, n_sb))
    return idx


def _sel_attend(q_rot, kv_g, idx, q_pos):
    b, nq = q_rot.shape[0], q_rot.shape[1]
    m = idx.shape[-1] * SEL_BLOCK
    qg = q_rot.astype(jnp.float32).reshape(b, nq, N_KV_HEADS, GQA_RATIO, HEAD_DIM)
    k = kv_g[..., 0, :].astype(jnp.float32).reshape(b, nq, N_KV_HEADS, m, HEAD_DIM)
    v = kv_g[..., 1, :].astype(jnp.float32).reshape(b, nq, N_KV_HEADS, m, HEAD_DIM)
    s = jnp.einsum('bqgrd,bqgmd->bqgrm', qg, k) * ATTN_SCALE
    k_pos = (idx[..., None] * SEL_BLOCK + jnp.arange(SEL_BLOCK)).reshape(b, nq, N_KV_HEADS, m)
    mask = (k_pos <= q_pos[None, :, None, None])[:, :, :, None, :]
    p = _masked_softmax(s, mask)
    o = jnp.einsum('bqgrm,bqgmd->bqgrd', p, v)
    return o.reshape(b, nq, N_NSA_HEADS, HEAD_DIM)


def _win_prompt(q_rot, kv_win):
    b, t = q_rot.shape[0], q_rot.shape[1]
    n_qb = t // WIN_Q_BLOCK
    span = WINDOW + WIN_Q_BLOCK
    kvp = jnp.pad(kv_win, ((0, 0), (WINDOW, 0), (0, 0), (0, 0), (0, 0)))
    idx = jnp.arange(n_qb)[:, None] * WIN_Q_BLOCK + jnp.arange(span)[None, :]
    kb = kvp[:, idx].astype(jnp.float32)
    k_pos = (idx - WINDOW)[:, None, :]
    q_pos = jnp.arange(t).reshape(n_qb, WIN_Q_BLOCK)[:, :, None]
    qb = q_rot.astype(jnp.float32).reshape(b, n_qb, WIN_Q_BLOCK, N_KV_HEADS, GQA_RATIO, HEAD_DIM)
    s = jnp.einsum('bnqgrd,bnkgd->bnqgrk', qb, kb[:, :, :, 0]) * ATTN_SCALE
    mask = (k_pos <= q_pos) & (k_pos > q_pos - WINDOW) & (k_pos >= 0)
    p = _masked_softmax(s, mask[None, :, :, None, None, :])
    o = jnp.einsum('bnqgrk,bnkgd->bnqgrd', p, kb[:, :, :, 1])
    return o.reshape(b, t, N_NSA_HEADS, HEAD_DIM)


def _win_sample(q_rot, win_buf, kv_new, q_pos, past):
    b, s_len = q_rot.shape[0], q_rot.shape[1]
    wb = win_buf.shape[1]
    kv = jnp.concatenate([win_buf.astype(kv_new.dtype), kv_new], axis=1)
    k_pos = (past - wb + jnp.arange(wb + s_len))[None, :]
    qp = q_pos[:, None]
    qg = q_rot.astype(jnp.float32).reshape(b, s_len, N_KV_HEADS, GQA_RATIO, HEAD_DIM)
    s = jnp.einsum('bqgrd,bkgd->bqgrk', qg, kv[:, :, 0].astype(jnp.float32)) * ATTN_SCALE
    mask = (k_pos <= qp) & (k_pos > qp - WINDOW) & (k_pos >= 0)
    p = _masked_softmax(s, mask[None, :, None, None, :])
    o = jnp.einsum('bqgrk,bkgd->bqgrd', p, kv[:, :, 1].astype(jnp.float32))
    return o.reshape(b, s_len, N_NSA_HEADS, HEAD_DIM), kv[:, s_len:]


def _nsa_merge(gates, o_c, o_s, o_w):
    b, t = o_c.shape[0], o_c.shape[1]
    o = gates[..., 0:1] * o_c + gates[..., 1:2] * o_s + gates[..., 2:3] * o_w
    return o.reshape(b, t, NSA_WIDTH)


def _nsa_prompt(q, kv_cmp, kv_sel, kv_win, w1, b1, w2, b2, pos):
    b, t = q.shape[0], q.shape[1]
    q_rot = _rope(q, pos)
    blocks = _compress_blocks(_segment_proj(kv_cmp, w1), b1, w2, b2)
    n_sb = t // SEL_BLOCK
    kbs = kv_sel.reshape(b, n_sb, SEL_BLOCK, 2, N_KV_HEADS, HEAD_DIM).transpose(0, 4, 1, 2, 3, 5)
    bi = jnp.arange(b)[:, None, None, None]
    gi = jnp.arange(N_KV_HEADS)[None, None, :, None]
    n_qb = t // NSA_Q_BLOCK

    def q_block(args):
        qb, qrb, pb = args
        o_c, p_grp = _cmp_attend(qb, blocks, pb)
        idx = _select_blocks(p_grp, pb, n_sb)
        kv_g = kbs[bi, gi, idx]
        return o_c, _sel_attend(qrb, kv_g, idx, pb)

    def to_blocks(a):
        return a.reshape(b, n_qb, NSA_Q_BLOCK, N_NSA_HEADS, HEAD_DIM).swapaxes(0, 1)

    def from_blocks(a):
        return a.swapaxes(0, 1).reshape(b, t, N_NSA_HEADS, HEAD_DIM)

    o_c, o_s = lax.map(q_block, (to_blocks(q), to_blocks(q_rot), pos.reshape(n_qb, NSA_Q_BLOCK)))
    return from_blocks(o_c), from_blocks(o_s), _win_prompt(q_rot, kv_win)


def _nsa_sample(q, kv_cmp, kv_sel, kv_win, cache_cmp_kv, cache_sel_kv, win_buf, page_table, layer, w1, b1, w2, b2, pos):
    db, s_len = q.shape[0], q.shape[1]
    past = page_table.shape[1] * PAGE_SIZE
    q_rot = _rope(q, pos)
    past_cmp = cache_cmp_kv[page_table, layer].reshape(db, past, 2, N_KV_HEADS, HEAD_DIM)
    new_cmp = jnp.pad(kv_cmp, ((0, 0), (0, (-s_len) % CMP_STRIDE), (0, 0), (0, 0), (0, 0)))
    p_seg = jnp.concatenate([_segment_proj(past_cmp.astype(kv_cmp.dtype), w1), _segment_proj(new_cmp, w1)], axis=1)
    blocks = _compress_blocks(p_seg, b1, w2, b2)
    o_c, p_grp = _cmp_attend(q, blocks, pos)
    n_pb = past // SEL_BLOCK
    n_tail = -(-s_len // SEL_BLOCK)
    idx = _select_blocks(p_grp, pos, n_pb + n_tail)
    bi = jnp.arange(db)[:, None, None, None]
    gi = jnp.arange(N_KV_HEADS)[None, None, :, None]
    blocks_per_page = PAGE_SIZE // SEL_BLOCK
    blk_past = jnp.minimum(idx, n_pb - 1)
    page = page_table[bi, blk_past // blocks_per_page]
    row = (blk_past % blocks_per_page)[..., None] * SEL_BLOCK + jnp.arange(SEL_BLOCK)
    past_g = cache_sel_kv[page[..., None], layer, row, :, gi[..., None], :]
    tail = jnp.pad(kv_sel, ((0, 0), (0, n_tail * SEL_BLOCK - s_len), (0, 0), (0, 0), (0, 0)))
    tail = tail.reshape(db, n_tail, SEL_BLOCK, 2, N_KV_HEADS, HEAD_DIM).transpose(0, 4, 1, 2, 3, 5)
    tail_g = tail[bi, gi, jnp.clip(idx - n_pb, 0, n_tail - 1)]
    kv_g = jnp.where((idx >= n_pb)[..., None, None, None], tail_g, past_g.astype(tail_g.dtype))
    o_s = _sel_attend(q_rot, kv_g, idx, pos)
    o_w, new_win = _win_sample(q_rot, win_buf, kv_win, pos, past)
    return o_c, o_s, o_w, new_win


def _ret_log_decay():
    return jnp.log1p(-jnp.exp2(-5.0 - jnp.arange(N_RET_HEADS, dtype=jnp.float32)))


def _ret_chunk(state, q, k, v, log_g):
    c = q.shape[1]
    i = jnp.arange(c, dtype=jnp.float32)
    diff = i[:, None] - i[None, :]
    decay = jnp.where(diff[None] >= 0, jnp.exp(log_g[:, None, None] * jnp.maximum(diff, 0.0)[None]), 0.0)
    scores = jnp.einsum('bihd,bjhd->bhij', q, k) * decay[None]
    inner = jnp.einsum('bhij,bjhv->bihv', scores, v)
    q_dec = jnp.exp(log_g[None, :] * (i[:, None] + 1.0))
    cross = jnp.einsum('bihd,bhdv->bihv', q * q_dec[None, :, :, None], state)
    k_dec = jnp.exp(log_g[None, :] * (c - 1.0 - i[:, None]))
    new_state = jnp.exp(log_g * c)[None, :, None, None] * state + jnp.einsum('bjhd,bjhv->bhdv', k * k_dec[None, :, :, None], v)
    return new_state, inner + cross


def _ret_prompt(r_q, r_k, r_v, log_g):
    b, t = r_q.shape[0], r_q.shape[1]
    n_ch = t // RET_CHUNK

    def chunks(a):
        return a.astype(jnp.float32).reshape(b, n_ch, RET_CHUNK, a.shape[2], a.shape[3]).swapaxes(0, 1)

    s0 = jnp.zeros((b, N_RET_HEADS, RET_DK, RET_DV), jnp.float32)
    s_fin, o = lax.scan(lambda s, xs: _ret_chunk(s, xs[0], xs[1], xs[2], log_g), s0, (chunks(r_q), chunks(r_k), chunks(r_v)))
    return s_fin, o.swapaxes(0, 1).reshape(b, t, N_RET_HEADS, RET_DV)


def _ret_out(o, r_g):
    b, t = o.shape[0], o.shape[1]
    o = o * lax.rsqrt(jnp.mean(o * o, axis=-1, keepdims=True) + RMS_EPS)
    return o.reshape(b, t, RET_WIDTH) * jax.nn.silu(r_g.astype(jnp.float32))


def _moe(h, w_router, b_router, w_up, b_up, w_down, b_down):
    n_tok, d = h.shape
    logits = jnp.einsum('nd,de->ne', h, w_router).astype(jnp.float32) + b_router.astype(jnp.float32)
    top_val, top_idx = lax.top_k(logits, TOP_K)
    gate = jax.nn.softmax(top_val, axis=-1)
    n_assign = n_tok * TOP_K
    flat_e = top_idx.reshape(-1)
    order = jnp.argsort(flat_e)
    sorted_e = flat_e[order]
    tok = order // TOP_K
    counts = jnp.zeros((N_EXPERTS,), jnp.int32).at[flat_e].add(1)
    padded = (counts + MOE_BLOCK - 1) // MOE_BLOCK * MOE_BLOCK
    pad_end = jnp.cumsum(padded)
    pad_start = pad_end - padded
    start = jnp.cumsum(counts) - counts
    dest = pad_start[sorted_e] + jnp.arange(n_assign, dtype=jnp.int32) - start[sorted_e]
    n_blocks = -(-n_assign // MOE_BLOCK) + N_EXPERTS
    rows = jnp.zeros((n_blocks * MOE_BLOCK, d), h.dtype).at[dest].set(h[tok])
    blk_e = jnp.minimum(jnp.searchsorted(pad_end, jnp.arange(n_blocks, dtype=jnp.int32) * MOE_BLOCK, side='right'), N_EXPERTS - 1)

    def expert_block(args):
        xb, e = args
        u = xb @ w_up[e] + b_up[e]
        glu = jnp.minimum(u[:, :D_FF], SWIGLU_LIMIT)
        lin = jnp.clip(u[:, D_FF:], -SWIGLU_LIMIT, SWIGLU_LIMIT)
        a = glu * jax.nn.sigmoid(SWIGLU_ALPHA * glu) * (lin + 1.0)
        return a @ w_down[e] + b_down[e]

    out = lax.map(expert_block, (rows.reshape(n_blocks, MOE_BLOCK, d), blk_e)).reshape(-1, d)
    contrib = out[dest] * gate.reshape(-1)[order][:, None].astype(h.dtype)
    return jnp.zeros((n_tok, d), h.dtype).at[tok].add(contrib)


def _finish(x, o_nsa, o_ret, w_out, g_ffn, w_router, b_router, w_up, b_up, w_down, b_down):
    b, t, d = x.shape
    mix = jnp.concatenate([o_nsa, o_ret], axis=-1).astype(x.dtype)
    x = x + jnp.einsum('btc,cd->btd', mix, w_out)
    h = _rmsnorm(x, g_ffn).reshape(b * t, d)
    return x + _moe(h, w_router, b_router, w_up, b_up, w_down, b_down).reshape(b, t, d)


NSA_TQ = 128
NSA_TK = 128
LANES = 128
HIGHEST = lax.Precision.HIGHEST


def _dot_nt(a, b, precision=None):
    return lax.dot_general(a, b, (((1,), (1,)), ((), ())), precision=precision, preferred_element_type=jnp.float32)


def _nsa_prompt_kernel(qc_ref, qr_ref, kc_ref, vc_ref, ks_ref, vs_ref, kw_ref, vw_ref, gt_ref, o_ref,
                       m_scr, l_scr, acc_scr, *, seq):
    tq, tk = NSA_TQ, NSA_TK
    rows = GQA_RATIO * tq
    qt = pl.program_id(1)
    q0 = qt * tq
    ncp = kc_ref.shape[1]
    n_sb = seq // SEL_BLOCK
    ratio = SEL_BLOCK // CMP_STRIDE
    n_sel = min(N_SEL, n_sb)
    lane = lax.broadcasted_iota(jnp.int32, (tq, LANES), 1)
    gates = jax.nn.sigmoid(gt_ref[0])

    qpos_k = q0 + lax.broadcasted_iota(jnp.int32, (tq, tk), 0)
    kiota = lax.broadcasted_iota(jnp.int32, (tq, tk), 1)

    def attend(q, k_ref, v_ref, lo, hi, mask_fn):
        m_scr[...] = jnp.full((rows, 1), NEG_INF, jnp.float32)
        l_scr[...] = jnp.zeros((rows, 1), jnp.float32)
        acc_scr[...] = jnp.zeros((rows, LANES), jnp.float32)

        def body(kt, carry):
            k0 = pl.multiple_of(kt * tk, tk)
            k = k_ref[0, pl.ds(k0, tk), :]
            v = v_ref[0, pl.ds(k0, tk), :]
            s = _dot_nt(q, k).reshape(GQA_RATIO, tq, tk)
            mask = mask_fn(k0)[None]
            s = jnp.where(mask, s, NEG_INF)
            m_old = m_scr[...].reshape(GQA_RATIO, tq, 1)
            m_new = jnp.maximum(m_old, jnp.max(s, axis=-1, keepdims=True))
            alpha = jnp.exp(m_old - m_new)
            p = jnp.where(mask, jnp.exp(s - m_new), 0.0)
            l_new = alpha * l_scr[...].reshape(GQA_RATIO, tq, 1) + jnp.sum(p, axis=-1, keepdims=True)
            pv = jnp.dot(p.reshape(rows, tk).astype(jnp.bfloat16), v, preferred_element_type=jnp.float32)
            acc_scr[...] = alpha.reshape(rows, 1) * acc_scr[...] + pv
            m_scr[...] = m_new.reshape(rows, 1)
            l_scr[...] = l_new.reshape(rows, 1)
            return carry

        lax.fori_loop(lo, hi, body, 0)
        return acc_scr[...] / jnp.maximum(l_scr[...], 1e-30)

    for g in range(N_KV_HEADS):
        qc = qc_ref[0, g].reshape(rows, LANES)
        s = _dot_nt(qc, kc_ref[0], HIGHEST).reshape(GQA_RATIO, tq, ncp)
        qpos_c = q0 + lax.broadcasted_iota(jnp.int32, (tq, ncp), 0)
        blk_end = lax.broadcasted_iota(jnp.int32, (tq, ncp), 1) * CMP_STRIDE + (CMP_BLOCK - 1)
        cmask = (blk_end <= qpos_c)[None]
        s = jnp.where(cmask, s, NEG_INF)
        mx = jnp.max(s, axis=-1, keepdims=True)
        p = jnp.where(cmask, jnp.exp(s - mx), 0.0)
        p = p / jnp.maximum(jnp.sum(p, axis=-1, keepdims=True), 1e-30)
        o_c = jnp.dot(p.reshape(rows, ncp).astype(jnp.bfloat16), vc_ref[0].astype(jnp.bfloat16),
                      preferred_element_type=jnp.float32)
        p_grp = jnp.sum(p, axis=0)

        jj = lax.broadcasted_iota(jnp.int32, (n_sb, ncp), 0)
        nn = lax.broadcasted_iota(jnp.int32, (n_sb, ncp), 1)
        overlap = ((nn >= ratio * jj - (N_HALF - 1)) & (nn <= ratio * jj + ratio - 1)).astype(jnp.float32)
        score = _dot_nt(overlap, p_grp, HIGHEST)
        jt = lax.broadcasted_iota(jnp.int32, (n_sb, tq), 0)
        qpt = q0 + lax.broadcasted_iota(jnp.int32, (n_sb, tq), 1)
        cur = qpt // SEL_BLOCK
        forced = (jt == 0) | (jt == cur) | (jt == cur - 1)
        valid = jt * SEL_BLOCK <= qpt
        score = jnp.where(forced, FORCED_SCORE, score)
        score = jnp.where(valid, score, INVALID_SCORE)
        jf = jt.astype(jnp.float32)
        sel_t = jnp.zeros((n_sb, tq), jnp.float32)
        for _ in range(n_sel):
            best = jnp.max(score, axis=0, keepdims=True)
            first = jnp.min(jnp.where(score == best, jf, float(n_sb)), axis=0, keepdims=True)
            hit = jf == first
            sel_t = jnp.where(hit, 1.0, sel_t)
            score = jnp.where(hit, -3e38, score)
        if n_sb < LANES:
            sel_t = jnp.concatenate([sel_t, jnp.zeros((LANES - n_sb, tq), jnp.float32)], axis=0)
        sel = sel_t.T.astype(jnp.bfloat16)

        def sel_mask(k0):
            blk_of_key = (k0 + lax.broadcasted_iota(jnp.int32, (LANES, tk), 1)) // SEL_BLOCK
            expand = (blk_of_key == lax.broadcasted_iota(jnp.int32, (LANES, tk), 0)).astype(jnp.bfloat16)
            chosen = jnp.dot(sel, expand, preferred_element_type=jnp.float32)
            return (chosen > 0.5) & (k0 + kiota <= qpos_k)

        qr = qr_ref[0, g].reshape(rows, LANES)
        o_s = attend(qr, ks_ref, vs_ref, 0, qt + 1, sel_mask)

        def win_mask(k0):
            kpos = k0 + kiota
            return (kpos <= qpos_k) & (kpos > qpos_k - WINDOW)

        o_w = attend(qr, kw_ref, vw_ref, jnp.maximum(qt - WINDOW // tk, 0), qt + 1, win_mask)

        for r in range(GQA_RATIO):
            col = (g * GQA_RATIO + r) * 3
            sl = slice(r * tq, (r + 1) * tq)
            comb = (gates[:, col:col + 1] * o_c[sl] + gates[:, col + 1:col + 2] * o_s[sl]
                    + gates[:, col + 2:col + 3] * o_w[sl])
            if g == 0:
                o_ref[0, r] = comb
            else:
                o_ref[0, r] = jnp.where(lane < HEAD_DIM, o_ref[0, r], comb)


def _nsa_prompt_pallas(q, q_rot, blocks, kv_sel, kv_win, gt):
    b, t = q.shape[0], q.shape[1]
    assert t % NSA_TQ == 0 and NSA_TQ == NSA_TK and WINDOW % NSA_TK == 0 and t // SEL_BLOCK <= LANES
    ncp = t // CMP_STRIDE
    n_qt = t // NSA_TQ

    def group_pad(a, dtype):
        a = (a * ATTN_SCALE).reshape(b, t, N_KV_HEADS, GQA_RATIO, HEAD_DIM).transpose(0, 2, 3, 1, 4)
        eye = jnp.eye(N_KV_HEADS, dtype=a.dtype)[None, :, None, None, :, None]
        return (a[:, :, :, :, None, :] * eye).reshape(b, N_KV_HEADS, GQA_RATIO, t, LANES).astype(dtype)

    qc = group_pad(q, jnp.float32)
    qr = group_pad(q_rot, jnp.bfloat16)
    blk = jnp.pad(blocks, ((0, 0), (0, ncp - blocks.shape[1]), (0, 0), (0, 0), (0, 0))).reshape(b, ncp, 2 * LANES)
    ks = kv_sel.astype(jnp.bfloat16).reshape(b, t, 2 * LANES)
    kw = kv_win.astype(jnp.bfloat16).reshape(b, t, 2 * LANES)
    gtp = jnp.pad(gt.astype(jnp.float32), ((0, 0), (0, 0), (0, LANES - gt.shape[-1])))

    q_spec = pl.BlockSpec((1, N_KV_HEADS, GQA_RATIO, NSA_TQ, LANES), lambda i, j: (i, 0, 0, j, 0))

    def kv_spec(c, n):
        return pl.BlockSpec((1, n, LANES), lambda i, j: (i, 0, c))

    rows = GQA_RATIO * NSA_TQ
    out = pl.pallas_call(
        functools.partial(_nsa_prompt_kernel, seq=t),
        grid=(b, n_qt),
        in_specs=[q_spec, q_spec, kv_spec(0, ncp), kv_spec(1, ncp), kv_spec(0, t), kv_spec(1, t),
                  kv_spec(0, t), kv_spec(1, t), pl.BlockSpec((1, NSA_TQ, LANES), lambda i, j: (i, j, 0))],
        out_specs=pl.BlockSpec((1, GQA_RATIO, NSA_TQ, LANES), lambda i, j: (i, 0, j, 0)),
        out_shape=jax.ShapeDtypeStruct((b, GQA_RATIO, t, LANES), jnp.float32),
        scratch_shapes=[pltpu.VMEM((rows, 1), jnp.float32), pltpu.VMEM((rows, 1), jnp.float32),
                        pltpu.VMEM((rows, LANES), jnp.float32)],
        compiler_params=pltpu.CompilerParams(dimension_semantics=("arbitrary", "arbitrary")),
        name="nsa_prompt",
    )(qc, qr, blk, blk, ks, ks, kw, kw, gtp)
    out = out.reshape(b, GQA_RATIO, t, N_KV_HEADS, HEAD_DIM).transpose(0, 2, 3, 1, 4)
    return out.reshape(b, t, NSA_WIDTH)


NSA_TKT = 256


def _nsa_prompt_t_kernel(qc_ref, qr_ref, kc_ref, vct_ref, ks_ref, vst_ref, kw_ref, vwt_ref, gt_ref, o_ref,
                         m_scr, l_scr, acc_scr, *, seq):
    tq, tk = NSA_TQ, NSA_TKT
    cols = GQA_RATIO * tq
    qt = pl.program_id(1)
    q0 = qt * tq
    ncp = kc_ref.shape[1]
    n_sb = seq // SEL_BLOCK
    ratio = SEL_BLOCK // CMP_STRIDE
    n_sel = min(N_SEL, n_sb)
    gates = jax.nn.sigmoid(gt_ref[0, 0])

    def heads(a):
        return jnp.concatenate([a] * GQA_RATIO, axis=1)

    kpos_i = lax.broadcasted_iota(jnp.int32, (tk, tq), 0)
    qpos = q0 + lax.broadcasted_iota(jnp.int32, (tk, tq), 1)

    def attend(q_t, k_ref, vt_ref, lo, hi, mask_fn):
        m_scr[...] = jnp.full((1, cols), NEG_INF, jnp.float32)
        l_scr[...] = jnp.zeros((1, cols), jnp.float32)
        acc_scr[...] = jnp.zeros((LANES, cols), jnp.float32)

        def body(kt, carry):
            k0 = pl.multiple_of(kt * tk, tk)
            keep = mask_fn(k0)
            s = jnp.dot(k_ref[0, pl.ds(k0, tk), :], q_t, preferred_element_type=jnp.float32)
            s = s + heads((keep - 1.0) * -NEG_INF)
            m_old = m_scr[...]
            m_new = jnp.maximum(m_old, jnp.max(s, axis=0, keepdims=True))
            alpha = jnp.exp(m_old - m_new)
            p = jnp.exp(s - m_new) * heads(keep)
            l_scr[...] = alpha * l_scr[...] + jnp.sum(p, axis=0, keepdims=True)
            acc_scr[...] = alpha * acc_scr[...] + jnp.dot(vt_ref[0, kt], p.astype(jnp.bfloat16),
                                                          preferred_element_type=jnp.float32)
            m_scr[...] = m_new
            return carry

        lax.fori_loop(lo, hi, body, 0)
        return acc_scr[...] / jnp.maximum(l_scr[...], 1e-30)

    for g in range(N_KV_HEADS):
        s = jnp.dot(kc_ref[0], qc_ref[0, g, 0], precision=HIGHEST, preferred_element_type=jnp.float32)
        blk_end = lax.broadcasted_iota(jnp.int32, (ncp, tq), 0) * CMP_STRIDE + (CMP_BLOCK - 1)
        ckeep = heads(jnp.where(blk_end <= q0 + lax.broadcasted_iota(jnp.int32, (ncp, tq), 1), 1.0, 0.0)) > 0.5
        s = jnp.where(ckeep, s, NEG_INF)
        mx = jnp.max(s, axis=0, keepdims=True)
        p = jnp.where(ckeep, jnp.exp(s - mx), 0.0)
        p = p / jnp.maximum(jnp.sum(p, axis=0, keepdims=True), 1e-30)
        o_c = jnp.dot(vct_ref[0].astype(jnp.bfloat16), p.astype(jnp.bfloat16),
                      preferred_element_type=jnp.float32)
        p_grp = sum(p[:, r * tq:(r + 1) * tq] for r in range(GQA_RATIO))

        jj = lax.broadcasted_iota(jnp.int32, (n_sb, ncp), 0)
        nn = lax.broadcasted_iota(jnp.int32, (n_sb, ncp), 1)
        overlap = ((nn >= ratio * jj - (N_HALF - 1)) & (nn <= ratio * jj + ratio - 1)).astype(jnp.float32)
        score = jnp.dot(overlap, p_grp, precision=HIGHEST, preferred_element_type=jnp.float32)
        jt = lax.broadcasted_iota(jnp.int32, (n_sb, tq), 0)
        qpt = q0 + lax.broadcasted_iota(jnp.int32, (n_sb, tq), 1)
        cur = qpt // SEL_BLOCK
        forced = (jt == 0) | (jt == cur) | (jt == cur - 1)
        valid = jt * SEL_BLOCK <= qpt
        score = jnp.where(forced, FORCED_SCORE, score)
        score = jnp.where(valid, score, INVALID_SCORE)
        jf = jt.astype(jnp.float32)
        sel_t = jnp.zeros((n_sb, tq), jnp.float32)
        for _ in range(n_sel):
            best = jnp.max(score, axis=0, keepdims=True)
            first = jnp.min(jnp.where(score == best, jf, float(n_sb)), axis=0, keepdims=True)
            hit = jf == first
            sel_t = jnp.where(hit, 1.0, sel_t)
            score = jnp.where(hit, -3e38, score)
        if n_sb < LANES:
            sel_t = jnp.concatenate([sel_t, jnp.zeros((LANES - n_sb, tq), jnp.float32)], axis=0)
        sel_bf = sel_t.astype(jnp.bfloat16)

        def sel_keep(k0):
            blk_of_key = (k0 + lax.broadcasted_iota(jnp.int32, (tk, LANES), 0)) // SEL_BLOCK
            expand = (blk_of_key == lax.broadcasted_iota(jnp.int32, (tk, LANES), 1)).astype(jnp.bfloat16)
            chosen = jnp.dot(expand, sel_bf, preferred_element_type=jnp.float32)
            return jnp.where((chosen > 0.5) & (k0 + kpos_i <= qpos), 1.0, 0.0)

        qr = qr_ref[0, g, 0]
        hi = (q0 + tq + tk - 1) // tk
        o_s = attend(qr, ks_ref, vst_ref, 0, hi, sel_keep)

        def win_keep(k0):
            kpos = k0 + kpos_i
            return jnp.where((kpos <= qpos) & (kpos > qpos - WINDOW), 1.0, 0.0)

        o_w = attend(qr, kw_ref, vwt_ref, jnp.maximum(q0 - WINDOW, 0) // tk, hi, win_keep)

        rows_g = slice(g * HEAD_DIM, (g + 1) * HEAD_DIM)
        for r in range(GQA_RATIO):
            col = (g * GQA_RATIO + r) * 3
            sl = slice(r * tq, (r + 1) * tq)
            comb = (gates[col:col + 1, :] * o_c[rows_g, sl] + gates[col + 1:col + 2, :] * o_s[rows_g, sl]
                    + gates[col + 2:col + 3, :] * o_w[rows_g, sl])
            o_ref[0, 0, r, rows_g, :] = comb


def _nsa_prompt_t_pallas(q, q_rot, blocks, kv_sel, kv_win, gt):
    b, t = q.shape[0], q.shape[1]
    tq, tk = NSA_TQ, NSA_TKT
    assert t % tq == 0 and t % tk == 0 and t // SEL_BLOCK <= LANES and gt.shape[-1] % 8 == 0
    ncp = t // CMP_STRIDE
    n_qt, n_kt = t // tq, t // tk
    cols = GQA_RATIO * tq

    def group_pad_t(a, dtype):
        a = (a * ATTN_SCALE).reshape(b, n_qt, tq, N_KV_HEADS, GQA_RATIO, HEAD_DIM).transpose(0, 3, 1, 5, 4, 2)
        eye = jnp.eye(N_KV_HEADS, dtype=a.dtype)[None, :, None, :, None, None, None]
        return (a[:, :, :, None] * eye).reshape(b, N_KV_HEADS, n_qt, LANES, cols).astype(dtype)

    def values_t(kv):
        v = kv[:, :, 1].astype(jnp.bfloat16).reshape(b, n_kt, tk, LANES)
        return v.transpose(0, 1, 3, 2)

    qc = group_pad_t(q, jnp.float32)
    qr = group_pad_t(q_rot, jnp.bfloat16)
    blk = jnp.pad(blocks, ((0, 0), (0, ncp - blocks.shape[1]), (0, 0), (0, 0), (0, 0))).reshape(b, ncp, 2, LANES)
    kc = blk[:, :, 0]
    vct = blk[:, :, 1].transpose(0, 2, 1)
    ks = kv_sel[:, :, 0].astype(jnp.bfloat16).reshape(b, t, LANES)
    kw = kv_win[:, :, 0].astype(jnp.bfloat16).reshape(b, t, LANES)
    gtt = gt.astype(jnp.float32).reshape(b, n_qt, tq, gt.shape[-1]).transpose(0, 1, 3, 2)

    q_spec = pl.BlockSpec((1, N_KV_HEADS, 1, LANES, cols), lambda i, j: (i, 0, j, 0, 0))

    def whole(shape):
        nd = len(shape)
        return pl.BlockSpec((1,) + shape, lambda i, j: (i,) + (0,) * nd)

    out = pl.pallas_call(
        functools.partial(_nsa_prompt_t_kernel, seq=t),
        grid=(b, n_qt),
        in_specs=[q_spec, q_spec, whole((ncp, LANES)), whole((LANES, ncp)), whole((t, LANES)),
                  whole((n_kt, LANES, tk)), whole((t, LANES)), whole((n_kt, LANES, tk)),
                  pl.BlockSpec((1, 1, gt.shape[-1], tq), lambda i, j: (i, j, 0, 0))],
        out_specs=pl.BlockSpec((1, 1, GQA_RATIO, LANES, tq), lambda i, j: (i, j, 0, 0, 0)),
        out_shape=jax.ShapeDtypeStruct((b, n_qt, GQA_RATIO, LANES, tq), jnp.float32),
        scratch_shapes=[pltpu.VMEM((1, cols), jnp.float32), pltpu.VMEM((1, cols), jnp.float32),
                        pltpu.VMEM((LANES, cols), jnp.float32)],
        compiler_params=pltpu.CompilerParams(dimension_semantics=("arbitrary", "arbitrary")),
        name="nsa_prompt",
    )(qc, qr, kc, vct, ks, values_t(kv_sel), kw, values_t(kv_win), gtt)
    out = out.reshape(b, n_qt, GQA_RATIO, N_KV_HEADS, HEAD_DIM, tq).transpose(0, 1, 5, 3, 2, 4)
    return out.reshape(b, t, NSA_WIDTH)


KV_ROW = 2 * N_KV_HEADS * HEAD_DIM
SEG_W = CMP_STRIDE * KV_ROW
SAMPLE_PG = 32


def _topk_rows(score, n_valid, k):
    rows, width = score.shape
    jf = lax.broadcasted_iota(jnp.int32, (rows, width), 1).astype(jnp.float32)
    score = jnp.where(jf < n_valid, score, -3e38)
    sel = jnp.zeros((rows, width), jnp.float32)
    for _ in range(k):
        best = jnp.max(score, axis=1, keepdims=True)
        first = jnp.min(jnp.where(score == best, jf, float(width)), axis=1, keepdims=True)
        hit = jf == first
        sel = jnp.where(hit, 1.0, sel)
        score = jnp.where(hit, -3e38, score)
    return sel


def _sample_cmp_kernel(pt_ref, *refs, pg, n_pages, s_len):
    page_refs = refs[:pg]
    (new_ref, w1_ref, b1_ref, w2_ref, b2_ref, qc_ref, oc_ref, sel_ref, t_scr, pseg_scr) = refs[pg:]
    j = pl.program_id(1)
    n_steps = n_pages // pg
    seg_per_page = PAGE_SIZE // CMP_STRIDE
    m_rows = pg * seg_per_page
    past = n_pages * PAGE_SIZE
    n_seg = past // CMP_STRIDE
    rows = GQA_RATIO * s_len

    for i in range(pg):
        for c in range(2):
            t_scr[c, i * PAGE_SIZE:(i + 1) * PAGE_SIZE, :] = page_refs[i][0, c * LANES:(c + 1) * LANES, :].T
    for c in range(2):
        xc = jnp.concatenate([t_scr[c, pl.ds(r, m_rows, stride=CMP_STRIDE), :]
                              for r in range(CMP_STRIDE)], axis=1)
        xn = jnp.concatenate([new_ref[0, :, r * KV_ROW + c * LANES:r * KV_ROW + (c + 1) * LANES]
                              for r in range(CMP_STRIDE)], axis=1)
        xc = jnp.concatenate([xc, xn], axis=0).astype(jnp.bfloat16)
        pseg_scr[c, pl.ds(pl.multiple_of(j * m_rows, m_rows), m_rows + 8), :] = jnp.dot(
            xc, w1_ref[c], preferred_element_type=jnp.float32)

    @pl.when(j == n_steps - 1)
    def _():
        kv = []
        for c in range(2):
            acc = jnp.zeros((n_seg, LANES), jnp.float32) + b2_ref[c]
            for g in range(N_KV_HEADS):
                lo = g * N_HALF * CMP_HIDDEN
                hid = (pseg_scr[c, 0:n_seg, lo:lo + CMP_HIDDEN]
                       + pseg_scr[c, 1:n_seg + 1, lo + CMP_HIDDEN:lo + 2 * CMP_HIDDEN] + b1_ref[c])
                hid = jax.nn.gelu(hid)
                acc = acc + jnp.dot(hid.astype(jnp.bfloat16), w2_ref[c, g], preferred_element_type=jnp.float32)
            kv.append(acc)
        k_c, v_c = kv
        n_sb = past // SEL_BLOCK + -(-s_len // SEL_BLOCK)
        width = sel_ref.shape[-1]
        ratio = SEL_BLOCK // CMP_STRIDE
        tok = lax.broadcasted_iota(jnp.int32, (rows, n_seg), 0) % s_len
        blk_end = lax.broadcasted_iota(jnp.int32, (rows, n_seg), 1) * CMP_STRIDE + (CMP_BLOCK - 1)
        cmask = blk_end <= past + tok
        same_tok = (lax.broadcasted_iota(jnp.int32, (rows, rows), 0) % s_len
                    == lax.broadcasted_iota(jnp.int32, (rows, rows), 1) % s_len).astype(jnp.float32)
        nn = lax.broadcasted_iota(jnp.int32, (n_seg, width), 0)
        jj = lax.broadcasted_iota(jnp.int32, (n_seg, width), 1)
        overlap = ((nn >= ratio * jj - (N_HALF - 1)) & (nn <= ratio * jj + ratio - 1)).astype(jnp.float32)
        jb = lax.broadcasted_iota(jnp.int32, (rows, width), 1)
        qp = past + lax.broadcasted_iota(jnp.int32, (rows, width), 0) % s_len
        cur = qp // SEL_BLOCK
        forced = (jb == 0) | (jb == cur) | (jb == cur - 1)
        valid = jb * SEL_BLOCK <= qp
        for g in range(N_KV_HEADS):
            s = _dot_nt(qc_ref[0, g], k_c, HIGHEST)
            s = jnp.where(cmask, s, NEG_INF)
            mx = jnp.max(s, axis=-1, keepdims=True)
            p = jnp.where(cmask, jnp.exp(s - mx), 0.0)
            p = p / jnp.maximum(jnp.sum(p, axis=-1, keepdims=True), 1e-30)
            oc_ref[0, g] = jnp.dot(p.astype(jnp.bfloat16), v_c.astype(jnp.bfloat16),
                                   preferred_element_type=jnp.float32)
            p_grp = jnp.dot(same_tok, p, precision=HIGHEST, preferred_element_type=jnp.float32)
            score = jnp.dot(p_grp, overlap, precision=HIGHEST, preferred_element_type=jnp.float32)
            score = jnp.where(forced, FORCED_SCORE, score)
            score = jnp.where(valid, score, INVALID_SCORE)
            sel_ref[0, g] = _topk_rows(score, n_sb, min(N_SEL, n_sb))


def _sample_attn_kernel(pt_ref, *refs, pg, n_pages, s_len):
    page_refs = refs[:pg]
    (qr_ref, sel_ref, tail_ref, win_ref, wnew_ref, oc_ref, gt_ref, o_ref,
     k_scr, v_scr, m_scr, l_scr, acc_scr) = refs[pg:]
    j = pl.program_id(1)
    n_steps = n_pages // pg
    rows = GQA_RATIO * s_len
    keys = pg * PAGE_SIZE
    past = n_pages * PAGE_SIZE
    width = sel_ref.shape[-1]

    @pl.when(j == 0)
    def _():
        m_scr[...] = jnp.full(m_scr.shape, NEG_INF, jnp.float32)
        l_scr[...] = jnp.zeros(l_scr.shape, jnp.float32)
        acc_scr[...] = jnp.zeros(acc_scr.shape, jnp.float32)

    for i in range(pg):
        k_scr[:, i * PAGE_SIZE:(i + 1) * PAGE_SIZE] = page_refs[i][0, 0:LANES, :].astype(jnp.bfloat16)
        v_scr[:, i * PAGE_SIZE:(i + 1) * PAGE_SIZE] = page_refs[i][0, LANES:2 * LANES, :].astype(jnp.bfloat16)
    blk_of_key = (j * keys + lax.broadcasted_iota(jnp.int32, (width, keys), 1)) // SEL_BLOCK
    expand = (blk_of_key == lax.broadcasted_iota(jnp.int32, (width, keys), 0)).astype(jnp.bfloat16)
    for g in range(N_KV_HEADS):
        q = qr_ref[0, g]
        s = jnp.dot(q, k_scr[...], preferred_element_type=jnp.float32)
        mask = jnp.dot(sel_ref[0, g].astype(jnp.bfloat16), expand, preferred_element_type=jnp.float32) > 0.5
        s = jnp.where(mask, s, NEG_INF)
        m_old = m_scr[g]
        m_new = jnp.maximum(m_old, jnp.max(s, axis=-1, keepdims=True))
        alpha = jnp.exp(m_old - m_new)
        p = jnp.where(mask, jnp.exp(s - m_new), 0.0)
        l_scr[g] = alpha * l_scr[g] + jnp.sum(p, axis=-1, keepdims=True)
        acc_scr[g] = alpha * acc_scr[g] + _dot_nt(p.astype(jnp.bfloat16), v_scr[...])
        m_scr[g] = m_new

    @pl.when(j == n_steps - 1)
    def _():
        tok8 = lax.broadcasted_iota(jnp.int32, (rows, 8), 0) % s_len
        new_ok = lax.broadcasted_iota(jnp.int32, (rows, 8), 1) <= tok8
        wb = win_ref.shape[2]
        tokw = lax.broadcasted_iota(jnp.int32, (rows, wb), 0) % s_len
        kpos = past - wb + lax.broadcasted_iota(jnp.int32, (rows, wb), 1)
        win_ok = (kpos > past + tokw - WINDOW) & (kpos >= 0)
        gates = jax.nn.sigmoid(gt_ref[0])
        for g in range(N_KV_HEADS):
            q = qr_ref[0, g]
            tail_sel = sel_ref[0, g][:, past // SEL_BLOCK:past // SEL_BLOCK + 1] > 0.5
            t_mask = new_ok & tail_sel
            s_t = jnp.where(t_mask, _dot_nt(q, tail_ref[0, :, 0:LANES].astype(jnp.bfloat16)), NEG_INF)
            m_old = m_scr[g]
            m_new = jnp.maximum(m_old, jnp.max(s_t, axis=-1, keepdims=True))
            alpha = jnp.exp(m_old - m_new)
            p_t = jnp.where(t_mask, jnp.exp(s_t - m_new), 0.0)
            l_s = alpha * l_scr[g] + jnp.sum(p_t, axis=-1, keepdims=True)
            o_s = (alpha * acc_scr[g] + jnp.dot(p_t.astype(jnp.bfloat16),
                                                tail_ref[0, :, LANES:2 * LANES].astype(jnp.bfloat16),
                                                preferred_element_type=jnp.float32)) / jnp.maximum(l_s, 1e-30)
            s_w = jnp.where(win_ok, jnp.dot(q, win_ref[0, 0:LANES, :].astype(jnp.bfloat16),
                                            preferred_element_type=jnp.float32), NEG_INF)
            s_n = jnp.where(new_ok, _dot_nt(q, wnew_ref[0, :, 0:LANES].astype(jnp.bfloat16)), NEG_INF)
            mw = jnp.maximum(jnp.max(s_w, axis=-1, keepdims=True), jnp.max(s_n, axis=-1, keepdims=True))
            p_w = jnp.where(win_ok, jnp.exp(s_w - mw), 0.0)
            p_n = jnp.where(new_ok, jnp.exp(s_n - mw), 0.0)
            l_w = jnp.sum(p_w, axis=-1, keepdims=True) + jnp.sum(p_n, axis=-1, keepdims=True)
            o_w = (_dot_nt(p_w.astype(jnp.bfloat16), win_ref[0, LANES:2 * LANES, :].astype(jnp.bfloat16))
                   + jnp.dot(p_n.astype(jnp.bfloat16), wnew_ref[0, :, LANES:2 * LANES].astype(jnp.bfloat16),
                             preferred_element_type=jnp.float32)) / jnp.maximum(l_w, 1e-30)
            gl = gates[g]
            o_ref[0, g] = gl[:, 0:1] * oc_ref[0, g] + gl[:, 1:2] * o_s + gl[:, 2:3] * o_w


def _nsa_sample_pallas(q, q_rot, kv_cmp, kv_sel, kv_win, gt, cache_cmp, cache_sel, win_buf, page_table,
                       w1, b1, w2, b2):
    pg = SAMPLE_PG
    db, s_len = q.shape[0], q.shape[1]
    n_phys = cache_cmp.shape[0]
    n_pages = page_table.shape[1]
    assert n_pages % pg == 0 and s_len <= 8 and s_len <= CMP_STRIDE
    past = n_pages * PAGE_SIZE
    n_seg = past // CMP_STRIDE
    n_sb = past // SEL_BLOCK + 1
    width = -(-n_sb // LANES) * LANES
    rows = GQA_RATIO * s_len
    seg_per_page = PAGE_SIZE // CMP_STRIDE

    def group_rows(a, dtype):
        a = (a * ATTN_SCALE).reshape(db, s_len, N_KV_HEADS, GQA_RATIO, HEAD_DIM).transpose(0, 2, 3, 1, 4)
        eye = jnp.eye(N_KV_HEADS, dtype=a.dtype)[None, :, None, None, :, None]
        return (a[:, :, :, :, None, :] * eye).reshape(db, N_KV_HEADS, rows, LANES).astype(dtype)

    def pad8(a):
        return jnp.pad(a.reshape(db, s_len, KV_ROW), ((0, 0), (0, 8 - s_len), (0, 0)))

    qc = group_rows(q, jnp.float32)
    qr = group_rows(q_rot, jnp.bfloat16)
    new_seg = jnp.pad(kv_cmp.reshape(db, 1, s_len * KV_ROW), ((0, 0), (0, 7), (0, SEG_W - s_len * KV_ROW)))
    eye_g = jnp.eye(N_KV_HEADS, dtype=w1.dtype)
    w1t = w1.transpose(0, 2, 3, 1, 4)
    w1_bd = (w1t[:, :, None, :, None, :, :] * eye_g[None, None, :, None, :, None, None]).reshape(
        2, CMP_STRIDE * N_KV_HEADS * HEAD_DIM, N_KV_HEADS * N_HALF * CMP_HIDDEN).astype(jnp.bfloat16)
    w2_g = (w2[:, None, :, None, :] * eye_g[None, :, None, :, None]).reshape(
        2, N_KV_HEADS, CMP_HIDDEN, LANES).astype(jnp.bfloat16)
    b2_t = jnp.tile(b2, (1, N_KV_HEADS)).reshape(2, 1, LANES)
    b1_r = b1.reshape(2, 1, CMP_HIDDEN)

    def page_specs(block):
        return [pl.BlockSpec(block, functools.partial(lambda b, j, pt, i: (pt[b, j * pg + i], 0, 0), i=i))
                for i in range(pg)]

    def per_seq(shape):
        nd = len(shape)
        return pl.BlockSpec((1,) + shape, lambda b, j, pt: (b,) + (0,) * nd)

    def const(shape):
        nd = len(shape)
        return pl.BlockSpec(shape, lambda b, j, pt: (0,) * nd)

    def feature_major(a):
        return a.transpose(0, 2, 3, 4, 1).reshape(a.shape[0], KV_ROW, a.shape[1])

    page_block = (1, KV_ROW, PAGE_SIZE)
    m_rows = pg * seg_per_page
    o_c, sel = pl.pallas_call(
        functools.partial(_sample_cmp_kernel, pg=pg, n_pages=n_pages, s_len=s_len),
        grid_spec=pltpu.PrefetchScalarGridSpec(
            num_scalar_prefetch=1, grid=(db, n_pages // pg),
            in_specs=page_specs(page_block) + [
                per_seq((8, SEG_W)), const(w1_bd.shape), const(b1_r.shape), const(w2_g.shape), const(b2_t.shape),
                per_seq((N_KV_HEADS, rows, LANES))],
            out_specs=[per_seq((N_KV_HEADS, rows, LANES)), per_seq((N_KV_HEADS, rows, width))],
            scratch_shapes=[pltpu.VMEM((2, pg * PAGE_SIZE, LANES), jnp.float32),
                            pltpu.VMEM((2, n_seg + 8, N_KV_HEADS * N_HALF * CMP_HIDDEN), jnp.float32)]),
        out_shape=[jax.ShapeDtypeStruct((db, N_KV_HEADS, rows, LANES), jnp.float32),
                   jax.ShapeDtypeStruct((db, N_KV_HEADS, rows, width), jnp.float32)],
        compiler_params=pltpu.CompilerParams(dimension_semantics=("arbitrary", "arbitrary"),
                                             vmem_limit_bytes=56 * 1024 * 1024),
        name="sample_cmp_select",
    )(page_table, *([feature_major(cache_cmp)] * pg), new_seg, w1_bd, b1_r, w2_g, b2_t, qc)

    gl = jnp.pad(gt.astype(jnp.float32).reshape(db, s_len, N_KV_HEADS, GQA_RATIO, 3).transpose(0, 2, 3, 1, 4)
                 .reshape(db, N_KV_HEADS, rows, 3), ((0, 0), (0, 0), (0, 0), (0, LANES - 3)))
    wb = win_buf.shape[1]
    out = pl.pallas_call(
        functools.partial(_sample_attn_kernel, pg=pg, n_pages=n_pages, s_len=s_len),
        grid_spec=pltpu.PrefetchScalarGridSpec(
            num_scalar_prefetch=1, grid=(db, n_pages // pg),
            in_specs=page_specs(page_block) + [
                per_seq((N_KV_HEADS, rows, LANES)), per_seq((N_KV_HEADS, rows, width)), per_seq((8, KV_ROW)),
                per_seq((KV_ROW, wb)), per_seq((8, KV_ROW)), per_seq((N_KV_HEADS, rows, LANES)),
                per_seq((N_KV_HEADS, rows, LANES))],
            out_specs=per_seq((N_KV_HEADS, rows, LANES)),
            scratch_shapes=[pltpu.VMEM((LANES, pg * PAGE_SIZE), jnp.bfloat16),
                            pltpu.VMEM((LANES, pg * PAGE_SIZE), jnp.bfloat16),
                            pltpu.VMEM((N_KV_HEADS, rows, 1), jnp.float32),
                            pltpu.VMEM((N_KV_HEADS, rows, 1), jnp.float32),
                            pltpu.VMEM((N_KV_HEADS, rows, LANES), jnp.float32)]),
        out_shape=jax.ShapeDtypeStruct((db, N_KV_HEADS, rows, LANES), jnp.float32),
        compiler_params=pltpu.CompilerParams(dimension_semantics=("arbitrary", "arbitrary")),
        name="sample_sel_win_attn",
    )(page_table, *([feature_major(cache_sel)] * pg), qr, sel, pad8(kv_sel),
      feature_major(win_buf), pad8(kv_win), o_c, gl)
    out = out.reshape(db, N_KV_HEADS, GQA_RATIO, s_len, N_KV_HEADS, HEAD_DIM)
    out = jnp.stack([out[:, g, :, :, g, :] for g in range(N_KV_HEADS)], axis=1)
    return out.transpose(0, 3, 1, 2, 4).reshape(db, s_len, NSA_WIDTH)


ROUTER_TT = 256
MOE_BM = 256


def _router_kernel(x_ref, mix_ref, wout_ref, g_ref, wrt_ref, br_ref,
                   x1_ref, h_ref, eidx_ref, gate_ref, rank_ref, cnt_ref, wout_bf, run_scr):
    tt = x_ref.shape[0]
    n_e = wrt_ref.shape[0]

    @pl.when(pl.program_id(0) == 0)
    def _():
        wout_bf[...] = wout_ref[...].astype(jnp.bfloat16)
        run_scr[...] = jnp.zeros_like(run_scr)

    x1 = x_ref[...] + jnp.dot(mix_ref[...].astype(jnp.bfloat16), wout_bf[...], preferred_element_type=jnp.float32)
    x1_ref[...] = x1
    hn = x1 * lax.rsqrt(jnp.mean(x1 * x1, axis=-1, keepdims=True) + RMS_EPS) * g_ref[...]
    h_ref[...] = hn

    score = _dot_nt(wrt_ref[...], hn, HIGHEST) + br_ref[...]
    ef = lax.broadcasted_iota(jnp.int32, (n_e, tt), 0).astype(jnp.float32)
    vals, hits = [], []
    for k in range(TOP_K):
        best = jnp.max(score, axis=0, keepdims=True)
        first = jnp.min(jnp.where(score == best, ef, float(n_e)), axis=0, keepdims=True)
        hit = ef == first
        vals.append(best)
        hits.append(hit)
        eidx_ref[k:k + 1, :] = first.astype(jnp.int32)
        score = jnp.where(hit, -3e38, score)
    exps = [jnp.exp(v - vals[0]) for v in vals]
    denom = sum(exps[1:], exps[0])
    for k in range(TOP_K):
        gate_ref[k:k + 1, :] = exps[k] / denom

    chosen = functools.reduce(jnp.logical_or, hits)
    before = (lax.broadcasted_iota(jnp.int32, (tt, tt), 0) < lax.broadcasted_iota(jnp.int32, (tt, tt), 1))
    earlier = jnp.dot(chosen.astype(jnp.bfloat16), before.astype(jnp.bfloat16), preferred_element_type=jnp.float32)
    pos = earlier + run_scr[...]
    for k in range(TOP_K):
        rank_ref[k:k + 1, :] = jnp.sum(jnp.where(hits[k], pos, 0.0), axis=0, keepdims=True).astype(jnp.int32)
    run_scr[...] = run_scr[...] + jnp.sum(chosen.astype(jnp.float32), axis=1, keepdims=True)
    cnt_ref[...] = jnp.broadcast_to(run_scr[...], cnt_ref.shape).astype(jnp.int32)


def _router_pallas(x, mix, w_out, g_ffn, w_router, b_router):
    n, d = x.shape
    c = mix.shape[1]
    n_e = w_router.shape[1]
    tt = ROUTER_TT
    assert n % tt == 0

    def row(w):
        return pl.BlockSpec((tt, w), lambda i: (i, 0))

    def full(a, b):
        return pl.BlockSpec((a, b), lambda i: (0, 0))

    k4 = pl.BlockSpec((TOP_K, tt), lambda i: (0, i))
    return pl.pallas_call(
        _router_kernel,
        grid=(n // tt,),
        in_specs=[row(d), row(c), full(c, d), full(1, d), full(n_e, d), full(n_e, 1)],
        out_specs=[row(d), row(d), k4, k4, k4, full(n_e, LANES)],
        out_shape=[jax.ShapeDtypeStruct((n, d), jnp.float32), jax.ShapeDtypeStruct((n, d), jnp.float32),
                   jax.ShapeDtypeStruct((TOP_K, n), jnp.int32), jax.ShapeDtypeStruct((TOP_K, n), jnp.float32),
                   jax.ShapeDtypeStruct((TOP_K, n), jnp.int32), jax.ShapeDtypeStruct((n_e, LANES), jnp.int32)],
        scratch_shapes=[pltpu.VMEM((c, d), jnp.bfloat16), pltpu.VMEM((n_e, 1), jnp.float32)],
        compiler_params=pltpu.CompilerParams(dimension_semantics=("arbitrary",)),
        name="outproj_router",
    )(x, mix, w_out, g_ffn.reshape(1, d), w_router.T, b_router.reshape(n_e, 1))


def _expert_kernel(blk_e_ref, n_used_ref, x_ref, wup_ref, bup_ref, wdn_ref, bdn_ref, o_ref, wup_bf, wdn_bf):
    i = pl.program_id(0)
    d_ff = wdn_ref.shape[1]

    @pl.when(i < n_used_ref[0])
    def _():
        e = blk_e_ref[i]
        prev = blk_e_ref[jnp.maximum(i - 1, 0)]

        @pl.when((i == 0) | (e != prev))
        def _():
            wup_bf[...] = wup_ref[0].astype(jnp.bfloat16)
            wdn_bf[...] = wdn_ref[0].astype(jnp.bfloat16)

        u = jnp.dot(x_ref[...].astype(jnp.bfloat16), wup_bf[...], preferred_element_type=jnp.float32) + bup_ref[0]
        glu = jnp.minimum(u[:, :d_ff], SWIGLU_LIMIT)
        lin = jnp.clip(u[:, d_ff:], -SWIGLU_LIMIT, SWIGLU_LIMIT)
        a = glu * jax.nn.sigmoid(SWIGLU_ALPHA * glu) * (lin + 1.0)
        o_ref[...] = jnp.dot(a.astype(jnp.bfloat16), wdn_bf[...], preferred_element_type=jnp.float32) + bdn_ref[0]

    @pl.when(i >= n_used_ref[0])
    def _():
        o_ref[...] = jnp.zeros_like(o_ref)


def _expert_vmem_bytes(bm, d, f2, d_ff):
    weights = 2 * 4 * (d * f2 + d_ff * d) + 2 * (d * f2 + d_ff * d)
    rows = 2 * bm * d * (4 + 4)
    temps = bm * f2 * 4 * 2 + bm * d_ff * (4 + 2)
    return weights + rows + temps


def _experts_pallas(xs, blk_e, n_used, w_up, b_up, w_down, b_down, bm):
    n_slots, d = xs.shape
    n_e, _, f2 = w_up.shape
    d_ff = w_down.shape[1]
    grid_spec = pltpu.PrefetchScalarGridSpec(
        num_scalar_prefetch=2,
        grid=(n_slots // bm,),
        in_specs=[pl.BlockSpec((bm, d), lambda i, be, nu: (i, 0)),
                  pl.BlockSpec((1, d, f2), lambda i, be, nu: (be[i], 0, 0)),
                  pl.BlockSpec((1, 1, f2), lambda i, be, nu: (be[i], 0, 0)),
                  pl.BlockSpec((1, d_ff, d), lambda i, be, nu: (be[i], 0, 0)),
                  pl.BlockSpec((1, 1, d), lambda i, be, nu: (be[i], 0, 0))],
        out_specs=pl.BlockSpec((bm, d), lambda i, be, nu: (i, 0)),
        scratch_shapes=[pltpu.VMEM((d, f2), jnp.bfloat16), pltpu.VMEM((d_ff, d), jnp.bfloat16)],
    )
    vmem_limit = _expert_vmem_bytes(bm, d, f2, d_ff) * 5 // 4
    return pl.pallas_call(
        _expert_kernel,
        grid_spec=grid_spec,
        out_shape=jax.ShapeDtypeStruct((n_slots, d), jnp.float32),
        compiler_params=pltpu.CompilerParams(dimension_semantics=("arbitrary",), vmem_limit_bytes=vmem_limit),
        name="expert_mlp",
    )(blk_e, n_used, xs, w_up, b_up.reshape(n_e, 1, f2), w_down, b_down.reshape(n_e, 1, d))


def _combine_norm_kernel(x1_ref, og_ref, gate_ref, g_ref, o_ref):
    y = x1_ref[...]
    gates = gate_ref[...]
    for k in range(TOP_K):
        y = y + gates[:, k:k + 1] * og_ref[k]
    o_ref[...] = y * lax.rsqrt(jnp.mean(y * y, axis=-1, keepdims=True) + RMS_EPS) * g_ref[...]


def _combine_norm_pallas(x1, og, gate_t, g_final):
    n, d = x1.shape
    tt = ROUTER_TT
    return pl.pallas_call(
        _combine_norm_kernel,
        grid=(n // tt,),
        in_specs=[pl.BlockSpec((tt, d), lambda i: (i, 0)), pl.BlockSpec((TOP_K, tt, d), lambda i: (0, i, 0)),
                  pl.BlockSpec((tt, TOP_K), lambda i: (i, 0)), pl.BlockSpec((1, d), lambda i: (0, 0))],
        out_specs=pl.BlockSpec((tt, d), lambda i: (i, 0)),
        out_shape=jax.ShapeDtypeStruct((n, d), jnp.float32),
        compiler_params=pltpu.CompilerParams(dimension_semantics=("arbitrary",)),
        name="combine_final_norm",
    )(x1, og, gate_t, g_final.reshape(1, d))


def _finish_pallas(x, mix, w_out, g_ffn, w_router, b_router, w_up, b_up, w_down, b_down, g_final):
    n, d = x.shape
    n_e = w_router.shape[1]
    bm = MOE_BM
    x1, h, eidx, gate, rank, cnt = _router_pallas(x, mix, w_out, g_ffn, w_router, b_router)
    counts = cnt[:, 0]
    padded = (counts + bm - 1) // bm * bm
    pad_end = jnp.cumsum(padded)
    gstart = pad_end - padded
    pick = eidx[None] == jnp.arange(n_e, dtype=jnp.int32)[:, None, None]
    dest = jnp.sum(jnp.where(pick, gstart[:, None, None], 0), axis=0) + rank
    nb = -(-(n * TOP_K) // bm) + n_e
    blk_start = jnp.arange(nb, dtype=jnp.int32) * bm
    blk_e = jnp.minimum(jnp.sum((pad_end[None, :] <= blk_start[:, None]).astype(jnp.int32), axis=1), n_e - 1)
    n_used = (pad_end[-1] // bm).astype(jnp.int32).reshape(1)
    tok = jnp.broadcast_to(jnp.arange(n, dtype=jnp.int32)[None, :], (TOP_K, n))
    src = jnp.zeros((nb * bm,), jnp.int32).at[dest.reshape(-1)].set(tok.reshape(-1))
    out = _experts_pallas(h[src], blk_e, n_used, w_up, b_up, w_down, b_down, bm)
    return _combine_norm_pallas(x1, out[dest], gate.T, g_final)


def kernel(x_prompt, x_sample, cache_cmp_kv, cache_sel_kv, cache_win_kv, state_ret, page_table, g_attn, w_in, w_cmp1, b_cmp1, w_cmp2, b_cmp2, w_out, g_ffn, w_router, b_router, w_up, b_up, w_down, b_down, g_final):
    seq = x_prompt.shape[1]
    past = page_table.shape[1] * PAGE_SIZE
    pos_p = jnp.arange(seq, dtype=jnp.int32)
    pos_s = past + jnp.arange(x_sample.shape[1], dtype=jnp.int32)
    log_g = _ret_log_decay()
    assert DEPTH == 1
    l = 0
    d = x_prompt.shape[-1]
    q, kv_cmp_p, kv_sel_p, kv_win_p, _, r_q, r_k, r_v, r_g, gt = _mixer_inputs(x_prompt, pos_p, g_attn[l], w_in[l])
    blocks = _compress_blocks(_segment_proj(kv_cmp_p, w_cmp1[l]), b_cmp1[l], w_cmp2[l], b_cmp2[l])
    o_nsa = _nsa_prompt_t_pallas(q, _rope(q, pos_p), blocks, kv_sel_p, kv_win_p, gt)
    ret_p, o_r = _ret_prompt(r_q, r_k, r_v, log_g)
    mix_p = jnp.concatenate([o_nsa, _ret_out(o_r, r_g)], axis=-1)
    q, kv_cmp_s, kv_sel_s, kv_win_s, _, r_q, r_k, r_v, r_g, gt = _mixer_inputs(x_sample, pos_s, g_attn[l], w_in[l])
    o_nsa_s = _nsa_sample_pallas(q, _rope(q, pos_s), kv_cmp_s, kv_sel_s, kv_win_s, gt, cache_cmp_kv[:, l],
                                 cache_sel_kv[:, l], cache_win_kv[:, l], page_table,
                                 w_cmp1[l], b_cmp1[l], w_cmp2[l], b_cmp2[l])
    win_s = jnp.concatenate([cache_win_kv[:, l], kv_win_s], axis=1)[:, x_sample.shape[1]:]
    ret_s, o_r = _ret_chunk(state_ret[:, l].astype(jnp.float32), r_q.astype(jnp.float32), r_k.astype(jnp.float32), r_v.astype(jnp.float32), log_g)
    mix_s = jnp.concatenate([o_nsa_s, _ret_out(o_r, r_g)], axis=-1)
    n_p = x_prompt.shape[0] * seq
    x_all = jnp.concatenate([x_prompt.reshape(n_p, d), x_sample.reshape(-1, d)], axis=0)
    mix_all = jnp.concatenate([mix_p.reshape(n_p, MIX_WIDTH), mix_s.reshape(-1, MIX_WIDTH)], axis=0)
    y = _finish_pallas(x_all, mix_all, w_out[l], g_ffn[l], w_router[l], b_router[l], w_up[l], b_up[l], w_down[l], b_down[l], g_final)
    y_prompt = y[:n_p].reshape(x_prompt.shape)
    y_sample = y[n_p:].reshape(x_sample.shape)
    win_p = kv_win_p[:, seq - min(WINDOW, seq):]
    return (y_prompt, y_sample, kv_cmp_p[:, None], kv_sel_p[:, None], win_p[:, None], ret_p[:, None],
            kv_cmp_s[:, None], kv_sel_s[:, None], win_s[:, None], ret_s[:, None])
```

```python
import functools
import jax, jax.numpy as jnp
from jax import lax
import numpy as np
from jax.experimental import pallas as pl
from jax.experimental.pallas import tpu as pltpu

DEPTH = 1
PAGE_SIZE = 128

HEAD_DIM = 64
N_NSA_HEADS = 8
N_KV_HEADS = 2
GQA_RATIO = N_NSA_HEADS // N_KV_HEADS
CMP_BLOCK = 32
CMP_STRIDE = 16
N_HALF = CMP_BLOCK // CMP_STRIDE
CMP_HIDDEN = 256
SEL_BLOCK = 64
N_SEL = 16
WINDOW = 512
N_RET_HEADS = 4
RET_DK = 64
RET_DV = 128
RET_CHUNK = 128
TOP_K = 4
SWIGLU_ALPHA = 1.702
SWIGLU_LIMIT = 7.0
ROPE_THETA = 10000.0
RMS_EPS = 1e-5
NEG_INF = -1e30
FORCED_SCORE = 1e6
INVALID_SCORE = -1e9
NSA_WIDTH = N_NSA_HEADS * HEAD_DIM
KV_WIDTH = N_KV_HEADS * HEAD_DIM
RET_QK_WIDTH = N_RET_HEADS * RET_DK
RET_WIDTH = N_RET_HEADS * RET_DV
IN_SIZES = (NSA_WIDTH, KV_WIDTH, KV_WIDTH, KV_WIDTH, KV_WIDTH, KV_WIDTH, KV_WIDTH, 3 * N_NSA_HEADS, RET_QK_WIDTH, RET_QK_WIDTH, RET_WIDTH, RET_WIDTH)
MIX_WIDTH = NSA_WIDTH + RET_WIDTH
ATTN_SCALE = HEAD_DIM ** -0.5

LANES = 128
HIGHEST = lax.Precision.HIGHEST


def _rmsnorm(x, g):
    xf = x.astype(jnp.float32)
    y = xf * lax.rsqrt(jnp.mean(xf * xf, axis=-1, keepdims=True) + RMS_EPS)
    return (y * g.astype(jnp.float32)).astype(x.dtype)


def _rope(x, pos):
    half = x.shape[-1] // 2
    inv = ROPE_THETA ** (-jnp.arange(half, dtype=jnp.float32) / half)
    ang = pos.astype(jnp.float32)[:, None] * inv[None, :]
    cos = jnp.cos(ang)[None, :, None, :]
    sin = jnp.sin(ang)[None, :, None, :]
    xf = x.astype(jnp.float32)
    x1, x2 = xf[..., :half], xf[..., half:]
    return jnp.concatenate([x1 * cos - x2 * sin, x2 * cos + x1 * sin], axis=-1).astype(x.dtype)


def _mixer_inputs(x, pos, g_attn, w_in):
    b, t = x.shape[0], x.shape[1]
    h = _rmsnorm(x, g_attn)
    proj = jnp.einsum('btd,dc->btc', h, w_in)
    offs = [int(o) for o in np.cumsum(np.array(IN_SIZES))[:-1]]
    q, kc, vc, ks, vs, kw, vw, gt, rq, rk, rv, rg = jnp.split(proj, offs, axis=-1)

    def heads(a, n, d):
        return a.reshape(b, t, n, d)

    def kvh(a):
        return heads(a, N_KV_HEADS, HEAD_DIM)

    kv_cmp = jnp.stack([kvh(kc), kvh(vc)], axis=2)
    kv_sel = jnp.stack([_rope(kvh(ks), pos), kvh(vs)], axis=2)
    kv_win = jnp.stack([_rope(kvh(kw), pos), kvh(vw)], axis=2)
    r_q = _rope(heads(rq, N_RET_HEADS, RET_DK), pos)
    r_k = _rope(heads(rk, N_RET_HEADS, RET_DK), pos) * (RET_DK ** -0.5)
    r_v = heads(rv, N_RET_HEADS, RET_DV)
    return heads(q, N_NSA_HEADS, HEAD_DIM), kv_cmp, kv_sel, kv_win, gt, r_q, r_k, r_v, rg


def _segment_proj(kv_rows, w1):
    b, l = kv_rows.shape[0], kv_rows.shape[1]
    seg = kv_rows.reshape(b, l // CMP_STRIDE, CMP_STRIDE, 2, N_KV_HEADS, HEAD_DIM)
    return jnp.einsum('bsrcgd,chrdk->bscghk', seg, w1)


def _compress_blocks(p_seg, b1, w2, b2):
    n_cmp = p_seg.shape[1] - N_HALF + 1
    hid = sum(p_seg[:, h:h + n_cmp, :, :, h, :] for h in range(N_HALF)) + b1[None, None, :, None, :]
    hid = jax.nn.gelu(hid.astype(jnp.float32))
    out = jnp.einsum('bncgk,ckd->bncgd', hid, w2.astype(jnp.float32))
    return out + b2.astype(jnp.float32)[None, None, :, None, :]


def _ret_log_decay():
    return jnp.log1p(-jnp.exp2(-5.0 - jnp.arange(N_RET_HEADS, dtype=jnp.float32)))


def _ret_chunk(state, q, k, v, log_g):
    c = q.shape[1]
    i = jnp.arange(c, dtype=jnp.float32)
    diff = i[:, None] - i[None, :]
    decay = jnp.where(diff[None] >= 0, jnp.exp(log_g[:, None, None] * jnp.maximum(diff, 0.0)[None]), 0.0)
    scores = jnp.einsum('bihd,bjhd->bhij', q, k) * decay[None]
    inner = jnp.einsum('bhij,bjhv->bihv', scores, v)
    q_dec = jnp.exp(log_g[None, :] * (i[:, None] + 1.0))
    cross = jnp.einsum('bihd,bhdv->bihv', q * q_dec[None, :, :, None], state)
    k_dec = jnp.exp(log_g[None, :] * (c - 1.0 - i[:, None]))
    new_state = jnp.exp(log_g * c)[None, :, None, None] * state + jnp.einsum('bjhd,bjhv->bhdv', k * k_dec[None, :, :, None], v)
    return new_state, inner + cross


def _ret_prompt(r_q, r_k, r_v, log_g):
    b, t = r_q.shape[0], r_q.shape[1]
    n_ch = t // RET_CHUNK

    def chunks(a):
        return a.astype(jnp.float32).reshape(b, n_ch, RET_CHUNK, a.shape[2], a.shape[3]).swapaxes(0, 1)

    s0 = jnp.zeros((b, N_RET_HEADS, RET_DK, RET_DV), jnp.float32)
    s_fin, o = lax.scan(lambda s, xs: _ret_chunk(s, xs[0], xs[1], xs[2], log_g), s0, (chunks(r_q), chunks(r_k), chunks(r_v)))
    return s_fin, o.swapaxes(0, 1).reshape(b, t, N_RET_HEADS, RET_DV)


def _ret_out(o, r_g):
    b, t = o.shape[0], o.shape[1]
    o = o * lax.rsqrt(jnp.mean(o * o, axis=-1, keepdims=True) + RMS_EPS)
    return o.reshape(b, t, RET_WIDTH) * jax.nn.silu(r_g.astype(jnp.float32))


def _dot_nt(a, b, precision=None):
    return lax.dot_general(a, b, (((1,), (1,)), ((), ())), precision=precision, preferred_element_type=jnp.float32)


def _topk_cols(score, n_valid, k):
    height, width = score.shape
    jf = lax.broadcasted_iota(jnp.int32, (height, width), 0).astype(jnp.float32)
    score = jnp.where(jf < n_valid, score, -3e38)
    sel = jnp.zeros((height, width), jnp.float32)
    for _ in range(k):
        best = jnp.max(score, axis=0, keepdims=True)
        first = jnp.min(jnp.where(score == best, jf, float(height)), axis=0, keepdims=True)
        hit = jf == first
        sel = jnp.where(hit, 1.0, sel)
        score = jnp.where(hit, -3e38, score)
    return sel


def _block_scores(p_grp, n_blocks, q_pos):
    n_cmp = p_grp.shape[0]
    ratio = SEL_BLOCK // CMP_STRIDE
    jj = lax.broadcasted_iota(jnp.int32, (n_blocks, n_cmp), 0)
    nn = lax.broadcasted_iota(jnp.int32, (n_blocks, n_cmp), 1)
    overlap = ((nn >= ratio * jj - (N_HALF - 1)) & (nn <= ratio * jj + ratio - 1)).astype(jnp.float32)
    score = jnp.dot(overlap, p_grp, precision=HIGHEST, preferred_element_type=jnp.float32)
    jb = lax.broadcasted_iota(jnp.int32, q_pos.shape, 0)
    cur = q_pos // SEL_BLOCK
    forced = (jb == 0) | (jb == cur) | (jb == cur - 1)
    valid = jb * SEL_BLOCK <= q_pos
    score = jnp.where(forced, FORCED_SCORE, score)
    return jnp.where(valid, score, INVALID_SCORE)


NSA_TQ = 128
NSA_TK = 256


def _nsa_prompt_kernel(qc_ref, qr_ref, kc_ref, vct_ref, ks_ref, vst_ref, kw_ref, vwt_ref, gt_ref, o_ref,
                       m_scr, l_scr, acc_scr, *, seq):
    tq, tk = NSA_TQ, NSA_TK
    cols = GQA_RATIO * tq
    qt = pl.program_id(1)
    q0 = qt * tq
    ncp = kc_ref.shape[1]
    n_sb = seq // SEL_BLOCK
    gates = jax.nn.sigmoid(gt_ref[0, 0])

    def heads(a):
        return jnp.concatenate([a] * GQA_RATIO, axis=1)

    kpos_i = lax.broadcasted_iota(jnp.int32, (tk, tq), 0)
    qpos = q0 + lax.broadcasted_iota(jnp.int32, (tk, tq), 1)

    def attend(q_t, k_ref, vt_ref, lo, hi, keep_fn):
        m_scr[...] = jnp.full((1, cols), NEG_INF, jnp.float32)
        l_scr[...] = jnp.zeros((1, cols), jnp.float32)
        acc_scr[...] = jnp.zeros((LANES, cols), jnp.float32)

        def body(kt, carry):
            k0 = pl.multiple_of(kt * tk, tk)
            keep = keep_fn(k0)
            s = jnp.dot(k_ref[0, pl.ds(k0, tk), :], q_t, preferred_element_type=jnp.float32)
            s = s + heads((keep - 1.0) * -NEG_INF)
            m_old = m_scr[...]
            m_new = jnp.maximum(m_old, jnp.max(s, axis=0, keepdims=True))
            alpha = jnp.exp(m_old - m_new)
            p = jnp.exp(s - m_new) * heads(keep)
            l_scr[...] = alpha * l_scr[...] + jnp.sum(p, axis=0, keepdims=True)
            acc_scr[...] = alpha * acc_scr[...] + jnp.dot(vt_ref[0, kt], p.astype(jnp.bfloat16),
                                                          preferred_element_type=jnp.float32)
            m_scr[...] = m_new
            return carry

        lax.fori_loop(lo, hi, body, 0)
        return acc_scr[...] / jnp.maximum(l_scr[...], 1e-30)

    for g in range(N_KV_HEADS):
        s = jnp.dot(kc_ref[0], qc_ref[0, g, 0], precision=HIGHEST, preferred_element_type=jnp.float32)
        blk_end = lax.broadcasted_iota(jnp.int32, (ncp, tq), 0) * CMP_STRIDE + (CMP_BLOCK - 1)
        ckeep = heads(jnp.where(blk_end <= q0 + lax.broadcasted_iota(jnp.int32, (ncp, tq), 1), 1.0, 0.0)) > 0.5
        s = jnp.where(ckeep, s, NEG_INF)
        mx = jnp.max(s, axis=0, keepdims=True)
        p = jnp.where(ckeep, jnp.exp(s - mx), 0.0)
        p = p / jnp.maximum(jnp.sum(p, axis=0, keepdims=True), 1e-30)
        o_c = jnp.dot(vct_ref[0].astype(jnp.bfloat16), p.astype(jnp.bfloat16),
                      preferred_element_type=jnp.float32)
        p_grp = sum(p[:, r * tq:(r + 1) * tq] for r in range(GQA_RATIO))

        score = _block_scores(p_grp, n_sb, q0 + lax.broadcasted_iota(jnp.int32, (n_sb, tq), 1))
        sel_t = _topk_cols(score, n_sb, min(N_SEL, n_sb))
        if n_sb < LANES:
            sel_t = jnp.concatenate([sel_t, jnp.zeros((LANES - n_sb, tq), jnp.float32)], axis=0)
        sel_bf = sel_t.astype(jnp.bfloat16)

        def sel_keep(k0):
            blk_of_key = (k0 + lax.broadcasted_iota(jnp.int32, (tk, LANES), 0)) // SEL_BLOCK
            expand = (blk_of_key == lax.broadcasted_iota(jnp.int32, (tk, LANES), 1)).astype(jnp.bfloat16)
            chosen = jnp.dot(expand, sel_bf, preferred_element_type=jnp.float32)
            return jnp.where((chosen > 0.5) & (k0 + kpos_i <= qpos), 1.0, 0.0)

        qr = qr_ref[0, g, 0]
        hi = (q0 + tq + tk - 1) // tk
        o_s = attend(qr, ks_ref, vst_ref, 0, hi, sel_keep)

        def win_keep(k0):
            kpos = k0 + kpos_i
            return jnp.where((kpos <= qpos) & (kpos > qpos - WINDOW), 1.0, 0.0)

        o_w = attend(qr, kw_ref, vwt_ref, jnp.maximum(q0 - WINDOW, 0) // tk, hi, win_keep)

        rows_g = slice(g * HEAD_DIM, (g + 1) * HEAD_DIM)
        for r in range(GQA_RATIO):
            col = (g * GQA_RATIO + r) * 3
            sl = slice(r * tq, (r + 1) * tq)
            comb = (gates[col:col + 1, :] * o_c[rows_g, sl] + gates[col + 1:col + 2, :] * o_s[rows_g, sl]
                    + gates[col + 2:col + 3, :] * o_w[rows_g, sl])
            o_ref[0, 0, r, rows_g, :] = comb


def _nsa_prompt_pallas(q, q_rot, blocks, kv_sel, kv_win, gt):
    b, t = q.shape[0], q.shape[1]
    tq, tk = NSA_TQ, NSA_TK
    assert t % tq == 0 and t % tk == 0 and t // SEL_BLOCK <= LANES and gt.shape[-1] % 8 == 0
    ncp = t // CMP_STRIDE
    n_qt, n_kt = t // tq, t // tk
    cols = GQA_RATIO * tq

    def group_pad_t(a, dtype):
        a = (a * ATTN_SCALE).reshape(b, n_qt, tq, N_KV_HEADS, GQA_RATIO, HEAD_DIM).transpose(0, 3, 1, 5, 4, 2)
        eye = jnp.eye(N_KV_HEADS, dtype=a.dtype)[None, :, None, :, None, None, None]
        return (a[:, :, :, None] * eye).reshape(b, N_KV_HEADS, n_qt, LANES, cols).astype(dtype)

    def values_t(kv):
        v = kv[:, :, 1].astype(jnp.bfloat16).reshape(b, n_kt, tk, LANES)
        return v.transpose(0, 1, 3, 2)

    qc = group_pad_t(q, jnp.float32)
    qr = group_pad_t(q_rot, jnp.bfloat16)
    blk = jnp.pad(blocks, ((0, 0), (0, ncp - blocks.shape[1]), (0, 0), (0, 0), (0, 0))).reshape(b, ncp, 2, LANES)
    kc = blk[:, :, 0]
    vct = blk[:, :, 1].transpose(0, 2, 1)
    ks = kv_sel[:, :, 0].astype(jnp.bfloat16).reshape(b, t, LANES)
    kw = kv_win[:, :, 0].astype(jnp.bfloat16).reshape(b, t, LANES)
    gtt = gt.astype(jnp.float32).reshape(b, n_qt, tq, gt.shape[-1]).transpose(0, 1, 3, 2)

    q_spec = pl.BlockSpec((1, N_KV_HEADS, 1, LANES, cols), lambda i, j: (i, 0, j, 0, 0))

    def whole(shape):
        nd = len(shape)
        return pl.BlockSpec((1,) + shape, lambda i, j: (i,) + (0,) * nd)

    out = pl.pallas_call(
        functools.partial(_nsa_prompt_kernel, seq=t),
        grid=(b, n_qt),
        in_specs=[q_spec, q_spec, whole((ncp, LANES)), whole((LANES, ncp)), whole((t, LANES)),
                  whole((n_kt, LANES, tk)), whole((t, LANES)), whole((n_kt, LANES, tk)),
                  pl.BlockSpec((1, 1, gt.shape[-1], tq), lambda i, j: (i, j, 0, 0))],
        out_specs=pl.BlockSpec((1, 1, GQA_RATIO, LANES, tq), lambda i, j: (i, j, 0, 0, 0)),
        out_shape=jax.ShapeDtypeStruct((b, n_qt, GQA_RATIO, LANES, tq), jnp.float32),
        scratch_shapes=[pltpu.VMEM((1, cols), jnp.float32), pltpu.VMEM((1, cols), jnp.float32),
                        pltpu.VMEM((LANES, cols), jnp.float32)],
        compiler_params=pltpu.CompilerParams(dimension_semantics=("arbitrary", "arbitrary")),
        name="nsa_prompt",
    )(qc, qr, kc, vct, ks, values_t(kv_sel), kw, values_t(kv_win), gtt)
    out = out.reshape(b, n_qt, GQA_RATIO, N_KV_HEADS, HEAD_DIM, tq).transpose(0, 1, 5, 3, 2, 4)
    return out.reshape(b, t, NSA_WIDTH)


KV_ROW = 2 * N_KV_HEADS * HEAD_DIM
SEG_W = CMP_STRIDE * KV_ROW
SAMPLE_PG = 32


def _sample_cmp_kernel(pt_ref, *refs, pg, n_pages, s_len):
    page_refs = refs[:pg]
    (new_ref, w1_ref, b1_ref, w2_ref, b2_ref, qct_ref, oc_ref, sel_ref, t_scr, pseg_scr) = refs[pg:]
    j = pl.program_id(1)
    n_steps = n_pages // pg
    seg_per_page = PAGE_SIZE // CMP_STRIDE
    m_rows = pg * seg_per_page
    past = n_pages * PAGE_SIZE
    n_seg = past // CMP_STRIDE
    rows = GQA_RATIO * s_len

    for i in range(pg):
        for c in range(2):
            t_scr[c, i * PAGE_SIZE:(i + 1) * PAGE_SIZE, :] = page_refs[i][0, c * LANES:(c + 1) * LANES, :].T
    for c in range(2):
        xc = jnp.concatenate([t_scr[c, pl.ds(r, m_rows, stride=CMP_STRIDE), :]
                              for r in range(CMP_STRIDE)], axis=1)
        xn = jnp.concatenate([new_ref[0, :, r * KV_ROW + c * LANES:r * KV_ROW + (c + 1) * LANES]
                              for r in range(CMP_STRIDE)], axis=1)
        xc = jnp.concatenate([xc, xn], axis=0).astype(jnp.bfloat16)
        pseg_scr[c, pl.ds(pl.multiple_of(j * m_rows, m_rows), m_rows + 8), :] = jnp.dot(
            xc, w1_ref[c], preferred_element_type=jnp.float32)

    @pl.when(j == n_steps - 1)
    def _():
        kv = []
        for c in range(2):
            acc = jnp.zeros((n_seg, LANES), jnp.float32) + b2_ref[c]
            for g in range(N_KV_HEADS):
                lo = g * N_HALF * CMP_HIDDEN
                hid = (pseg_scr[c, 0:n_seg, lo:lo + CMP_HIDDEN]
                       + pseg_scr[c, 1:n_seg + 1, lo + CMP_HIDDEN:lo + 2 * CMP_HIDDEN] + b1_ref[c])
                hid = jax.nn.gelu(hid)
                acc = acc + jnp.dot(hid.astype(jnp.bfloat16), w2_ref[c, g], preferred_element_type=jnp.float32)
            kv.append(acc)
        k_c, v_c = kv
        n_sb = past // SEL_BLOCK + -(-s_len // SEL_BLOCK)
        width = sel_ref.shape[-1]
        tok = lax.broadcasted_iota(jnp.int32, (n_seg, LANES), 1) % s_len
        blk_end = lax.broadcasted_iota(jnp.int32, (n_seg, LANES), 0) * CMP_STRIDE + (CMP_BLOCK - 1)
        cmask = blk_end <= past + tok
        s = jnp.dot(k_c, qct_ref[0], precision=HIGHEST, preferred_element_type=jnp.float32)
        s = jnp.where(cmask, s, NEG_INF)
        mx = jnp.max(s, axis=0, keepdims=True)
        p_t = jnp.where(cmask, jnp.exp(s - mx), 0.0)
        p_t = p_t / jnp.maximum(jnp.sum(p_t, axis=0, keepdims=True), 1e-30)
        o_c = jnp.dot(p_t.T.astype(jnp.bfloat16), v_c.astype(jnp.bfloat16), preferred_element_type=jnp.float32)
        ci = lax.broadcasted_iota(jnp.int32, (LANES, LANES), 0)
        cj = lax.broadcasted_iota(jnp.int32, (LANES, LANES), 1)
        same = ((ci // rows == cj // rows) & (ci % s_len == cj % s_len)).astype(jnp.float32)
        p_grp = jnp.dot(p_t, same, precision=HIGHEST, preferred_element_type=jnp.float32)
        score = _block_scores(p_grp, width, past + lax.broadcasted_iota(jnp.int32, (width, LANES), 1) % s_len)
        sel = _topk_cols(score, n_sb, min(N_SEL, n_sb)).T
        for g in range(N_KV_HEADS):
            oc_ref[0, g] = o_c[g * rows:(g + 1) * rows]
            sel_ref[0, g] = sel[g * rows:(g + 1) * rows]


def _sample_attn_kernel(pt_ref, *refs, pg, n_pages, s_len):
    page_refs = refs[:pg]
    (qr_ref, sel_ref, tail_ref, win_ref, wnew_ref, oc_ref, gt_ref, o_ref,
     k_scr, v_scr, m_scr, l_scr, acc_scr) = refs[pg:]
    j = pl.program_id(1)
    n_steps = n_pages // pg
    rows = GQA_RATIO * s_len
    keys = pg * PAGE_SIZE
    past = n_pages * PAGE_SIZE
    width = sel_ref.shape[-1]

    @pl.when(j == 0)
    def _():
        m_scr[...] = jnp.full(m_scr.shape, NEG_INF, jnp.float32)
        l_scr[...] = jnp.zeros(l_scr.shape, jnp.float32)
        acc_scr[...] = jnp.zeros(acc_scr.shape, jnp.float32)

    for i in range(pg):
        k_scr[:, i * PAGE_SIZE:(i + 1) * PAGE_SIZE] = page_refs[i][0, 0:LANES, :].astype(jnp.bfloat16)
        v_scr[:, i * PAGE_SIZE:(i + 1) * PAGE_SIZE] = page_refs[i][0, LANES:2 * LANES, :].astype(jnp.bfloat16)
    blk_of_key = (j * keys + lax.broadcasted_iota(jnp.int32, (width, keys), 1)) // SEL_BLOCK
    expand = (blk_of_key == lax.broadcasted_iota(jnp.int32, (width, keys), 0)).astype(jnp.bfloat16)
    for g in range(N_KV_HEADS):
        q = qr_ref[0, g]
        s = jnp.dot(q, k_scr[...], preferred_element_type=jnp.float32)
        mask = jnp.dot(sel_ref[0, g].astype(jnp.bfloat16), expand, preferred_element_type=jnp.float32) > 0.5
        s = jnp.where(mask, s, NEG_INF)
        m_old = m_scr[g]
        m_new = jnp.maximum(m_old, jnp.max(s, axis=-1, keepdims=True))
        alpha = jnp.exp(m_old - m_new)
        p = jnp.where(mask, jnp.exp(s - m_new), 0.0)
        l_scr[g] = alpha * l_scr[g] + jnp.sum(p, axis=-1, keepdims=True)
        acc_scr[g] = alpha * acc_scr[g] + _dot_nt(p.astype(jnp.bfloat16), v_scr[...])
        m_scr[g] = m_new

    @pl.when(j == n_steps - 1)
    def _():
        tok8 = lax.broadcasted_iota(jnp.int32, (rows, 8), 0) % s_len
        new_ok = lax.broadcasted_iota(jnp.int32, (rows, 8), 1) <= tok8
        wb = win_ref.shape[2]
        tokw = lax.broadcasted_iota(jnp.int32, (rows, wb), 0) % s_len
        kpos = past - wb + lax.broadcasted_iota(jnp.int32, (rows, wb), 1)
        win_ok = (kpos > past + tokw - WINDOW) & (kpos >= 0)
        gates = jax.nn.sigmoid(gt_ref[0])
        for g in range(N_KV_HEADS):
            q = qr_ref[0, g]
            tail_sel = sel_ref[0, g][:, past // SEL_BLOCK:past // SEL_BLOCK + 1] > 0.5
            t_mask = new_ok & tail_sel
            s_t = jnp.where(t_mask, _dot_nt(q, tail_ref[0, :, 0:LANES].astype(jnp.bfloat16)), NEG_INF)
            m_old = m_scr[g]
            m_new = jnp.maximum(m_old, jnp.max(s_t, axis=-1, keepdims=True))
            alpha = jnp.exp(m_old - m_new)
            p_t = jnp.where(t_mask, jnp.exp(s_t - m_new), 0.0)
            l_s = alpha * l_scr[g] + jnp.sum(p_t, axis=-1, keepdims=True)
            o_s = (alpha * acc_scr[g] + jnp.dot(p_t.astype(jnp.bfloat16),
                                                tail_ref[0, :, LANES:2 * LANES].astype(jnp.bfloat16),
                                                preferred_element_type=jnp.float32)) / jnp.maximum(l_s, 1e-30)
            s_w = jnp.where(win_ok, jnp.dot(q, win_ref[0, 0:LANES, :].astype(jnp.bfloat16),
                                            preferred_element_type=jnp.float32), NEG_INF)
            s_n = jnp.where(new_ok, _dot_nt(q, wnew_ref[0, :, 0:LANES].astype(jnp.bfloat16)), NEG_INF)
            mw = jnp.maximum(jnp.max(s_w, axis=-1, keepdims=True), jnp.max(s_n, axis=-1, keepdims=True))
            p_w = jnp.where(win_ok, jnp.exp(s_w - mw), 0.0)
            p_n = jnp.where(new_ok, jnp.exp(s_n - mw), 0.0)
            l_w = jnp.sum(p_w, axis=-1, keepdims=True) + jnp.sum(p_n, axis=-1, keepdims=True)
            o_w = (_dot_nt(p_w.astype(jnp.bfloat16), win_ref[0, LANES:2 * LANES, :].astype(jnp.bfloat16))
                   + jnp.dot(p_n.astype(jnp.bfloat16), wnew_ref[0, :, LANES:2 * LANES].astype(jnp.bfloat16),
                             preferred_element_type=jnp.float32)) / jnp.maximum(l_w, 1e-30)
            gl = gates[g]
            o_ref[0, g] = gl[:, 0:1] * oc_ref[0, g] + gl[:, 1:2] * o_s + gl[:, 2:3] * o_w


def _nsa_sample_pallas(q, q_rot, kv_cmp, kv_sel, kv_win, gt, cache_cmp, cache_sel, win_buf, page_table,
                       w1, b1, w2, b2):
    pg = SAMPLE_PG
    db, s_len = q.shape[0], q.shape[1]
    n_pages = page_table.shape[1]
    assert n_pages % pg == 0 and s_len <= 8 and s_len <= CMP_STRIDE and N_NSA_HEADS * s_len <= LANES
    past = n_pages * PAGE_SIZE
    n_seg = past // CMP_STRIDE
    n_sb = past // SEL_BLOCK + 1
    width = -(-n_sb // LANES) * LANES
    rows = GQA_RATIO * s_len
    seg_per_page = PAGE_SIZE // CMP_STRIDE

    def group_rows(a, dtype):
        a = (a * ATTN_SCALE).reshape(db, s_len, N_KV_HEADS, GQA_RATIO, HEAD_DIM).transpose(0, 2, 3, 1, 4)
        eye = jnp.eye(N_KV_HEADS, dtype=a.dtype)[None, :, None, None, :, None]
        return (a[:, :, :, :, None, :] * eye).reshape(db, N_KV_HEADS, rows, LANES).astype(dtype)

    def pad8(a):
        return jnp.pad(a.reshape(db, s_len, KV_ROW), ((0, 0), (0, 8 - s_len), (0, 0)))

    qg = (q * ATTN_SCALE).reshape(db, s_len, N_KV_HEADS, GQA_RATIO, HEAD_DIM).transpose(0, 2, 4, 3, 1)
    eye_q = jnp.eye(N_KV_HEADS, dtype=qg.dtype)[None, :, None, :, None]
    qct = (qg.reshape(db, N_KV_HEADS, HEAD_DIM, 1, rows) * eye_q).reshape(db, LANES, N_KV_HEADS * rows)
    qct = jnp.pad(qct, ((0, 0), (0, 0), (0, LANES - N_KV_HEADS * rows)))
    qr = group_rows(q_rot, jnp.bfloat16)
    new_seg = jnp.pad(kv_cmp.reshape(db, 1, s_len * KV_ROW), ((0, 0), (0, 7), (0, SEG_W - s_len * KV_ROW)))
    eye_g = jnp.eye(N_KV_HEADS, dtype=w1.dtype)
    w1t = w1.transpose(0, 2, 3, 1, 4)
    w1_bd = (w1t[:, :, None, :, None, :, :] * eye_g[None, None, :, None, :, None, None]).reshape(
        2, CMP_STRIDE * N_KV_HEADS * HEAD_DIM, N_KV_HEADS * N_HALF * CMP_HIDDEN).astype(jnp.bfloat16)
    w2_g = (w2[:, None, :, None, :] * eye_g[None, :, None, :, None]).reshape(
        2, N_KV_HEADS, CMP_HIDDEN, LANES).astype(jnp.bfloat16)
    b2_t = jnp.tile(b2, (1, N_KV_HEADS)).reshape(2, 1, LANES)
    b1_r = b1.reshape(2, 1, CMP_HIDDEN)

    def page_specs(block):
        return [pl.BlockSpec(block, functools.partial(lambda b, j, pt, i: (pt[b, j * pg + i], 0, 0), i=i))
                for i in range(pg)]

    def per_seq(shape):
        nd = len(shape)
        return pl.BlockSpec((1,) + shape, lambda b, j, pt: (b,) + (0,) * nd)

    def const(shape):
        nd = len(shape)
        return pl.BlockSpec(shape, lambda b, j, pt: (0,) * nd)

    def feature_major(a):
        return a.transpose(0, 2, 3, 4, 1).reshape(a.shape[0], KV_ROW, a.shape[1])

    page_block = (1, KV_ROW, PAGE_SIZE)
    o_c, sel = pl.pallas_call(
        functools.partial(_sample_cmp_kernel, pg=pg, n_pages=n_pages, s_len=s_len),
        grid_spec=pltpu.PrefetchScalarGridSpec(
            num_scalar_prefetch=1, grid=(db, n_pages // pg),
            in_specs=page_specs(page_block) + [
                per_seq((8, SEG_W)), const(w1_bd.shape), const(b1_r.shape), const(w2_g.shape), const(b2_t.shape),
                per_seq((LANES, LANES))],
            out_specs=[per_seq((N_KV_HEADS, rows, LANES)), per_seq((N_KV_HEADS, rows, width))],
            scratch_shapes=[pltpu.VMEM((2, pg * PAGE_SIZE, LANES), jnp.float32),
                            pltpu.VMEM((2, n_seg + 8, N_KV_HEADS * N_HALF * CMP_HIDDEN), jnp.float32)]),
        out_shape=[jax.ShapeDtypeStruct((db, N_KV_HEADS, rows, LANES), jnp.float32),
                   jax.ShapeDtypeStruct((db, N_KV_HEADS, rows, width), jnp.float32)],
        compiler_params=pltpu.CompilerParams(dimension_semantics=("arbitrary", "arbitrary"),
                                             vmem_limit_bytes=56 * 1024 * 1024),
        name="sample_cmp_select",
    )(page_table, *([feature_major(cache_cmp)] * pg), new_seg, w1_bd, b1_r, w2_g, b2_t, qct)

    gl = jnp.pad(gt.astype(jnp.float32).reshape(db, s_len, N_KV_HEADS, GQA_RATIO, 3).transpose(0, 2, 3, 1, 4)
                 .reshape(db, N_KV_HEADS, rows, 3), ((0, 0), (0, 0), (0, 0), (0, LANES - 3)))
    wb = win_buf.shape[1]
    out = pl.pallas_call(
        functools.partial(_sample_attn_kernel, pg=pg, n_pages=n_pages, s_len=s_len),
        grid_spec=pltpu.PrefetchScalarGridSpec(
            num_scalar_prefetch=1, grid=(db, n_pages // pg),
            in_specs=page_specs(page_block) + [
                per_seq((N_KV_HEADS, rows, LANES)), per_seq((N_KV_HEADS, rows, width)), per_seq((8, KV_ROW)),
                per_seq((KV_ROW, wb)), per_seq((8, KV_ROW)), per_seq((N_KV_HEADS, rows, LANES)),
                per_seq((N_KV_HEADS, rows, LANES))],
            out_specs=per_seq((N_KV_HEADS, rows, LANES)),
            scratch_shapes=[pltpu.VMEM((LANES, pg * PAGE_SIZE), jnp.bfloat16),
                            pltpu.VMEM((LANES, pg * PAGE_SIZE), jnp.bfloat16),
                            pltpu.VMEM((N_KV_HEADS, rows, 1), jnp.float32),
                            pltpu.VMEM((N_KV_HEADS, rows, 1), jnp.float32),
                            pltpu.VMEM((N_KV_HEADS, rows, LANES), jnp.float32)]),
        out_shape=jax.ShapeDtypeStruct((db, N_KV_HEADS, rows, LANES), jnp.float32),
        compiler_params=pltpu.CompilerParams(dimension_semantics=("arbitrary", "arbitrary")),
        name="sample_sel_win_attn",
    )(page_table, *([feature_major(cache_sel)] * pg), qr, sel, pad8(kv_sel),
      feature_major(win_buf), pad8(kv_win), o_c, gl)
    out = out.reshape(db, N_KV_HEADS, GQA_RATIO, s_len, N_KV_HEADS, HEAD_DIM)
    out = jnp.stack([out[:, g, :, :, g, :] for g in range(N_KV_HEADS)], axis=1)
    return out.transpose(0, 3, 1, 2, 4).reshape(db, s_len, NSA_WIDTH)


ROUTER_TT = 256
MOE_BM = 256


def _router_kernel(x_ref, mix_ref, wout_ref, g_ref, wrt_ref, br_ref,
                   x1_ref, h_ref, eidx_ref, gate_ref, rank_ref, cnt_ref, wout_bf, run_scr):
    tt = x_ref.shape[0]
    n_e = wrt_ref.shape[0]

    @pl.when(pl.program_id(0) == 0)
    def _():
        wout_bf[...] = wout_ref[...].astype(jnp.bfloat16)
        run_scr[...] = jnp.zeros_like(run_scr)

    x1 = x_ref[...] + jnp.dot(mix_ref[...].astype(jnp.bfloat16), wout_bf[...], preferred_element_type=jnp.float32)
    x1_ref[...] = x1
    hn = x1 * lax.rsqrt(jnp.mean(x1 * x1, axis=-1, keepdims=True) + RMS_EPS) * g_ref[...]
    h_ref[...] = hn

    score = _dot_nt(wrt_ref[...], hn, HIGHEST) + br_ref[...]
    ef = lax.broadcasted_iota(jnp.int32, (n_e, tt), 0).astype(jnp.float32)
    vals, hits = [], []
    for k in range(TOP_K):
        best = jnp.max(score, axis=0, keepdims=True)
        first = jnp.min(jnp.where(score == best, ef, float(n_e)), axis=0, keepdims=True)
        hit = ef == first
        vals.append(best)
        hits.append(hit)
        eidx_ref[k:k + 1, :] = first.astype(jnp.int32)
        score = jnp.where(hit, -3e38, score)
    exps = [jnp.exp(v - vals[0]) for v in vals]
    denom = sum(exps[1:], exps[0])
    for k in range(TOP_K):
        gate_ref[k:k + 1, :] = exps[k] / denom

    chosen = functools.reduce(jnp.logical_or, hits)
    before = (lax.broadcasted_iota(jnp.int32, (tt, tt), 0) < lax.broadcasted_iota(jnp.int32, (tt, tt), 1))
    earlier = jnp.dot(chosen.astype(jnp.bfloat16), before.astype(jnp.bfloat16), preferred_element_type=jnp.float32)
    pos = earlier + run_scr[...]
    for k in range(TOP_K):
        rank_ref[k:k + 1, :] = jnp.sum(jnp.where(hits[k], pos, 0.0), axis=0, keepdims=True).astype(jnp.int32)
    run_scr[...] = run_scr[...] + jnp.sum(chosen.astype(jnp.float32), axis=1, keepdims=True)
    cnt_ref[...] = jnp.broadcast_to(run_scr[...], cnt_ref.shape).astype(jnp.int32)


def _router_pallas(x, mix, w_out, g_ffn, w_router, b_router):
    n, d = x.shape
    c = mix.shape[1]
    n_e = w_router.shape[1]
    tt = ROUTER_TT
    assert n % tt == 0

    def row(w):
        return pl.BlockSpec((tt, w), lambda i: (i, 0))

    def full(a, b):
        return pl.BlockSpec((a, b), lambda i: (0, 0))

    k4 = pl.BlockSpec((TOP_K, tt), lambda i: (0, i))
    return pl.pallas_call(
        _router_kernel,
        grid=(n // tt,),
        in_specs=[row(d), row(c), full(c, d), full(1, d), full(n_e, d), full(n_e, 1)],
        out_specs=[row(d), row(d), k4, k4, k4, full(n_e, LANES)],
        out_shape=[jax.ShapeDtypeStruct((n, d), jnp.float32), jax.ShapeDtypeStruct((n, d), jnp.float32),
                   jax.ShapeDtypeStruct((TOP_K, n), jnp.int32), jax.ShapeDtypeStruct((TOP_K, n), jnp.float32),
                   jax.ShapeDtypeStruct((TOP_K, n), jnp.int32), jax.ShapeDtypeStruct((n_e, LANES), jnp.int32)],
        scratch_shapes=[pltpu.VMEM((c, d), jnp.bfloat16), pltpu.VMEM((n_e, 1), jnp.float32)],
        compiler_params=pltpu.CompilerParams(dimension_semantics=("arbitrary",)),
        name="outproj_router",
    )(x, mix, w_out, g_ffn.reshape(1, d), w_router.T, b_router.reshape(n_e, 1))


def _expert_kernel(blk_e_ref, n_used_ref, x_ref, wup_ref, bup_ref, wdn_ref, bdn_ref, o_ref, wup_bf, wdn_bf):
    i = pl.program_id(0)
    d_ff = wdn_ref.shape[1]

    @pl.when(i < n_used_ref[0])
    def _():
        e = blk_e_ref[i]
        prev = blk_e_ref[jnp.maximum(i - 1, 0)]

        @pl.when((i == 0) | (e != prev))
        def _():
            wup_bf[...] = wup_ref[0].astype(jnp.bfloat16)
            wdn_bf[...] = wdn_ref[0].astype(jnp.bfloat16)

        u = jnp.dot(x_ref[...].astype(jnp.bfloat16), wup_bf[...], preferred_element_type=jnp.float32) + bup_ref[0]
        glu = jnp.minimum(u[:, :d_ff], SWIGLU_LIMIT)
        lin = jnp.clip(u[:, d_ff:], -SWIGLU_LIMIT, SWIGLU_LIMIT)
        a = glu * jax.nn.sigmoid(SWIGLU_ALPHA * glu) * (lin + 1.0)
        o_ref[...] = jnp.dot(a.astype(jnp.bfloat16), wdn_bf[...], preferred_element_type=jnp.float32) + bdn_ref[0]

    @pl.when(i >= n_used_ref[0])
    def _():
        o_ref[...] = jnp.zeros_like(o_ref)


def _expert_vmem_bytes(bm, d, f2, d_ff):
    weights = 2 * 4 * (d * f2 + d_ff * d) + 2 * (d * f2 + d_ff * d)
    rows = 2 * bm * d * (4 + 4)
    temps = bm * f2 * 4 * 2 + bm * d_ff * (4 + 2)
    return weights + rows + temps


def _experts_pallas(xs, blk_e, n_used, w_up, b_up, w_down, b_down, bm):
    n_slots, d = xs.shape
    n_e, _, f2 = w_up.shape
    d_ff = w_down.shape[1]
    grid_spec = pltpu.PrefetchScalarGridSpec(
        num_scalar_prefetch=2,
        grid=(n_slots // bm,),
        in_specs=[pl.BlockSpec((bm, d), lambda i, be, nu: (i, 0)),
                  pl.BlockSpec((1, d, f2), lambda i, be, nu: (be[i], 0, 0)),
                  pl.BlockSpec((1, 1, f2), lambda i, be, nu: (be[i], 0, 0)),
                  pl.BlockSpec((1, d_ff, d), lambda i, be, nu: (be[i], 0, 0)),
                  pl.BlockSpec((1, 1, d), lambda i, be, nu: (be[i], 0, 0))],
        out_specs=pl.BlockSpec((bm, d), lambda i, be, nu: (i, 0)),
        scratch_shapes=[pltpu.VMEM((d, f2), jnp.bfloat16), pltpu.VMEM((d_ff, d), jnp.bfloat16)],
    )
    vmem_limit = _expert_vmem_bytes(bm, d, f2, d_ff) * 5 // 4
    return pl.pallas_call(
        _expert_kernel,
        grid_spec=grid_spec,
        out_shape=jax.ShapeDtypeStruct((n_slots, d), jnp.float32),
        compiler_params=pltpu.CompilerParams(dimension_semantics=("arbitrary",), vmem_limit_bytes=vmem_limit),
        name="expert_mlp",
    )(blk_e, n_used, xs, w_up, b_up.reshape(n_e, 1, f2), w_down, b_down.reshape(n_e, 1, d))


def _combine_norm_kernel(x1_ref, og_ref, gate_ref, g_ref, o_ref):
    y = x1_ref[...]
    gates = gate_ref[...]
    for k in range(TOP_K):
        y = y + gates[:, k:k + 1] * og_ref[k]
    o_ref[...] = y * lax.rsqrt(jnp.mean(y * y, axis=-1, keepdims=True) + RMS_EPS) * g_ref[...]


def _combine_norm_pallas(x1, og, gate_t, g_final):
    n, d = x1.shape
    tt = ROUTER_TT
    return pl.pallas_call(
        _combine_norm_kernel,
        grid=(n // tt,),
        in_specs=[pl.BlockSpec((tt, d), lambda i: (i, 0)), pl.BlockSpec((TOP_K, tt, d), lambda i: (0, i, 0)),
                  pl.BlockSpec((tt, TOP_K), lambda i: (i, 0)), pl.BlockSpec((1, d), lambda i: (0, 0))],
        out_specs=pl.BlockSpec((tt, d), lambda i: (i, 0)),
        out_shape=jax.ShapeDtypeStruct((n, d), jnp.float32),
        compiler_params=pltpu.CompilerParams(dimension_semantics=("arbitrary",)),
        name="combine_final_norm",
    )(x1, og, gate_t, g_final.reshape(1, d))


def _finish_pallas(x, mix, w_out, g_ffn, w_router, b_router, w_up, b_up, w_down, b_down, g_final):
    n, d = x.shape
    n_e = w_router.shape[1]
    bm = MOE_BM
    x1, h, eidx, gate, rank, cnt = _router_pallas(x, mix, w_out, g_ffn, w_router, b_router)
    counts = cnt[:, 0]
    padded = (counts + bm - 1) // bm * bm
    pad_end = jnp.cumsum(padded)
    gstart = pad_end - padded
    pick = eidx[None] == jnp.arange(n_e, dtype=jnp.int32)[:, None, None]
    dest = jnp.sum(jnp.where(pick, gstart[:, None, None], 0), axis=0) + rank
    nb = -(-(n * TOP_K) // bm) + n_e
    blk_start = jnp.arange(nb, dtype=jnp.int32) * bm
    blk_e = jnp.minimum(jnp.sum((pad_end[None, :] <= blk_start[:, None]).astype(jnp.int32), axis=1), n_e - 1)
    n_used = (pad_end[-1] // bm).astype(jnp.int32).reshape(1)
    tok = jnp.broadcast_to(jnp.arange(n, dtype=jnp.int32)[None, :], (TOP_K, n))
    src = jnp.zeros((nb * bm,), jnp.int32).at[dest.reshape(-1)].set(tok.reshape(-1))
    out = _experts_pallas(h[src], blk_e, n_used, w_up, b_up, w_down, b_down, bm)
    return _combine_norm_pallas(x1, out[dest], gate.T, g_final)


def kernel(x_prompt, x_sample, cache_cmp_kv, cache_sel_kv, cache_win_kv, state_ret, page_table, g_attn, w_in, w_cmp1, b_cmp1, w_cmp2, b_cmp2, w_out, g_ffn, w_router, b_router, w_up, b_up, w_down, b_down, g_final):
    seq = x_prompt.shape[1]
    past = page_table.shape[1] * PAGE_SIZE
    pos_p = jnp.arange(seq, dtype=jnp.int32)
    pos_s = past + jnp.arange(x_sample.shape[1], dtype=jnp.int32)
    log_g = _ret_log_decay()
    assert DEPTH == 1
    l = 0
    d = x_prompt.shape[-1]
    q, kv_cmp_p, kv_sel_p, kv_win_p, gt, r_q, r_k, r_v, r_g = _mixer_inputs(x_prompt, pos_p, g_attn[l], w_in[l])
    blocks = _compress_blocks(_segment_proj(kv_cmp_p, w_cmp1[l]), b_cmp1[l], w_cmp2[l], b_cmp2[l])
    o_nsa = _nsa_prompt_pallas(q, _rope(q, pos_p), blocks, kv_sel_p, kv_win_p, gt)
    ret_p, o_r = _ret_prompt(r_q, r_k, r_v, log_g)
    mix_p = jnp.concatenate([o_nsa, _ret_out(o_r, r_g)], axis=-1)
    q, kv_cmp_s, kv_sel_s, kv_win_s, gt, r_q, r_k, r_v, r_g = _mixer_inputs(x_sample, pos_s, g_attn[l], w_in[l])
    o_nsa_s = _nsa_sample_pallas(q, _rope(q, pos_s), kv_cmp_s, kv_sel_s, kv_win_s, gt, cache_cmp_kv[:, l],
                                 cache_sel_kv[:, l], cache_win_kv[:, l], page_table,
                                 w_cmp1[l], b_cmp1[l], w_cmp2[l], b_cmp2[l])
    win_s = jnp.concatenate([cache_win_kv[:, l], kv_win_s], axis=1)[:, x_sample.shape[1]:]
    ret_s, o_r = _ret_chunk(state_ret[:, l].astype(jnp.float32), r_q.astype(jnp.float32), r_k.astype(jnp.float32), r_v.astype(jnp.float32), log_g)
    mix_s = jnp.concatenate([o_nsa_s, _ret_out(o_r, r_g)], axis=-1)
    n_p = x_prompt.shape[0] * seq
    x_all = jnp.concatenate([x_prompt.reshape(n_p, d), x_sample.reshape(-1, d)], axis=0)
    mix_all = jnp.concatenate([mix_p.reshape(n_p, MIX_WIDTH), mix_s.reshape(-1, MIX_WIDTH)], axis=0)
    y = _finish_pallas(x_all, mix_all, w_out[l], g_ffn[l], w_router[l], b_router[l], w_up[l], b_up[l], w_down[l], b_down[l], g_final)
    y_prompt = y[:n_p].reshape(x_prompt.shape)
    y_sample = y[n_p:].reshape(x_sample.shape)
    win_p = kv_win_p[:, seq - min(WINDOW, seq):]
    return (y_prompt, y_sample, kv_cmp_p[:, None], kv_sel_p[:, None], win_p[:, None], ret_p[:, None],
            kv_cmp_s[:, None], kv_sel_s[:, None], win_s[:, None], ret_s[:, None])
```

```python
import functools
import jax, jax.numpy as jnp
from jax import lax
import numpy as np
from jax.experimental import pallas as pl
from jax.experimental.pallas import tpu as pltpu

DEPTH = 1
PAGE_SIZE = 128

HEAD_DIM = 64
N_NSA_HEADS = 8
N_KV_HEADS = 2
GQA_RATIO = N_NSA_HEADS // N_KV_HEADS
CMP_BLOCK = 32
CMP_STRIDE = 16
N_HALF = CMP_BLOCK // CMP_STRIDE
CMP_HIDDEN = 256
SEL_BLOCK = 64
N_SEL = 16
WINDOW = 512
N_RET_HEADS = 4
RET_DK = 64
RET_DV = 128
RET_CHUNK = 128
TOP_K = 4
SWIGLU_ALPHA = 1.702
SWIGLU_LIMIT = 7.0
ROPE_THETA = 10000.0
RMS_EPS = 1e-5
NEG_INF = -1e30
FORCED_SCORE = 1e6
INVALID_SCORE = -1e9
NSA_WIDTH = N_NSA_HEADS * HEAD_DIM
KV_WIDTH = N_KV_HEADS * HEAD_DIM
RET_QK_WIDTH = N_RET_HEADS * RET_DK
RET_WIDTH = N_RET_HEADS * RET_DV
IN_SIZES = (NSA_WIDTH, KV_WIDTH, KV_WIDTH, KV_WIDTH, KV_WIDTH, KV_WIDTH, KV_WIDTH, 3 * N_NSA_HEADS, RET_QK_WIDTH, RET_QK_WIDTH, RET_WIDTH, RET_WIDTH)
MIX_WIDTH = NSA_WIDTH + RET_WIDTH
ATTN_SCALE = HEAD_DIM ** -0.5

LANES = 128
HIGHEST = lax.Precision.HIGHEST


def _rmsnorm(x, g):
    xf = x.astype(jnp.float32)
    y = xf * lax.rsqrt(jnp.mean(xf * xf, axis=-1, keepdims=True) + RMS_EPS)
    return (y * g.astype(jnp.float32)).astype(x.dtype)


def _rope(x, pos):
    half = x.shape[-1] // 2
    inv = ROPE_THETA ** (-jnp.arange(half, dtype=jnp.float32) / half)
    ang = pos.astype(jnp.float32)[:, None] * inv[None, :]
    cos = jnp.cos(ang)[None, :, None, :]
    sin = jnp.sin(ang)[None, :, None, :]
    xf = x.astype(jnp.float32)
    x1, x2 = xf[..., :half], xf[..., half:]
    return jnp.concatenate([x1 * cos - x2 * sin, x2 * cos + x1 * sin], axis=-1).astype(x.dtype)


def _mixer_inputs(x, pos, g_attn, w_in):
    b, t = x.shape[0], x.shape[1]
    h = _rmsnorm(x, g_attn)
    proj = jnp.einsum('btd,dc->btc', h, w_in)
    offs = [int(o) for o in np.cumsum(np.array(IN_SIZES))[:-1]]
    q, kc, vc, ks, vs, kw, vw, gt, rq, rk, rv, rg = jnp.split(proj, offs, axis=-1)

    def heads(a, n, d):
        return a.reshape(b, t, n, d)

    def kvh(a):
        return heads(a, N_KV_HEADS, HEAD_DIM)

    kv_cmp = jnp.stack([kvh(kc), kvh(vc)], axis=2)
    kv_sel = jnp.stack([_rope(kvh(ks), pos), kvh(vs)], axis=2)
    kv_win = jnp.stack([_rope(kvh(kw), pos), kvh(vw)], axis=2)
    r_q = _rope(heads(rq, N_RET_HEADS, RET_DK), pos)
    r_k = _rope(heads(rk, N_RET_HEADS, RET_DK), pos) * (RET_DK ** -0.5)
    r_v = heads(rv, N_RET_HEADS, RET_DV)
    return heads(q, N_NSA_HEADS, HEAD_DIM), kv_cmp, kv_sel, kv_win, gt, r_q, r_k, r_v, rg


def _segment_proj(kv_rows, w1):
    b, l = kv_rows.shape[0], kv_rows.shape[1]
    seg = kv_rows.reshape(b, l // CMP_STRIDE, CMP_STRIDE, 2, N_KV_HEADS, HEAD_DIM)
    return jnp.einsum('bsrcgd,chrdk->bscghk', seg, w1)


def _compress_blocks(p_seg, b1, w2, b2):
    n_cmp = p_seg.shape[1] - N_HALF + 1
    hid = sum(p_seg[:, h:h + n_cmp, :, :, h, :] for h in range(N_HALF)) + b1[None, None, :, None, :]
    hid = jax.nn.gelu(hid.astype(jnp.float32))
    out = jnp.einsum('bncgk,ckd->bncgd', hid, w2.astype(jnp.float32))
    return out + b2.astype(jnp.float32)[None, None, :, None, :]


def _ret_log_decay():
    return jnp.log1p(-jnp.exp2(-5.0 - jnp.arange(N_RET_HEADS, dtype=jnp.float32)))


def _ret_chunk(state, q, k, v, log_g):
    c = q.shape[1]
    i = jnp.arange(c, dtype=jnp.float32)
    diff = i[:, None] - i[None, :]
    decay = jnp.where(diff[None] >= 0, jnp.exp(log_g[:, None, None] * jnp.maximum(diff, 0.0)[None]), 0.0)
    scores = jnp.einsum('bihd,bjhd->bhij', q, k) * decay[None]
    inner = jnp.einsum('bhij,bjhv->bihv', scores, v)
    q_dec = jnp.exp(log_g[None, :] * (i[:, None] + 1.0))
    cross = jnp.einsum('bihd,bhdv->bihv', q * q_dec[None, :, :, None], state)
    k_dec = jnp.exp(log_g[None, :] * (c - 1.0 - i[:, None]))
    new_state = jnp.exp(log_g * c)[None, :, None, None] * state + jnp.einsum('bjhd,bjhv->bhdv', k * k_dec[None, :, :, None], v)
    return new_state, inner + cross


def _ret_prompt_kernel(lg_ref, q_ref, k_ref, kt_ref, v_ref, rg_ref, o_ref, s_ref, state):
    c = q_ref.shape[2]

    @pl.when(pl.program_id(1) == 0)
    def _():
        state[...] = jnp.zeros_like(state)

    ii = lax.broadcasted_iota(jnp.int32, (c, c), 0)
    jj = lax.broadcasted_iota(jnp.int32, (c, c), 1)
    diff = (ii - jj).astype(jnp.float32)
    pos_col = lax.broadcasted_iota(jnp.int32, (c, 1), 0).astype(jnp.float32)
    pos_row = lax.broadcasted_iota(jnp.int32, (1, c), 1).astype(jnp.float32)
    for h in range(N_RET_HEADS):
        lg = lg_ref[h]
        q, k, v = q_ref[0, h], k_ref[0, h], v_ref[0, h].astype(jnp.bfloat16)
        decay = jnp.where(diff >= 0, jnp.exp(lg * jnp.maximum(diff, 0.0)), 0.0)
        scores = _dot_nt(q.astype(jnp.bfloat16), k.astype(jnp.bfloat16)) * decay
        inner = jnp.dot(scores.astype(jnp.bfloat16), v, preferred_element_type=jnp.float32)
        q_dec = q * jnp.exp(lg * (pos_col + 1.0))
        cross = jnp.dot(q_dec.astype(jnp.bfloat16), state[h].astype(jnp.bfloat16), preferred_element_type=jnp.float32)
        kt_dec = kt_ref[0, h] * jnp.exp(lg * (c - 1.0 - pos_row))
        state[h] = jnp.exp(lg * c) * state[h] + jnp.dot(kt_dec.astype(jnp.bfloat16), v,
                                                        preferred_element_type=jnp.float32)
        o = inner + cross
        o = o * lax.rsqrt(jnp.mean(o * o, axis=-1, keepdims=True) + RMS_EPS)
        o_ref[0, :, h * RET_DV:(h + 1) * RET_DV] = o * jax.nn.silu(rg_ref[0, :, h * RET_DV:(h + 1) * RET_DV])
    s_ref[0] = state[...]


def _ret_prompt_pallas(r_q, r_k, r_v, r_g, log_g):
    b, t = r_q.shape[0], r_q.shape[1]
    c = RET_CHUNK
    assert t % c == 0
    qh = r_q.transpose(0, 2, 1, 3)
    kh = r_k.transpose(0, 2, 1, 3)
    kt = r_k.transpose(0, 2, 3, 1)
    vh = r_v.transpose(0, 2, 1, 3)

    def chunk(d):
        return pl.BlockSpec((1, N_RET_HEADS, c, d), lambda i, j, lg: (i, 0, j, 0))

    return pl.pallas_call(
        _ret_prompt_kernel,
        grid_spec=pltpu.PrefetchScalarGridSpec(
            num_scalar_prefetch=1, grid=(b, t // c),
            in_specs=[chunk(RET_DK), chunk(RET_DK),
                      pl.BlockSpec((1, N_RET_HEADS, RET_DK, c), lambda i, j, lg: (i, 0, 0, j)),
                      chunk(RET_DV), pl.BlockSpec((1, c, RET_WIDTH), lambda i, j, lg: (i, j, 0))],
            out_specs=[pl.BlockSpec((1, c, RET_WIDTH), lambda i, j, lg: (i, j, 0)),
                       pl.BlockSpec((1, N_RET_HEADS, RET_DK, RET_DV), lambda i, j, lg: (i, 0, 0, 0))],
            scratch_shapes=[pltpu.VMEM((N_RET_HEADS, RET_DK, RET_DV), jnp.float32)]),
        out_shape=[jax.ShapeDtypeStruct((b, t, RET_WIDTH), jnp.float32),
                   jax.ShapeDtypeStruct((b, N_RET_HEADS, RET_DK, RET_DV), jnp.float32)],
        compiler_params=pltpu.CompilerParams(dimension_semantics=("arbitrary", "arbitrary")),
        name="retention_prompt",
    )(log_g, qh, kh, kt, vh, r_g)


def _ret_out(o, r_g):
    b, t = o.shape[0], o.shape[1]
    o = o * lax.rsqrt(jnp.mean(o * o, axis=-1, keepdims=True) + RMS_EPS)
    return o.reshape(b, t, RET_WIDTH) * jax.nn.silu(r_g.astype(jnp.float32))


def _dot_nt(a, b, precision=None):
    return lax.dot_general(a, b, (((1,), (1,)), ((), ())), precision=precision, preferred_element_type=jnp.float32)


def _topk_cols(score, n_valid, k):
    height, width = score.shape
    jf = lax.broadcasted_iota(jnp.int32, (height, width), 0).astype(jnp.float32)
    score = jnp.where(jf < n_valid, score, -3e38)
    sel = jnp.zeros((height, width), jnp.float32)
    for _ in range(k):
        best = jnp.max(score, axis=0, keepdims=True)
        first = jnp.min(jnp.where(score == best, jf, float(height)), axis=0, keepdims=True)
        hit = jf == first
        sel = jnp.where(hit, 1.0, sel)
        score = jnp.where(hit, -3e38, score)
    return sel


def _block_scores(p_grp, n_blocks, q_pos):
    n_cmp = p_grp.shape[0]
    ratio = SEL_BLOCK // CMP_STRIDE
    jj = lax.broadcasted_iota(jnp.int32, (n_blocks, n_cmp), 0)
    nn = lax.broadcasted_iota(jnp.int32, (n_blocks, n_cmp), 1)
    overlap = ((nn >= ratio * jj - (N_HALF - 1)) & (nn <= ratio * jj + ratio - 1)).astype(jnp.float32)
    score = jnp.dot(overlap, p_grp, precision=HIGHEST, preferred_element_type=jnp.float32)
    jb = lax.broadcasted_iota(jnp.int32, q_pos.shape, 0)
    cur = q_pos // SEL_BLOCK
    forced = (jb == 0) | (jb == cur) | (jb == cur - 1)
    valid = jb * SEL_BLOCK <= q_pos
    score = jnp.where(forced, FORCED_SCORE, score)
    return jnp.where(valid, score, INVALID_SCORE)


NSA_TQ = 128
NSA_TK = 256


def _nsa_prompt_kernel(qc_ref, qr_ref, kc_ref, vct_ref, ks_ref, vst_ref, kw_ref, vwt_ref, gt_ref, o_ref,
                       m_scr, l_scr, acc_scr, oc_scr, *, seq):
    tq, tk = NSA_TQ, NSA_TK
    cols = GQA_RATIO * tq
    qt = pl.program_id(1)
    q0 = qt * tq
    ncp = kc_ref.shape[1]
    n_sb = seq // SEL_BLOCK
    gates = jax.nn.sigmoid(gt_ref[0, 0])

    def heads(a):
        return jnp.concatenate([a] * GQA_RATIO, axis=1)

    kpos_i = lax.broadcasted_iota(jnp.int32, (tk, tq), 0)
    qpos = q0 + lax.broadcasted_iota(jnp.int32, (tk, tq), 1)

    def attend(k_ref, vt_ref, lo, hi, keep_fn):
        m_scr[...] = jnp.full(m_scr.shape, NEG_INF, jnp.float32)
        l_scr[...] = jnp.zeros(l_scr.shape, jnp.float32)
        acc_scr[...] = jnp.zeros(acc_scr.shape, jnp.float32)

        def body(kt, carry):
            k0 = pl.multiple_of(kt * tk, tk)
            k = k_ref[0, pl.ds(k0, tk), :]
            vt = vt_ref[0, kt]
            keeps = keep_fn(k0)
            for g in range(N_KV_HEADS):
                keep = keeps[g]
                s = jnp.dot(k, qr_ref[0, g, 0], preferred_element_type=jnp.float32)
                s = s + heads((keep - 1.0) * -NEG_INF)
                m_old = m_scr[g]
                m_new = jnp.maximum(m_old, jnp.max(s, axis=0, keepdims=True))
                alpha = jnp.exp(m_old - m_new)
                p = jnp.exp(s - m_new) * heads(keep)
                l_scr[g] = alpha * l_scr[g] + jnp.sum(p, axis=0, keepdims=True)
                acc_scr[g] = alpha * acc_scr[g] + jnp.dot(vt, p.astype(jnp.bfloat16),
                                                          preferred_element_type=jnp.float32)
                m_scr[g] = m_new
            return carry

        lax.fori_loop(lo, hi, body, 0)
        return [acc_scr[g, g * HEAD_DIM:(g + 1) * HEAD_DIM, :] / jnp.maximum(l_scr[g], 1e-30)
                for g in range(N_KV_HEADS)]

    sel_bf = []
    for g in range(N_KV_HEADS):
        s = jnp.dot(kc_ref[0], qc_ref[0, g, 0], precision=HIGHEST, preferred_element_type=jnp.float32)
        blk_end = lax.broadcasted_iota(jnp.int32, (ncp, tq), 0) * CMP_STRIDE + (CMP_BLOCK - 1)
        ckeep = heads(jnp.where(blk_end <= q0 + lax.broadcasted_iota(jnp.int32, (ncp, tq), 1), 1.0, 0.0)) > 0.5
        s = jnp.where(ckeep, s, NEG_INF)
        mx = jnp.max(s, axis=0, keepdims=True)
        p = jnp.where(ckeep, jnp.exp(s - mx), 0.0)
        p = p / jnp.maximum(jnp.sum(p, axis=0, keepdims=True), 1e-30)
        o_c = jnp.dot(vct_ref[0].astype(jnp.bfloat16), p.astype(jnp.bfloat16),
                      preferred_element_type=jnp.float32)
        p_grp = sum(p[:, r * tq:(r + 1) * tq] for r in range(GQA_RATIO))

        score = _block_scores(p_grp, n_sb, q0 + lax.broadcasted_iota(jnp.int32, (n_sb, tq), 1))
        sel_t = _topk_cols(score, n_sb, min(N_SEL, n_sb))
        if n_sb < LANES:
            sel_t = jnp.concatenate([sel_t, jnp.zeros((LANES - n_sb, tq), jnp.float32)], axis=0)
        sel_bf.append(sel_t.astype(jnp.bfloat16))
        oc_scr[g * HEAD_DIM:(g + 1) * HEAD_DIM, :] = o_c[g * HEAD_DIM:(g + 1) * HEAD_DIM, :]

    def sel_keep(k0):
        blk_of_key = (k0 + lax.broadcasted_iota(jnp.int32, (tk, LANES), 0)) // SEL_BLOCK
        expand = (blk_of_key == lax.broadcasted_iota(jnp.int32, (tk, LANES), 1)).astype(jnp.bfloat16)
        causal = k0 + kpos_i <= qpos
        return [jnp.where((jnp.dot(expand, sel_bf[g], preferred_element_type=jnp.float32) > 0.5) & causal, 1.0, 0.0)
                for g in range(N_KV_HEADS)]

    def win_keep(k0):
        kpos = k0 + kpos_i
        return [jnp.where((kpos <= qpos) & (kpos > qpos - WINDOW), 1.0, 0.0)] * N_KV_HEADS

    hi = (q0 + tq + tk - 1) // tk
    o_s = attend(ks_ref, vst_ref, 0, hi, sel_keep)
    o_w = attend(kw_ref, vwt_ref, jnp.maximum(q0 - WINDOW, 0) // tk, hi, win_keep)

    for g in range(N_KV_HEADS):
        rows_g = slice(g * HEAD_DIM, (g + 1) * HEAD_DIM)
        for r in range(GQA_RATIO):
            col = (g * GQA_RATIO + r) * 3
            sl = slice(r * tq, (r + 1) * tq)
            comb = (gates[col:col + 1, :] * oc_scr[rows_g, sl] + gates[col + 1:col + 2, :] * o_s[g][:, sl]
                    + gates[col + 2:col + 3, :] * o_w[g][:, sl])
            o_ref[0, 0, r, rows_g, :] = comb


def _nsa_prompt_pallas(q, q_rot, blocks, kv_sel, kv_win, gt):
    b, t = q.shape[0], q.shape[1]
    tq, tk = NSA_TQ, NSA_TK
    assert t % tq == 0 and t % tk == 0 and t // SEL_BLOCK <= LANES and gt.shape[-1] % 8 == 0
    ncp = t // CMP_STRIDE
    n_qt, n_kt = t // tq, t // tk
    cols = GQA_RATIO * tq

    def group_pad_t(a, dtype):
        a = (a * ATTN_SCALE).reshape(b, n_qt, tq, N_KV_HEADS, GQA_RATIO, HEAD_DIM).transpose(0, 3, 1, 5, 4, 2)
        eye = jnp.eye(N_KV_HEADS, dtype=a.dtype)[None, :, None, :, None, None, None]
        return (a[:, :, :, None] * eye).reshape(b, N_KV_HEADS, n_qt, LANES, cols).astype(dtype)

    def values_t(kv):
        v = kv[:, :, 1].astype(jnp.bfloat16).reshape(b, n_kt, tk, LANES)
        return v.transpose(0, 1, 3, 2)

    qc = group_pad_t(q, jnp.float32)
    qr = group_pad_t(q_rot, jnp.bfloat16)
    blk = jnp.pad(blocks, ((0, 0), (0, ncp - blocks.shape[1]), (0, 0), (0, 0), (0, 0))).reshape(b, ncp, 2, LANES)
    kc = blk[:, :, 0]
    vct = blk[:, :, 1].transpose(0, 2, 1)
    ks = kv_sel[:, :, 0].astype(jnp.bfloat16).reshape(b, t, LANES)
    kw = kv_win[:, :, 0].astype(jnp.bfloat16).reshape(b, t, LANES)
    gtt = gt.astype(jnp.float32).reshape(b, n_qt, tq, gt.shape[-1]).transpose(0, 1, 3, 2)

    q_spec = pl.BlockSpec((1, N_KV_HEADS, 1, LANES, cols), lambda i, j: (i, 0, j, 0, 0))

    def whole(shape):
        nd = len(shape)
        return pl.BlockSpec((1,) + shape, lambda i, j: (i,) + (0,) * nd)

    out = pl.pallas_call(
        functools.partial(_nsa_prompt_kernel, seq=t),
        grid=(b, n_qt),
        in_specs=[q_spec, q_spec, whole((ncp, LANES)), whole((LANES, ncp)), whole((t, LANES)),
                  whole((n_kt, LANES, tk)), whole((t, LANES)), whole((n_kt, LANES, tk)),
                  pl.BlockSpec((1, 1, gt.shape[-1], tq), lambda i, j: (i, j, 0, 0))],
        out_specs=pl.BlockSpec((1, 1, GQA_RATIO, LANES, tq), lambda i, j: (i, j, 0, 0, 0)),
        out_shape=jax.ShapeDtypeStruct((b, n_qt, GQA_RATIO, LANES, tq), jnp.float32),
        scratch_shapes=[pltpu.VMEM((N_KV_HEADS, 1, cols), jnp.float32), pltpu.VMEM((N_KV_HEADS, 1, cols), jnp.float32),
                        pltpu.VMEM((N_KV_HEADS, LANES, cols), jnp.float32), pltpu.VMEM((LANES, cols), jnp.float32)],
        compiler_params=pltpu.CompilerParams(dimension_semantics=("arbitrary", "arbitrary")),
        name="nsa_prompt",
    )(qc, qr, kc, vct, ks, values_t(kv_sel), kw, values_t(kv_win), gtt)
    out = out.reshape(b, n_qt, GQA_RATIO, N_KV_HEADS, HEAD_DIM, tq).transpose(0, 1, 5, 3, 2, 4)
    return out.reshape(b, t, NSA_WIDTH)


KV_ROW = 2 * N_KV_HEADS * HEAD_DIM
SEG_W = CMP_STRIDE * KV_ROW
SAMPLE_PG = 32


def _sample_cmp_kernel(pt_ref, *refs, pg, n_pages, s_len):
    page_refs = refs[:pg]
    (new_ref, w1_ref, b1_ref, w2_ref, b2_ref, qct_ref, oc_ref, sel_ref, t_scr, pseg_scr) = refs[pg:]
    j = pl.program_id(1)
    n_steps = n_pages // pg
    seg_per_page = PAGE_SIZE // CMP_STRIDE
    m_rows = pg * seg_per_page
    past = n_pages * PAGE_SIZE
    n_seg = past // CMP_STRIDE
    rows = GQA_RATIO * s_len

    for i in range(pg):
        for c in range(2):
            t_scr[c, i * PAGE_SIZE:(i + 1) * PAGE_SIZE, :] = page_refs[i][0, c * LANES:(c + 1) * LANES, :].T
    for c in range(2):
        xc = jnp.concatenate([t_scr[c, pl.ds(r, m_rows, stride=CMP_STRIDE), :]
                              for r in range(CMP_STRIDE)], axis=1)
        xn = jnp.concatenate([new_ref[0, :, r * KV_ROW + c * LANES:r * KV_ROW + (c + 1) * LANES]
                              for r in range(CMP_STRIDE)], axis=1)
        xc = jnp.concatenate([xc, xn], axis=0).astype(jnp.bfloat16)
        pseg_scr[c, pl.ds(pl.multiple_of(j * m_rows, m_rows), m_rows + 8), :] = jnp.dot(
            xc, w1_ref[c], preferred_element_type=jnp.float32)

    @pl.when(j == n_steps - 1)
    def _():
        kv = []
        for c in range(2):
            acc = jnp.zeros((n_seg, LANES), jnp.float32) + b2_ref[c]
            for g in range(N_KV_HEADS):
                lo = g * N_HALF * CMP_HIDDEN
                hid = (pseg_scr[c, 0:n_seg, lo:lo + CMP_HIDDEN]
                       + pseg_scr[c, 1:n_seg + 1, lo + CMP_HIDDEN:lo + 2 * CMP_HIDDEN] + b1_ref[c])
                hid = jax.nn.gelu(hid)
                acc = acc + jnp.dot(hid.astype(jnp.bfloat16), w2_ref[c, g], preferred_element_type=jnp.float32)
            kv.append(acc)
        k_c, v_c = kv
        n_sb = past // SEL_BLOCK + -(-s_len // SEL_BLOCK)
        width = sel_ref.shape[-1]
        tok = lax.broadcasted_iota(jnp.int32, (n_seg, LANES), 1) % s_len
        blk_end = lax.broadcasted_iota(jnp.int32, (n_seg, LANES), 0) * CMP_STRIDE + (CMP_BLOCK - 1)
        cmask = blk_end <= past + tok
        s = jnp.dot(k_c, qct_ref[0], precision=HIGHEST, preferred_element_type=jnp.float32)
        s = jnp.where(cmask, s, NEG_INF)
        mx = jnp.max(s, axis=0, keepdims=True)
        p_t = jnp.where(cmask, jnp.exp(s - mx), 0.0)
        p_t = p_t / jnp.maximum(jnp.sum(p_t, axis=0, keepdims=True), 1e-30)
        o_c = jnp.dot(p_t.T.astype(jnp.bfloat16), v_c.astype(jnp.bfloat16), preferred_element_type=jnp.float32)
        ci = lax.broadcasted_iota(jnp.int32, (LANES, LANES), 0)
        cj = lax.broadcasted_iota(jnp.int32, (LANES, LANES), 1)
        same = ((ci // rows == cj // rows) & (ci % s_len == cj % s_len)).astype(jnp.float32)
        p_grp = jnp.dot(p_t, same, precision=HIGHEST, preferred_element_type=jnp.float32)
        score = _block_scores(p_grp, width, past + lax.broadcasted_iota(jnp.int32, (width, LANES), 1) % s_len)
        sel = _topk_cols(score, n_sb, min(N_SEL, n_sb)).T
        for g in range(N_KV_HEADS):
            oc_ref[0, g] = o_c[g * rows:(g + 1) * rows]
            sel_ref[0, g] = sel[g * rows:(g + 1) * rows]


def _sample_attn_kernel(pt_ref, *refs, pg, n_pages, s_len):
    page_refs = refs[:pg]
    (qr_ref, sel_ref, tail_ref, win_ref, wnew_ref, oc_ref, gt_ref, o_ref,
     k_scr, v_scr, m_scr, l_scr, acc_scr) = refs[pg:]
    j = pl.program_id(1)
    n_steps = n_pages // pg
    rows = GQA_RATIO * s_len
    keys = pg * PAGE_SIZE
    past = n_pages * PAGE_SIZE
    width = sel_ref.shape[-1]

    @pl.when(j == 0)
    def _():
        m_scr[...] = jnp.full(m_scr.shape, NEG_INF, jnp.float32)
        l_scr[...] = jnp.zeros(l_scr.shape, jnp.float32)
        acc_scr[...] = jnp.zeros(acc_scr.shape, jnp.float32)

    for i in range(pg):
        k_scr[:, i * PAGE_SIZE:(i + 1) * PAGE_SIZE] = page_refs[i][0, 0:LANES, :].astype(jnp.bfloat16)
        v_scr[:, i * PAGE_SIZE:(i + 1) * PAGE_SIZE] = page_refs[i][0, LANES:2 * LANES, :].astype(jnp.bfloat16)
    blk_of_key = (j * keys + lax.broadcasted_iota(jnp.int32, (width, keys), 1)) // SEL_BLOCK
    expand = (blk_of_key == lax.broadcasted_iota(jnp.int32, (width, keys), 0)).astype(jnp.bfloat16)
    for g in range(N_KV_HEADS):
        q = qr_ref[0, g]
        s = jnp.dot(q, k_scr[...], preferred_element_type=jnp.float32)
        mask = jnp.dot(sel_ref[0, g].astype(jnp.bfloat16), expand, preferred_element_type=jnp.float32) > 0.5
        s = jnp.where(mask, s, NEG_INF)
        m_old = m_scr[g]
        m_new = jnp.maximum(m_old, jnp.max(s, axis=-1, keepdims=True))
        alpha = jnp.exp(m_old - m_new)
        p = jnp.where(mask, jnp.exp(s - m_new), 0.0)
        l_scr[g] = alpha * l_scr[g] + jnp.sum(p, axis=-1, keepdims=True)
        acc_scr[g] = alpha * acc_scr[g] + _dot_nt(p.astype(jnp.bfloat16), v_scr[...])
        m_scr[g] = m_new

    @pl.when(j == n_steps - 1)
    def _():
        tok8 = lax.broadcasted_iota(jnp.int32, (rows, 8), 0) % s_len
        new_ok = lax.broadcasted_iota(jnp.int32, (rows, 8), 1) <= tok8
        wb = win_ref.shape[2]
        tokw = lax.broadcasted_iota(jnp.int32, (rows, wb), 0) % s_len
        kpos = past - wb + lax.broadcasted_iota(jnp.int32, (rows, wb), 1)
        win_ok = (kpos > past + tokw - WINDOW) & (kpos >= 0)
        gates = jax.nn.sigmoid(gt_ref[0])
        for g in range(N_KV_HEADS):
            q = qr_ref[0, g]
            tail_sel = sel_ref[0, g][:, past // SEL_BLOCK:past // SEL_BLOCK + 1] > 0.5
            t_mask = new_ok & tail_sel
            s_t = jnp.where(t_mask, _dot_nt(q, tail_ref[0, :, 0:LANES].astype(jnp.bfloat16)), NEG_INF)
            m_old = m_scr[g]
            m_new = jnp.maximum(m_old, jnp.max(s_t, axis=-1, keepdims=True))
            alpha = jnp.exp(m_old - m_new)
            p_t = jnp.where(t_mask, jnp.exp(s_t - m_new), 0.0)
            l_s = alpha * l_scr[g] + jnp.sum(p_t, axis=-1, keepdims=True)
            o_s = (alpha * acc_scr[g] + jnp.dot(p_t.astype(jnp.bfloat16),
                                                tail_ref[0, :, LANES:2 * LANES].astype(jnp.bfloat16),
                                                preferred_element_type=jnp.float32)) / jnp.maximum(l_s, 1e-30)
            s_w = jnp.where(win_ok, jnp.dot(q, win_ref[0, 0:LANES, :].astype(jnp.bfloat16),
                                            preferred_element_type=jnp.float32), NEG_INF)
            s_n = jnp.where(new_ok, _dot_nt(q, wnew_ref[0, :, 0:LANES].astype(jnp.bfloat16)), NEG_INF)
            mw = jnp.maximum(jnp.max(s_w, axis=-1, keepdims=True), jnp.max(s_n, axis=-1, keepdims=True))
            p_w = jnp.where(win_ok, jnp.exp(s_w - mw), 0.0)
            p_n = jnp.where(new_ok, jnp.exp(s_n - mw), 0.0)
            l_w = jnp.sum(p_w, axis=-1, keepdims=True) + jnp.sum(p_n, axis=-1, keepdims=True)
            o_w = (_dot_nt(p_w.astype(jnp.bfloat16), win_ref[0, LANES:2 * LANES, :].astype(jnp.bfloat16))
                   + jnp.dot(p_n.astype(jnp.bfloat16), wnew_ref[0, :, LANES:2 * LANES].astype(jnp.bfloat16),
                             preferred_element_type=jnp.float32)) / jnp.maximum(l_w, 1e-30)
            gl = gates[g]
            o_ref[0, g] = gl[:, 0:1] * oc_ref[0, g] + gl[:, 1:2] * o_s + gl[:, 2:3] * o_w


def _nsa_sample_pallas(q, q_rot, kv_cmp, kv_sel, kv_win, gt, cache_cmp, cache_sel, win_buf, page_table,
                       w1, b1, w2, b2):
    pg = SAMPLE_PG
    db, s_len = q.shape[0], q.shape[1]
    n_pages = page_table.shape[1]
    assert n_pages % pg == 0 and s_len <= 8 and s_len <= CMP_STRIDE and N_NSA_HEADS * s_len <= LANES
    past = n_pages * PAGE_SIZE
    n_seg = past // CMP_STRIDE
    n_sb = past // SEL_BLOCK + 1
    width = -(-n_sb // LANES) * LANES
    rows = GQA_RATIO * s_len
    seg_per_page = PAGE_SIZE // CMP_STRIDE

    def group_rows(a, dtype):
        a = (a * ATTN_SCALE).reshape(db, s_len, N_KV_HEADS, GQA_RATIO, HEAD_DIM).transpose(0, 2, 3, 1, 4)
        eye = jnp.eye(N_KV_HEADS, dtype=a.dtype)[None, :, None, None, :, None]
        return (a[:, :, :, :, None, :] * eye).reshape(db, N_KV_HEADS, rows, LANES).astype(dtype)

    def pad8(a):
        return jnp.pad(a.reshape(db, s_len, KV_ROW), ((0, 0), (0, 8 - s_len), (0, 0)))

    qg = (q * ATTN_SCALE).reshape(db, s_len, N_KV_HEADS, GQA_RATIO, HEAD_DIM).transpose(0, 2, 4, 3, 1)
    eye_q = jnp.eye(N_KV_HEADS, dtype=qg.dtype)[None, :, None, :, None]
    qct = (qg.reshape(db, N_KV_HEADS, HEAD_DIM, 1, rows) * eye_q).reshape(db, LANES, N_KV_HEADS * rows)
    qct = jnp.pad(qct, ((0, 0), (0, 0), (0, LANES - N_KV_HEADS * rows)))
    qr = group_rows(q_rot, jnp.bfloat16)
    new_seg = jnp.pad(kv_cmp.reshape(db, 1, s_len * KV_ROW), ((0, 0), (0, 7), (0, SEG_W - s_len * KV_ROW)))
    eye_g = jnp.eye(N_KV_HEADS, dtype=w1.dtype)
    w1t = w1.transpose(0, 2, 3, 1, 4)
    w1_bd = (w1t[:, :, None, :, None, :, :] * eye_g[None, None, :, None, :, None, None]).reshape(
        2, CMP_STRIDE * N_KV_HEADS * HEAD_DIM, N_KV_HEADS * N_HALF * CMP_HIDDEN).astype(jnp.bfloat16)
    w2_g = (w2[:, None, :, None, :] * eye_g[None, :, None, :, None]).reshape(
        2, N_KV_HEADS, CMP_HIDDEN, LANES).astype(jnp.bfloat16)
    b2_t = jnp.tile(b2, (1, N_KV_HEADS)).reshape(2, 1, LANES)
    b1_r = b1.reshape(2, 1, CMP_HIDDEN)

    def page_specs(block):
        return [pl.BlockSpec(block, functools.partial(lambda b, j, pt, i: (pt[b, j * pg + i], 0, 0), i=i))
                for i in range(pg)]

    def per_seq(shape):
        nd = len(shape)
        return pl.BlockSpec((1,) + shape, lambda b, j, pt: (b,) + (0,) * nd)

    def const(shape):
        nd = len(shape)
        return pl.BlockSpec(shape, lambda b, j, pt: (0,) * nd)

    def feature_major(a):
        return a.transpose(0, 2, 3, 4, 1).reshape(a.shape[0], KV_ROW, a.shape[1])

    page_block = (1, KV_ROW, PAGE_SIZE)
    o_c, sel = pl.pallas_call(
        functools.partial(_sample_cmp_kernel, pg=pg, n_pages=n_pages, s_len=s_len),
        grid_spec=pltpu.PrefetchScalarGridSpec(
            num_scalar_prefetch=1, grid=(db, n_pages // pg),
            in_specs=page_specs(page_block) + [
                per_seq((8, SEG_W)), const(w1_bd.shape), const(b1_r.shape), const(w2_g.shape), const(b2_t.shape),
                per_seq((LANES, LANES))],
            out_specs=[per_seq((N_KV_HEADS, rows, LANES)), per_seq((N_KV_HEADS, rows, width))],
            scratch_shapes=[pltpu.VMEM((2, pg * PAGE_SIZE, LANES), jnp.float32),
                            pltpu.VMEM((2, n_seg + 8, N_KV_HEADS * N_HALF * CMP_HIDDEN), jnp.float32)]),
        out_shape=[jax.ShapeDtypeStruct((db, N_KV_HEADS, rows, LANES), jnp.float32),
                   jax.ShapeDtypeStruct((db, N_KV_HEADS, rows, width), jnp.float32)],
        compiler_params=pltpu.CompilerParams(dimension_semantics=("arbitrary", "arbitrary"),
                                             vmem_limit_bytes=56 * 1024 * 1024),
        name="sample_cmp_select",
    )(page_table, *([feature_major(cache_cmp)] * pg), new_seg, w1_bd, b1_r, w2_g, b2_t, qct)

    gl = jnp.pad(gt.astype(jnp.float32).reshape(db, s_len, N_KV_HEADS, GQA_RATIO, 3).transpose(0, 2, 3, 1, 4)
                 .reshape(db, N_KV_HEADS, rows, 3), ((0, 0), (0, 0), (0, 0), (0, LANES - 3)))
    wb = win_buf.shape[1]
    out = pl.pallas_call(
        functools.partial(_sample_attn_kernel, pg=pg, n_pages=n_pages, s_len=s_len),
        grid_spec=pltpu.PrefetchScalarGridSpec(
            num_scalar_prefetch=1, grid=(db, n_pages // pg),
            in_specs=page_specs(page_block) + [
                per_seq((N_KV_HEADS, rows, LANES)), per_seq((N_KV_HEADS, rows, width)), per_seq((8, KV_ROW)),
                per_seq((KV_ROW, wb)), per_seq((8, KV_ROW)), per_seq((N_KV_HEADS, rows, LANES)),
                per_seq((N_KV_HEADS, rows, LANES))],
            out_specs=per_seq((N_KV_HEADS, rows, LANES)),
            scratch_shapes=[pltpu.VMEM((LANES, pg * PAGE_SIZE), jnp.bfloat16),
                            pltpu.VMEM((LANES, pg * PAGE_SIZE), jnp.bfloat16),
                            pltpu.VMEM((N_KV_HEADS, rows, 1), jnp.float32),
                            pltpu.VMEM((N_KV_HEADS, rows, 1), jnp.float32),
                            pltpu.VMEM((N_KV_HEADS, rows, LANES), jnp.float32)]),
        out_shape=jax.ShapeDtypeStruct((db, N_KV_HEADS, rows, LANES), jnp.float32),
        compiler_params=pltpu.CompilerParams(dimension_semantics=("arbitrary", "arbitrary")),
        name="sample_sel_win_attn",
    )(page_table, *([feature_major(cache_sel)] * pg), qr, sel, pad8(kv_sel),
      feature_major(win_buf), pad8(kv_win), o_c, gl)
    out = out.reshape(db, N_KV_HEADS, GQA_RATIO, s_len, N_KV_HEADS, HEAD_DIM)
    out = jnp.stack([out[:, g, :, :, g, :] for g in range(N_KV_HEADS)], axis=1)
    return out.transpose(0, 3, 1, 2, 4).reshape(db, s_len, NSA_WIDTH)


ROUTER_TT = 256
MOE_BM = 256


def _router_kernel(x_ref, mix_ref, wout_ref, g_ref, wrt_ref, br_ref,
                   x1_ref, h_ref, eidx_ref, gate_ref, rank_ref, cnt_ref, wout_bf, run_scr):
    tt = x_ref.shape[0]
    n_e = wrt_ref.shape[0]

    @pl.when(pl.program_id(0) == 0)
    def _():
        wout_bf[...] = wout_ref[...].astype(jnp.bfloat16)
        run_scr[...] = jnp.zeros_like(run_scr)

    x1 = x_ref[...] + jnp.dot(mix_ref[...].astype(jnp.bfloat16), wout_bf[...], preferred_element_type=jnp.float32)
    x1_ref[...] = x1
    hn = x1 * lax.rsqrt(jnp.mean(x1 * x1, axis=-1, keepdims=True) + RMS_EPS) * g_ref[...]
    h_ref[...] = hn

    score = _dot_nt(wrt_ref[...], hn, HIGHEST) + br_ref[...]
    ef = lax.broadcasted_iota(jnp.int32, (n_e, tt), 0).astype(jnp.float32)
    vals, hits = [], []
    for k in range(TOP_K):
        best = jnp.max(score, axis=0, keepdims=True)
        first = jnp.min(jnp.where(score == best, ef, float(n_e)), axis=0, keepdims=True)
        hit = ef == first
        vals.append(best)
        hits.append(hit)
        eidx_ref[k:k + 1, :] = first.astype(jnp.int32)
        score = jnp.where(hit, -3e38, score)
    exps = [jnp.exp(v - vals[0]) for v in vals]
    denom = sum(exps[1:], exps[0])
    for k in range(TOP_K):
        gate_ref[k:k + 1, :] = exps[k] / denom

    chosen = functools.reduce(jnp.logical_or, hits)
    before = (lax.broadcasted_iota(jnp.int32, (tt, tt), 0) < lax.broadcasted_iota(jnp.int32, (tt, tt), 1))
    earlier = jnp.dot(chosen.astype(jnp.bfloat16), before.astype(jnp.bfloat16), preferred_element_type=jnp.float32)
    pos = earlier + run_scr[...]
    for k in range(TOP_K):
        rank_ref[k:k + 1, :] = jnp.sum(jnp.where(hits[k], pos, 0.0), axis=0, keepdims=True).astype(jnp.int32)
    run_scr[...] = run_scr[...] + jnp.sum(chosen.astype(jnp.float32), axis=1, keepdims=True)
    cnt_ref[...] = jnp.broadcast_to(run_scr[...], cnt_ref.shape).astype(jnp.int32)


def _router_pallas(x, mix, w_out, g_ffn, w_router, b_router):
    n, d = x.shape
    c = mix.shape[1]
    n_e = w_router.shape[1]
    tt = ROUTER_TT
    assert n % tt == 0

    def row(w):
        return pl.BlockSpec((tt, w), lambda i: (i, 0))

    def full(a, b):
        return pl.BlockSpec((a, b), lambda i: (0, 0))

    k4 = pl.BlockSpec((TOP_K, tt), lambda i: (0, i))
    return pl.pallas_call(
        _router_kernel,
        grid=(n // tt,),
        in_specs=[row(d), row(c), full(c, d), full(1, d), full(n_e, d), full(n_e, 1)],
        out_specs=[row(d), row(d), k4, k4, k4, full(n_e, LANES)],
        out_shape=[jax.ShapeDtypeStruct((n, d), jnp.float32), jax.ShapeDtypeStruct((n, d), jnp.float32),
                   jax.ShapeDtypeStruct((TOP_K, n), jnp.int32), jax.ShapeDtypeStruct((TOP_K, n), jnp.float32),
                   jax.ShapeDtypeStruct((TOP_K, n), jnp.int32), jax.ShapeDtypeStruct((n_e, LANES), jnp.int32)],
        scratch_shapes=[pltpu.VMEM((c, d), jnp.bfloat16), pltpu.VMEM((n_e, 1), jnp.float32)],
        compiler_params=pltpu.CompilerParams(dimension_semantics=("arbitrary",)),
        name="outproj_router",
    )(x, mix, w_out, g_ffn.reshape(1, d), w_router.T, b_router.reshape(n_e, 1))


def _expert_kernel(blk_e_ref, n_used_ref, x_ref, wup_ref, bup_ref, wdn_ref, bdn_ref, o_ref, wup_bf, wdn_bf):
    i = pl.program_id(0)
    d_ff = wdn_ref.shape[1]

    @pl.when(i < n_used_ref[0])
    def _():
        e = blk_e_ref[i]
        prev = blk_e_ref[jnp.maximum(i - 1, 0)]

        @pl.when((i == 0) | (e != prev))
        def _():
            wup_bf[...] = wup_ref[0].astype(jnp.bfloat16)
            wdn_bf[...] = wdn_ref[0].astype(jnp.bfloat16)

        u = jnp.dot(x_ref[...].astype(jnp.bfloat16), wup_bf[...], preferred_element_type=jnp.float32) + bup_ref[0]
        glu = jnp.minimum(u[:, :d_ff], SWIGLU_LIMIT)
        lin = jnp.clip(u[:, d_ff:], -SWIGLU_LIMIT, SWIGLU_LIMIT)
        a = glu * jax.nn.sigmoid(SWIGLU_ALPHA * glu) * (lin + 1.0)
        o_ref[...] = jnp.dot(a.astype(jnp.bfloat16), wdn_bf[...], preferred_element_type=jnp.float32) + bdn_ref[0]

    @pl.when(i >= n_used_ref[0])
    def _():
        o_ref[...] = jnp.zeros_like(o_ref)


def _expert_vmem_bytes(bm, d, f2, d_ff):
    weights = 2 * 4 * (d * f2 + d_ff * d) + 2 * (d * f2 + d_ff * d)
    rows = 2 * bm * d * (4 + 4)
    temps = bm * f2 * 4 * 2 + bm * d_ff * (4 + 2)
    return weights + rows + temps


def _experts_pallas(xs, blk_e, n_used, w_up, b_up, w_down, b_down, bm):
    n_slots, d = xs.shape
    n_e, _, f2 = w_up.shape
    d_ff = w_down.shape[1]
    grid_spec = pltpu.PrefetchScalarGridSpec(
        num_scalar_prefetch=2,
        grid=(n_slots // bm,),
        in_specs=[pl.BlockSpec((bm, d), lambda i, be, nu: (i, 0)),
                  pl.BlockSpec((1, d, f2), lambda i, be, nu: (be[i], 0, 0)),
                  pl.BlockSpec((1, 1, f2), lambda i, be, nu: (be[i], 0, 0)),
                  pl.BlockSpec((1, d_ff, d), lambda i, be, nu: (be[i], 0, 0)),
                  pl.BlockSpec((1, 1, d), lambda i, be, nu: (be[i], 0, 0))],
        out_specs=pl.BlockSpec((bm, d), lambda i, be, nu: (i, 0)),
        scratch_shapes=[pltpu.VMEM((d, f2), jnp.bfloat16), pltpu.VMEM((d_ff, d), jnp.bfloat16)],
    )
    vmem_limit = _expert_vmem_bytes(bm, d, f2, d_ff) * 5 // 4
    return pl.pallas_call(
        _expert_kernel,
        grid_spec=grid_spec,
        out_shape=jax.ShapeDtypeStruct((n_slots, d), jnp.float32),
        compiler_params=pltpu.CompilerParams(dimension_semantics=("arbitrary",), vmem_limit_bytes=vmem_limit),
        name="expert_mlp",
    )(blk_e, n_used, xs, w_up, b_up.reshape(n_e, 1, f2), w_down, b_down.reshape(n_e, 1, d))


def _combine_norm_kernel(x1_ref, og_ref, gate_ref, g_ref, o_ref):
    y = x1_ref[...]
    gates = gate_ref[...]
    for k in range(TOP_K):
        y = y + gates[:, k:k + 1] * og_ref[k]
    o_ref[...] = y * lax.rsqrt(jnp.mean(y * y, axis=-1, keepdims=True) + RMS_EPS) * g_ref[...]


def _combine_norm_pallas(x1, og, gate_t, g_final):
    n, d = x1.shape
    tt = ROUTER_TT
    return pl.pallas_call(
        _combine_norm_kernel,
        grid=(n // tt,),
        in_specs=[pl.BlockSpec((tt, d), lambda i: (i, 0)), pl.BlockSpec((TOP_K, tt, d), lambda i: (0, i, 0)),
                  pl.BlockSpec((tt, TOP_K), lambda i: (i, 0)), pl.BlockSpec((1, d), lambda i: (0, 0))],
        out_specs=pl.BlockSpec((tt, d), lambda i: (i, 0)),
        out_shape=jax.ShapeDtypeStruct((n, d), jnp.float32),
        compiler_params=pltpu.CompilerParams(dimension_semantics=("arbitrary",)),
        name="combine_final_norm",
    )(x1, og, gate_t, g_final.reshape(1, d))


def _finish_pallas(x, mix, w_out, g_ffn, w_router, b_router, w_up, b_up, w_down, b_down, g_final):
    n, d = x.shape
    n_e = w_router.shape[1]
    bm = MOE_BM
    x1, h, eidx, gate, rank, cnt = _router_pallas(x, mix, w_out, g_ffn, w_router, b_router)
    counts = cnt[:, 0]
    padded = (counts + bm - 1) // bm * bm
    pad_end = jnp.cumsum(padded)
    gstart = pad_end - padded
    pick = eidx[None] == jnp.arange(n_e, dtype=jnp.int32)[:, None, None]
    dest = jnp.sum(jnp.where(pick, gstart[:, None, None], 0), axis=0) + rank
    nb = -(-(n * TOP_K) // bm) + n_e
    blk_start = jnp.arange(nb, dtype=jnp.int32) * bm
    blk_e = jnp.minimum(jnp.sum((pad_end[None, :] <= blk_start[:, None]).astype(jnp.int32), axis=1), n_e - 1)
    n_used = (pad_end[-1] // bm).astype(jnp.int32).reshape(1)
    tok = jnp.broadcast_to(jnp.arange(n, dtype=jnp.int32)[None, :], (TOP_K, n))
    src = jnp.zeros((nb * bm,), jnp.int32).at[dest.reshape(-1)].set(tok.reshape(-1))
    out = _experts_pallas(h[src], blk_e, n_used, w_up, b_up, w_down, b_down, bm)
    return _combine_norm_pallas(x1, out[dest], gate.T, g_final)


def kernel(x_prompt, x_sample, cache_cmp_kv, cache_sel_kv, cache_win_kv, state_ret, page_table, g_attn, w_in, w_cmp1, b_cmp1, w_cmp2, b_cmp2, w_out, g_ffn, w_router, b_router, w_up, b_up, w_down, b_down, g_final):
    seq = x_prompt.shape[1]
    past = page_table.shape[1] * PAGE_SIZE
    pos_p = jnp.arange(seq, dtype=jnp.int32)
    pos_s = past + jnp.arange(x_sample.shape[1], dtype=jnp.int32)
    log_g = _ret_log_decay()
    assert DEPTH == 1
    l = 0
    d = x_prompt.shape[-1]
    q, kv_cmp_p, kv_sel_p, kv_win_p, gt, r_q, r_k, r_v, r_g = _mixer_inputs(x_prompt, pos_p, g_attn[l], w_in[l])
    blocks = _compress_blocks(_segment_proj(kv_cmp_p, w_cmp1[l]), b_cmp1[l], w_cmp2[l], b_cmp2[l])
    o_nsa = _nsa_prompt_pallas(q, _rope(q, pos_p), blocks, kv_sel_p, kv_win_p, gt)
    o_ret, ret_p = _ret_prompt_pallas(r_q, r_k, r_v, r_g, log_g)
    mix_p = jnp.concatenate([o_nsa, o_ret], axis=-1)
    q, kv_cmp_s, kv_sel_s, kv_win_s, gt, r_q, r_k, r_v, r_g = _mixer_inputs(x_sample, pos_s, g_attn[l], w_in[l])
    o_nsa_s = _nsa_sample_pallas(q, _rope(q, pos_s), kv_cmp_s, kv_sel_s, kv_win_s, gt, cache_cmp_kv[:, l],
                                 cache_sel_kv[:, l], cache_win_kv[:, l], page_table,
                                 w_cmp1[l], b_cmp1[l], w_cmp2[l], b_cmp2[l])
    win_s = jnp.concatenate([cache_win_kv[:, l], kv_win_s], axis=1)[:, x_sample.shape[1]:]
    ret_s, o_r = _ret_chunk(state_ret[:, l].astype(jnp.float32), r_q.astype(jnp.float32), r_k.astype(jnp.float32), r_v.astype(jnp.float32), log_g)
    mix_s = jnp.concatenate([o_nsa_s, _ret_out(o_r, r_g)], axis=-1)
    n_p = x_prompt.shape[0] * seq
    x_all = jnp.concatenate([x_prompt.reshape(n_p, d), x_sample.reshape(-1, d)], axis=0)
    mix_all = jnp.concatenate([mix_p.reshape(n_p, MIX_WIDTH), mix_s.reshape(-1, MIX_WIDTH)], axis=0)
    y = _finish_pallas(x_all, mix_all, w_out[l], g_ffn[l], w_router[l], b_router[l], w_up[l], b_up[l], w_down[l], b_down[l], g_final)
    y_prompt = y[:n_p].reshape(x_prompt.shape)
    y_sample = y[n_p:].reshape(x_sample.shape)
    win_p = kv_win_p[:, seq - min(WINDOW, seq):]
    return (y_prompt, y_sample, kv_cmp_p[:, None], kv_sel_p[:, None], win_p[:, None], ret_p[:, None],
            kv_cmp_s[:, None], kv_sel_s[:, None], win_s[:, None], ret_s[:, None])
```

```python
import functools
import jax, jax.numpy as jnp
from jax import lax
import numpy as np
from jax.experimental import pallas as pl
from jax.experimental.pallas import tpu as pltpu

DEPTH = 1
PAGE_SIZE = 128

HEAD_DIM = 64
N_NSA_HEADS = 8
N_KV_HEADS = 2
GQA_RATIO = N_NSA_HEADS // N_KV_HEADS
CMP_BLOCK = 32
CMP_STRIDE = 16
N_HALF = CMP_BLOCK // CMP_STRIDE
CMP_HIDDEN = 256
SEL_BLOCK = 64
N_SEL = 16
WINDOW = 512
N_RET_HEADS = 4
RET_DK = 64
RET_DV = 128
RET_CHUNK = 128
TOP_K = 4
SWIGLU_ALPHA = 1.702
SWIGLU_LIMIT = 7.0
ROPE_THETA = 10000.0
RMS_EPS = 1e-5
NEG_INF = -1e30
FORCED_SCORE = 1e6
INVALID_SCORE = -1e9
NSA_WIDTH = N_NSA_HEADS * HEAD_DIM
KV_WIDTH = N_KV_HEADS * HEAD_DIM
RET_QK_WIDTH = N_RET_HEADS * RET_DK
RET_WIDTH = N_RET_HEADS * RET_DV
IN_SIZES = (NSA_WIDTH, KV_WIDTH, KV_WIDTH, KV_WIDTH, KV_WIDTH, KV_WIDTH, KV_WIDTH, 3 * N_NSA_HEADS, RET_QK_WIDTH, RET_QK_WIDTH, RET_WIDTH, RET_WIDTH)
MIX_WIDTH = NSA_WIDTH + RET_WIDTH
ATTN_SCALE = HEAD_DIM ** -0.5

LANES = 128
HIGHEST = lax.Precision.HIGHEST


def _rmsnorm(x, g):
    xf = x.astype(jnp.float32)
    y = xf * lax.rsqrt(jnp.mean(xf * xf, axis=-1, keepdims=True) + RMS_EPS)
    return (y * g.astype(jnp.float32)).astype(x.dtype)


def _rope(x, pos):
    half = x.shape[-1] // 2
    inv = ROPE_THETA ** (-jnp.arange(half, dtype=jnp.float32) / half)
    ang = pos.astype(jnp.float32)[:, None] * inv[None, :]
    cos = jnp.cos(ang)[None, :, None, :]
    sin = jnp.sin(ang)[None, :, None, :]
    xf = x.astype(jnp.float32)
    x1, x2 = xf[..., :half], xf[..., half:]
    return jnp.concatenate([x1 * cos - x2 * sin, x2 * cos + x1 * sin], axis=-1).astype(x.dtype)


def _mixer_inputs(x, pos, g_attn, w_in):
    b, t = x.shape[0], x.shape[1]
    h = _rmsnorm(x, g_attn)
    proj = jnp.einsum('btd,dc->btc', h, w_in)
    offs = [int(o) for o in np.cumsum(np.array(IN_SIZES))[:-1]]
    q, kc, vc, ks, vs, kw, vw, gt, rq, rk, rv, rg = jnp.split(proj, offs, axis=-1)

    def heads(a, n, d):
        return a.reshape(b, t, n, d)

    def kvh(a):
        return heads(a, N_KV_HEADS, HEAD_DIM)

    kv_cmp = jnp.stack([kvh(kc), kvh(vc)], axis=2)
    kv_sel = jnp.stack([_rope(kvh(ks), pos), kvh(vs)], axis=2)
    kv_win = jnp.stack([_rope(kvh(kw), pos), kvh(vw)], axis=2)
    r_q = _rope(heads(rq, N_RET_HEADS, RET_DK), pos)
    r_k = _rope(heads(rk, N_RET_HEADS, RET_DK), pos) * (RET_DK ** -0.5)
    r_v = heads(rv, N_RET_HEADS, RET_DV)
    return heads(q, N_NSA_HEADS, HEAD_DIM), kv_cmp, kv_sel, kv_win, gt, r_q, r_k, r_v, rg


def _segment_proj(kv_rows, w1):
    b, l = kv_rows.shape[0], kv_rows.shape[1]
    seg = kv_rows.reshape(b, l // CMP_STRIDE, CMP_STRIDE, 2, N_KV_HEADS, HEAD_DIM)
    return jnp.einsum('bsrcgd,chrdk->bscghk', seg, w1)


def _compress_blocks(p_seg, b1, w2, b2):
    n_cmp = p_seg.shape[1] - N_HALF + 1
    hid = sum(p_seg[:, h:h + n_cmp, :, :, h, :] for h in range(N_HALF)) + b1[None, None, :, None, :]
    hid = jax.nn.gelu(hid.astype(jnp.float32))
    out = jnp.einsum('bncgk,ckd->bncgd', hid, w2.astype(jnp.float32))
    return out + b2.astype(jnp.float32)[None, None, :, None, :]


def _ret_log_decay():
    return jnp.log1p(-jnp.exp2(-5.0 - jnp.arange(N_RET_HEADS, dtype=jnp.float32)))


def _ret_chunk(state, q, k, v, log_g):
    c = q.shape[1]
    i = jnp.arange(c, dtype=jnp.float32)
    diff = i[:, None] - i[None, :]
    decay = jnp.where(diff[None] >= 0, jnp.exp(log_g[:, None, None] * jnp.maximum(diff, 0.0)[None]), 0.0)
    scores = jnp.einsum('bihd,bjhd->bhij', q, k) * decay[None]
    inner = jnp.einsum('bhij,bjhv->bihv', scores, v)
    q_dec = jnp.exp(log_g[None, :] * (i[:, None] + 1.0))
    cross = jnp.einsum('bihd,bhdv->bihv', q * q_dec[None, :, :, None], state)
    k_dec = jnp.exp(log_g[None, :] * (c - 1.0 - i[:, None]))
    new_state = jnp.exp(log_g * c)[None, :, None, None] * state + jnp.einsum('bjhd,bjhv->bhdv', k * k_dec[None, :, :, None], v)
    return new_state, inner + cross


def _ret_prompt_kernel(lg_ref, q_ref, k_ref, kt_ref, v_ref, rg_ref, o_ref, s_ref, state):
    c = q_ref.shape[2]

    @pl.when(pl.program_id(1) == 0)
    def _():
        state[...] = jnp.zeros_like(state)

    ii = lax.broadcasted_iota(jnp.int32, (c, c), 0)
    jj = lax.broadcasted_iota(jnp.int32, (c, c), 1)
    diff = (ii - jj).astype(jnp.float32)
    pos_col = lax.broadcasted_iota(jnp.int32, (c, 1), 0).astype(jnp.float32)
    pos_row = lax.broadcasted_iota(jnp.int32, (1, c), 1).astype(jnp.float32)
    for h in range(N_RET_HEADS):
        lg = lg_ref[h]
        q, k, v = q_ref[0, h], k_ref[0, h], v_ref[0, h].astype(jnp.bfloat16)
        decay = jnp.where(diff >= 0, jnp.exp(lg * jnp.maximum(diff, 0.0)), 0.0)
        scores = _dot_nt(q.astype(jnp.bfloat16), k.astype(jnp.bfloat16)) * decay
        inner = jnp.dot(scores.astype(jnp.bfloat16), v, preferred_element_type=jnp.float32)
        q_dec = q * jnp.exp(lg * (pos_col + 1.0))
        cross = jnp.dot(q_dec.astype(jnp.bfloat16), state[h].astype(jnp.bfloat16), preferred_element_type=jnp.float32)
        kt_dec = kt_ref[0, h] * jnp.exp(lg * (c - 1.0 - pos_row))
        state[h] = jnp.exp(lg * c) * state[h] + jnp.dot(kt_dec.astype(jnp.bfloat16), v,
                                                        preferred_element_type=jnp.float32)
        o = inner + cross
        o = o * lax.rsqrt(jnp.mean(o * o, axis=-1, keepdims=True) + RMS_EPS)
        o_ref[0, :, h * RET_DV:(h + 1) * RET_DV] = o * jax.nn.silu(rg_ref[0, :, h * RET_DV:(h + 1) * RET_DV])
    s_ref[0] = state[...]


def _ret_prompt_pallas(r_q, r_k, r_v, r_g, log_g):
    b, t = r_q.shape[0], r_q.shape[1]
    c = RET_CHUNK
    assert t % c == 0
    qh = r_q.transpose(0, 2, 1, 3)
    kh = r_k.transpose(0, 2, 1, 3)
    kt = r_k.transpose(0, 2, 3, 1)
    vh = r_v.transpose(0, 2, 1, 3)

    def chunk(d):
        return pl.BlockSpec((1, N_RET_HEADS, c, d), lambda i, j, lg: (i, 0, j, 0))

    return pl.pallas_call(
        _ret_prompt_kernel,
        grid_spec=pltpu.PrefetchScalarGridSpec(
            num_scalar_prefetch=1, grid=(b, t // c),
            in_specs=[chunk(RET_DK), chunk(RET_DK),
                      pl.BlockSpec((1, N_RET_HEADS, RET_DK, c), lambda i, j, lg: (i, 0, 0, j)),
                      chunk(RET_DV), pl.BlockSpec((1, c, RET_WIDTH), lambda i, j, lg: (i, j, 0))],
            out_specs=[pl.BlockSpec((1, c, RET_WIDTH), lambda i, j, lg: (i, j, 0)),
                       pl.BlockSpec((1, N_RET_HEADS, RET_DK, RET_DV), lambda i, j, lg: (i, 0, 0, 0))],
            scratch_shapes=[pltpu.VMEM((N_RET_HEADS, RET_DK, RET_DV), jnp.float32)]),
        out_shape=[jax.ShapeDtypeStruct((b, t, RET_WIDTH), jnp.float32),
                   jax.ShapeDtypeStruct((b, N_RET_HEADS, RET_DK, RET_DV), jnp.float32)],
        compiler_params=pltpu.CompilerParams(dimension_semantics=("arbitrary", "arbitrary")),
        name="retention_prompt",
    )(log_g, qh, kh, kt, vh, r_g)


def _ret_out(o, r_g):
    b, t = o.shape[0], o.shape[1]
    o = o * lax.rsqrt(jnp.mean(o * o, axis=-1, keepdims=True) + RMS_EPS)
    return o.reshape(b, t, RET_WIDTH) * jax.nn.silu(r_g.astype(jnp.float32))


def _dot_nt(a, b, precision=None):
    return lax.dot_general(a, b, (((1,), (1,)), ((), ())), precision=precision, preferred_element_type=jnp.float32)


def _topk_cols(score, n_valid, k):
    height, width = score.shape
    jf = lax.broadcasted_iota(jnp.int32, (height, width), 0).astype(jnp.float32)
    score = jnp.where(jf < n_valid, score, -3e38)
    sel = jnp.zeros((height, width), jnp.float32)
    for _ in range(k):
        best = jnp.max(score, axis=0, keepdims=True)
        first = jnp.min(jnp.where(score == best, jf, float(height)), axis=0, keepdims=True)
        hit = jf == first
        sel = jnp.where(hit, 1.0, sel)
        score = jnp.where(hit, -3e38, score)
    return sel


def _block_scores(p_grp, n_blocks, q_pos):
    n_cmp = p_grp.shape[0]
    ratio = SEL_BLOCK // CMP_STRIDE
    jj = lax.broadcasted_iota(jnp.int32, (n_blocks, n_cmp), 0)
    nn = lax.broadcasted_iota(jnp.int32, (n_blocks, n_cmp), 1)
    overlap = ((nn >= ratio * jj - (N_HALF - 1)) & (nn <= ratio * jj + ratio - 1)).astype(jnp.float32)
    score = jnp.dot(overlap, p_grp, precision=HIGHEST, preferred_element_type=jnp.float32)
    jb = lax.broadcasted_iota(jnp.int32, q_pos.shape, 0)
    cur = q_pos // SEL_BLOCK
    forced = (jb == 0) | (jb == cur) | (jb == cur - 1)
    valid = jb * SEL_BLOCK <= q_pos
    score = jnp.where(forced, FORCED_SCORE, score)
    return jnp.where(valid, score, INVALID_SCORE)


NSA_TQ = 128
NSA_TK = 512


def _nsa_prompt_kernel(qc_ref, qr_ref, kc_ref, vct_ref, ks_ref, vst_ref, kw_ref, vwt_ref, gt_ref, o_ref,
                       m_scr, l_scr, acc_scr, oc_scr, *, seq):
    tq, tk = NSA_TQ, NSA_TK
    cols = GQA_RATIO * tq
    qt = pl.program_id(1)
    q0 = qt * tq
    ncp = kc_ref.shape[1]
    n_sb = seq // SEL_BLOCK
    gates = jax.nn.sigmoid(gt_ref[0, 0])

    def heads(a):
        return jnp.concatenate([a] * GQA_RATIO, axis=1)

    kpos_i = lax.broadcasted_iota(jnp.int32, (tk, tq), 0)
    qpos = q0 + lax.broadcasted_iota(jnp.int32, (tk, tq), 1)

    def attend(k_ref, vt_ref, lo, hi, keep_fn):
        m_scr[...] = jnp.full(m_scr.shape, NEG_INF, jnp.float32)
        l_scr[...] = jnp.zeros(l_scr.shape, jnp.float32)
        acc_scr[...] = jnp.zeros(acc_scr.shape, jnp.float32)

        def body(kt, carry):
            k0 = pl.multiple_of(kt * tk, tk)
            k = k_ref[0, pl.ds(k0, tk), :]
            vt = vt_ref[0, kt]
            keeps = keep_fn(k0)
            for g in range(N_KV_HEADS):
                keep = keeps[g]
                s = jnp.dot(k, qr_ref[0, g, 0], preferred_element_type=jnp.float32)
                s = s + heads((keep - 1.0) * -NEG_INF)
                m_old = m_scr[g]
                m_new = jnp.maximum(m_old, jnp.max(s, axis=0, keepdims=True))
                alpha = jnp.exp(m_old - m_new)
                p = jnp.exp(s - m_new) * heads(keep)
                l_scr[g] = alpha * l_scr[g] + jnp.sum(p, axis=0, keepdims=True)
                acc_scr[g] = alpha * acc_scr[g] + jnp.dot(vt, p.astype(jnp.bfloat16),
                                                          preferred_element_type=jnp.float32)
                m_scr[g] = m_new
            return carry

        lax.fori_loop(lo, hi, body, 0)
        return [acc_scr[g, g * HEAD_DIM:(g + 1) * HEAD_DIM, :] / jnp.maximum(l_scr[g], 1e-30)
                for g in range(N_KV_HEADS)]

    sel_bf = []
    for g in range(N_KV_HEADS):
        s = jnp.dot(kc_ref[0], qc_ref[0, g, 0], precision=HIGHEST, preferred_element_type=jnp.float32)
        blk_end = lax.broadcasted_iota(jnp.int32, (ncp, tq), 0) * CMP_STRIDE + (CMP_BLOCK - 1)
        ckeep = heads(jnp.where(blk_end <= q0 + lax.broadcasted_iota(jnp.int32, (ncp, tq), 1), 1.0, 0.0)) > 0.5
        s = jnp.where(ckeep, s, NEG_INF)
        mx = jnp.max(s, axis=0, keepdims=True)
        p = jnp.where(ckeep, jnp.exp(s - mx), 0.0)
        p = p / jnp.maximum(jnp.sum(p, axis=0, keepdims=True), 1e-30)
        o_c = jnp.dot(vct_ref[0].astype(jnp.bfloat16), p.astype(jnp.bfloat16),
                      preferred_element_type=jnp.float32)
        p_grp = sum(p[:, r * tq:(r + 1) * tq] for r in range(GQA_RATIO))

        score = _block_scores(p_grp, n_sb, q0 + lax.broadcasted_iota(jnp.int32, (n_sb, tq), 1))
        sel_t = _topk_cols(score, n_sb, min(N_SEL, n_sb))
        if n_sb < LANES:
            sel_t = jnp.concatenate([sel_t, jnp.zeros((LANES - n_sb, tq), jnp.float32)], axis=0)
        sel_bf.append(sel_t.astype(jnp.bfloat16))
        oc_scr[g * HEAD_DIM:(g + 1) * HEAD_DIM, :] = o_c[g * HEAD_DIM:(g + 1) * HEAD_DIM, :]

    def sel_keep(k0):
        blk_of_key = (k0 + lax.broadcasted_iota(jnp.int32, (tk, LANES), 0)) // SEL_BLOCK
        expand = (blk_of_key == lax.broadcasted_iota(jnp.int32, (tk, LANES), 1)).astype(jnp.bfloat16)
        causal = k0 + kpos_i <= qpos
        return [jnp.where((jnp.dot(expand, sel_bf[g], preferred_element_type=jnp.float32) > 0.5) & causal, 1.0, 0.0)
                for g in range(N_KV_HEADS)]

    def win_keep(k0):
        kpos = k0 + kpos_i
        return [jnp.where((kpos <= qpos) & (kpos > qpos - WINDOW), 1.0, 0.0)] * N_KV_HEADS

    hi = (q0 + tq + tk - 1) // tk
    o_s = attend(ks_ref, vst_ref, 0, hi, sel_keep)
    o_w = attend(kw_ref, vwt_ref, jnp.maximum(q0 - WINDOW, 0) // tk, hi, win_keep)

    for g in range(N_KV_HEADS):
        rows_g = slice(g * HEAD_DIM, (g + 1) * HEAD_DIM)
        for r in range(GQA_RATIO):
            col = (g * GQA_RATIO + r) * 3
            sl = slice(r * tq, (r + 1) * tq)
            comb = (gates[col:col + 1, :] * oc_scr[rows_g, sl] + gates[col + 1:col + 2, :] * o_s[g][:, sl]
                    + gates[col + 2:col + 3, :] * o_w[g][:, sl])
            o_ref[0, 0, r, rows_g, :] = comb


def _nsa_prompt_pallas(q, q_rot, blocks, kv_sel, kv_win, gt):
    b, t = q.shape[0], q.shape[1]
    tq, tk = NSA_TQ, NSA_TK
    assert t % tq == 0 and t % tk == 0 and t // SEL_BLOCK <= LANES and gt.shape[-1] % 8 == 0
    ncp = t // CMP_STRIDE
    n_qt, n_kt = t // tq, t // tk
    cols = GQA_RATIO * tq

    def group_pad_t(a, dtype):
        a = (a * ATTN_SCALE).reshape(b, n_qt, tq, N_KV_HEADS, GQA_RATIO, HEAD_DIM).transpose(0, 3, 1, 5, 4, 2)
        eye = jnp.eye(N_KV_HEADS, dtype=a.dtype)[None, :, None, :, None, None, None]
        return (a[:, :, :, None] * eye).reshape(b, N_KV_HEADS, n_qt, LANES, cols).astype(dtype)

    def values_t(kv):
        v = kv[:, :, 1].astype(jnp.bfloat16).reshape(b, n_kt, tk, LANES)
        return v.transpose(0, 1, 3, 2)

    qc = group_pad_t(q, jnp.float32)
    qr = group_pad_t(q_rot, jnp.bfloat16)
    blk = jnp.pad(blocks, ((0, 0), (0, ncp - blocks.shape[1]), (0, 0), (0, 0), (0, 0))).reshape(b, ncp, 2, LANES)
    kc = blk[:, :, 0]
    vct = blk[:, :, 1].transpose(0, 2, 1)
    ks = kv_sel[:, :, 0].astype(jnp.bfloat16).reshape(b, t, LANES)
    kw = kv_win[:, :, 0].astype(jnp.bfloat16).reshape(b, t, LANES)
    gtt = gt.astype(jnp.float32).reshape(b, n_qt, tq, gt.shape[-1]).transpose(0, 1, 3, 2)

    q_spec = pl.BlockSpec((1, N_KV_HEADS, 1, LANES, cols), lambda i, j: (i, 0, j, 0, 0))

    def whole(shape):
        nd = len(shape)
        return pl.BlockSpec((1,) + shape, lambda i, j: (i,) + (0,) * nd)

    out = pl.pallas_call(
        functools.partial(_nsa_prompt_kernel, seq=t),
        grid=(b, n_qt),
        in_specs=[q_spec, q_spec, whole((ncp, LANES)), whole((LANES, ncp)), whole((t, LANES)),
                  whole((n_kt, LANES, tk)), whole((t, LANES)), whole((n_kt, LANES, tk)),
                  pl.BlockSpec((1, 1, gt.shape[-1], tq), lambda i, j: (i, j, 0, 0))],
        out_specs=pl.BlockSpec((1, 1, GQA_RATIO, LANES, tq), lambda i, j: (i, j, 0, 0, 0)),
        out_shape=jax.ShapeDtypeStruct((b, n_qt, GQA_RATIO, LANES, tq), jnp.float32),
        scratch_shapes=[pltpu.VMEM((N_KV_HEADS, 1, cols), jnp.float32), pltpu.VMEM((N_KV_HEADS, 1, cols), jnp.float32),
                        pltpu.VMEM((N_KV_HEADS, LANES, cols), jnp.float32), pltpu.VMEM((LANES, cols), jnp.float32)],
        compiler_params=pltpu.CompilerParams(dimension_semantics=("arbitrary", "arbitrary")),
        name="nsa_prompt",
    )(qc, qr, kc, vct, ks, values_t(kv_sel), kw, values_t(kv_win), gtt)
    out = out.reshape(b, n_qt, GQA_RATIO, N_KV_HEADS, HEAD_DIM, tq).transpose(0, 1, 5, 3, 2, 4)
    return out.reshape(b, t, NSA_WIDTH)


KV_ROW = 2 * N_KV_HEADS * HEAD_DIM
SEG_W = CMP_STRIDE * KV_ROW
SAMPLE_PG = 32


def _sample_cmp_kernel(pt_ref, *refs, pg, n_pages, s_len):
    page_refs = refs[:pg]
    (new_ref, w1_ref, b1_ref, w2_ref, b2_ref, qct_ref, oc_ref, sel_ref, t_scr, pseg_scr) = refs[pg:]
    j = pl.program_id(1)
    n_steps = n_pages // pg
    seg_per_page = PAGE_SIZE // CMP_STRIDE
    m_rows = pg * seg_per_page
    past = n_pages * PAGE_SIZE
    n_seg = past // CMP_STRIDE
    rows = GQA_RATIO * s_len

    for i in range(pg):
        for c in range(2):
            t_scr[c, i * PAGE_SIZE:(i + 1) * PAGE_SIZE, :] = page_refs[i][0, c * LANES:(c + 1) * LANES, :].T
    for c in range(2):
        xc = jnp.concatenate([t_scr[c, pl.ds(r, m_rows, stride=CMP_STRIDE), :]
                              for r in range(CMP_STRIDE)], axis=1)
        xn = jnp.concatenate([new_ref[0, :, r * KV_ROW + c * LANES:r * KV_ROW + (c + 1) * LANES]
                              for r in range(CMP_STRIDE)], axis=1)
        xc = jnp.concatenate([xc, xn], axis=0).astype(jnp.bfloat16)
        pseg_scr[c, pl.ds(pl.multiple_of(j * m_rows, m_rows), m_rows + 8), :] = jnp.dot(
            xc, w1_ref[c], preferred_element_type=jnp.float32)

    @pl.when(j == n_steps - 1)
    def _():
        kv = []
        for c in range(2):
            acc = jnp.zeros((n_seg, LANES), jnp.float32) + b2_ref[c]
            for g in range(N_KV_HEADS):
                lo = g * N_HALF * CMP_HIDDEN
                hid = (pseg_scr[c, 0:n_seg, lo:lo + CMP_HIDDEN]
                       + pseg_scr[c, 1:n_seg + 1, lo + CMP_HIDDEN:lo + 2 * CMP_HIDDEN] + b1_ref[c])
                hid = jax.nn.gelu(hid)
                acc = acc + jnp.dot(hid.astype(jnp.bfloat16), w2_ref[c, g], preferred_element_type=jnp.float32)
            kv.append(acc)
        k_c, v_c = kv
        n_sb = past // SEL_BLOCK + -(-s_len // SEL_BLOCK)
        width = sel_ref.shape[-1]
        tok = lax.broadcasted_iota(jnp.int32, (n_seg, LANES), 1) % s_len
        blk_end = lax.broadcasted_iota(jnp.int32, (n_seg, LANES), 0) * CMP_STRIDE + (CMP_BLOCK - 1)
        cmask = blk_end <= past + tok
        s = jnp.dot(k_c, qct_ref[0], precision=HIGHEST, preferred_element_type=jnp.float32)
        s = jnp.where(cmask, s, NEG_INF)
        mx = jnp.max(s, axis=0, keepdims=True)
        p_t = jnp.where(cmask, jnp.exp(s - mx), 0.0)
        p_t = p_t / jnp.maximum(jnp.sum(p_t, axis=0, keepdims=True), 1e-30)
        o_c = jnp.dot(p_t.T.astype(jnp.bfloat16), v_c.astype(jnp.bfloat16), preferred_element_type=jnp.float32)
        ci = lax.broadcasted_iota(jnp.int32, (LANES, LANES), 0)
        cj = lax.broadcasted_iota(jnp.int32, (LANES, LANES), 1)
        same = ((ci // rows == cj // rows) & (ci % s_len == cj % s_len)).astype(jnp.float32)
        p_grp = jnp.dot(p_t, same, precision=HIGHEST, preferred_element_type=jnp.float32)
        score = _block_scores(p_grp, width, past + lax.broadcasted_iota(jnp.int32, (width, LANES), 1) % s_len)
        sel = _topk_cols(score, n_sb, min(N_SEL, n_sb)).T
        for g in range(N_KV_HEADS):
            oc_ref[0, g] = o_c[g * rows:(g + 1) * rows]
            sel_ref[0, g] = sel[g * rows:(g + 1) * rows]


def _sample_attn_kernel(pt_ref, *refs, pg, n_pages, s_len):
    page_refs = refs[:pg]
    (qr_ref, sel_ref, tail_ref, win_ref, wnew_ref, oc_ref, gt_ref, o_ref,
     k_scr, v_scr, m_scr, l_scr, acc_scr) = refs[pg:]
    j = pl.program_id(1)
    n_steps = n_pages // pg
    rows = GQA_RATIO * s_len
    keys = pg * PAGE_SIZE
    past = n_pages * PAGE_SIZE
    width = sel_ref.shape[-1]

    @pl.when(j == 0)
    def _():
        m_scr[...] = jnp.full(m_scr.shape, NEG_INF, jnp.float32)
        l_scr[...] = jnp.zeros(l_scr.shape, jnp.float32)
        acc_scr[...] = jnp.zeros(acc_scr.shape, jnp.float32)

    for i in range(pg):
        k_scr[:, i * PAGE_SIZE:(i + 1) * PAGE_SIZE] = page_refs[i][0, 0:LANES, :].astype(jnp.bfloat16)
        v_scr[:, i * PAGE_SIZE:(i + 1) * PAGE_SIZE] = page_refs[i][0, LANES:2 * LANES, :].astype(jnp.bfloat16)
    blk_of_key = (j * keys + lax.broadcasted_iota(jnp.int32, (width, keys), 1)) // SEL_BLOCK
    expand = (blk_of_key == lax.broadcasted_iota(jnp.int32, (width, keys), 0)).astype(jnp.bfloat16)
    for g in range(N_KV_HEADS):
        q = qr_ref[0, g]
        s = jnp.dot(q, k_scr[...], preferred_element_type=jnp.float32)
        mask = jnp.dot(sel_ref[0, g].astype(jnp.bfloat16), expand, preferred_element_type=jnp.float32) > 0.5
        s = jnp.where(mask, s, NEG_INF)
        m_old = m_scr[g]
        m_new = jnp.maximum(m_old, jnp.max(s, axis=-1, keepdims=True))
        alpha = jnp.exp(m_old - m_new)
        p = jnp.where(mask, jnp.exp(s - m_new), 0.0)
        l_scr[g] = alpha * l_scr[g] + jnp.sum(p, axis=-1, keepdims=True)
        acc_scr[g] = alpha * acc_scr[g] + _dot_nt(p.astype(jnp.bfloat16), v_scr[...])
        m_scr[g] = m_new

    @pl.when(j == n_steps - 1)
    def _():
        tok8 = lax.broadcasted_iota(jnp.int32, (rows, 8), 0) % s_len
        new_ok = lax.broadcasted_iota(jnp.int32, (rows, 8), 1) <= tok8
        wb = win_ref.shape[2]
        tokw = lax.broadcasted_iota(jnp.int32, (rows, wb), 0) % s_len
        kpos = past - wb + lax.broadcasted_iota(jnp.int32, (rows, wb), 1)
        win_ok = (kpos > past + tokw - WINDOW) & (kpos >= 0)
        gates = jax.nn.sigmoid(gt_ref[0])
        for g in range(N_KV_HEADS):
            q = qr_ref[0, g]
            tail_sel = sel_ref[0, g][:, past // SEL_BLOCK:past // SEL_BLOCK + 1] > 0.5
            t_mask = new_ok & tail_sel
            s_t = jnp.where(t_mask, _dot_nt(q, tail_ref[0, :, 0:LANES].astype(jnp.bfloat16)), NEG_INF)
            m_old = m_scr[g]
            m_new = jnp.maximum(m_old, jnp.max(s_t, axis=-1, keepdims=True))
            alpha = jnp.exp(m_old - m_new)
            p_t = jnp.where(t_mask, jnp.exp(s_t - m_new), 0.0)
            l_s = alpha * l_scr[g] + jnp.sum(p_t, axis=-1, keepdims=True)
            o_s = (alpha * acc_scr[g] + jnp.dot(p_t.astype(jnp.bfloat16),
                                                tail_ref[0, :, LANES:2 * LANES].astype(jnp.bfloat16),
                                                preferred_element_type=jnp.float32)) / jnp.maximum(l_s, 1e-30)
            s_w = jnp.where(win_ok, jnp.dot(q, win_ref[0, 0:LANES, :].astype(jnp.bfloat16),
                                            preferred_element_type=jnp.float32), NEG_INF)
            s_n = jnp.where(new_ok, _dot_nt(q, wnew_ref[0, :, 0:LANES].astype(jnp.bfloat16)), NEG_INF)
            mw = jnp.maximum(jnp.max(s_w, axis=-1, keepdims=True), jnp.max(s_n, axis=-1, keepdims=True))
            p_w = jnp.where(win_ok, jnp.exp(s_w - mw), 0.0)
            p_n = jnp.where(new_ok, jnp.exp(s_n - mw), 0.0)
            l_w = jnp.sum(p_w, axis=-1, keepdims=True) + jnp.sum(p_n, axis=-1, keepdims=True)
            o_w = (_dot_nt(p_w.astype(jnp.bfloat16), win_ref[0, LANES:2 * LANES, :].astype(jnp.bfloat16))
                   + jnp.dot(p_n.astype(jnp.bfloat16), wnew_ref[0, :, LANES:2 * LANES].astype(jnp.bfloat16),
                             preferred_element_type=jnp.float32)) / jnp.maximum(l_w, 1e-30)
            gl = gates[g]
            o_ref[0, g] = gl[:, 0:1] * oc_ref[0, g] + gl[:, 1:2] * o_s + gl[:, 2:3] * o_w


def _nsa_sample_pallas(q, q_rot, kv_cmp, kv_sel, kv_win, gt, cache_cmp, cache_sel, win_buf, page_table,
                       w1, b1, w2, b2):
    pg = SAMPLE_PG
    db, s_len = q.shape[0], q.shape[1]
    n_pages = page_table.shape[1]
    assert n_pages % pg == 0 and s_len <= 8 and s_len <= CMP_STRIDE and N_NSA_HEADS * s_len <= LANES
    past = n_pages * PAGE_SIZE
    n_seg = past // CMP_STRIDE
    n_sb = past // SEL_BLOCK + 1
    width = -(-n_sb // LANES) * LANES
    rows = GQA_RATIO * s_len
    seg_per_page = PAGE_SIZE // CMP_STRIDE

    def group_rows(a, dtype):
        a = (a * ATTN_SCALE).reshape(db, s_len, N_KV_HEADS, GQA_RATIO, HEAD_DIM).transpose(0, 2, 3, 1, 4)
        eye = jnp.eye(N_KV_HEADS, dtype=a.dtype)[None, :, None, None, :, None]
        return (a[:, :, :, :, None, :] * eye).reshape(db, N_KV_HEADS, rows, LANES).astype(dtype)

    def pad8(a):
        return jnp.pad(a.reshape(db, s_len, KV_ROW), ((0, 0), (0, 8 - s_len), (0, 0)))

    qg = (q * ATTN_SCALE).reshape(db, s_len, N_KV_HEADS, GQA_RATIO, HEAD_DIM).transpose(0, 2, 4, 3, 1)
    eye_q = jnp.eye(N_KV_HEADS, dtype=qg.dtype)[None, :, None, :, None]
    qct = (qg.reshape(db, N_KV_HEADS, HEAD_DIM, 1, rows) * eye_q).reshape(db, LANES, N_KV_HEADS * rows)
    qct = jnp.pad(qct, ((0, 0), (0, 0), (0, LANES - N_KV_HEADS * rows)))
    qr = group_rows(q_rot, jnp.bfloat16)
    new_seg = jnp.pad(kv_cmp.reshape(db, 1, s_len * KV_ROW), ((0, 0), (0, 7), (0, SEG_W - s_len * KV_ROW)))
    eye_g = jnp.eye(N_KV_HEADS, dtype=w1.dtype)
    w1t = w1.transpose(0, 2, 3, 1, 4)
    w1_bd = (w1t[:, :, None, :, None, :, :] * eye_g[None, None, :, None, :, None, None]).reshape(
        2, CMP_STRIDE * N_KV_HEADS * HEAD_DIM, N_KV_HEADS * N_HALF * CMP_HIDDEN).astype(jnp.bfloat16)
    w2_g = (w2[:, None, :, None, :] * eye_g[None, :, None, :, None]).reshape(
        2, N_KV_HEADS, CMP_HIDDEN, LANES).astype(jnp.bfloat16)
    b2_t = jnp.tile(b2, (1, N_KV_HEADS)).reshape(2, 1, LANES)
    b1_r = b1.reshape(2, 1, CMP_HIDDEN)

    def page_specs(block):
        return [pl.BlockSpec(block, functools.partial(lambda b, j, pt, i: (pt[b, j * pg + i], 0, 0), i=i))
                for i in range(pg)]

    def per_seq(shape):
        nd = len(shape)
        return pl.BlockSpec((1,) + shape, lambda b, j, pt: (b,) + (0,) * nd)

    def const(shape):
        nd = len(shape)
        return pl.BlockSpec(shape, lambda b, j, pt: (0,) * nd)

    def feature_major(a):
        return a.transpose(0, 2, 3, 4, 1).reshape(a.shape[0], KV_ROW, a.shape[1])

    page_block = (1, KV_ROW, PAGE_SIZE)
    o_c, sel = pl.pallas_call(
        functools.partial(_sample_cmp_kernel, pg=pg, n_pages=n_pages, s_len=s_len),
        grid_spec=pltpu.PrefetchScalarGridSpec(
            num_scalar_prefetch=1, grid=(db, n_pages // pg),
            in_specs=page_specs(page_block) + [
                per_seq((8, SEG_W)), const(w1_bd.shape), const(b1_r.shape), const(w2_g.shape), const(b2_t.shape),
                per_seq((LANES, LANES))],
            out_specs=[per_seq((N_KV_HEADS, rows, LANES)), per_seq((N_KV_HEADS, rows, width))],
            scratch_shapes=[pltpu.VMEM((2, pg * PAGE_SIZE, LANES), jnp.float32),
                            pltpu.VMEM((2, n_seg + 8, N_KV_HEADS * N_HALF * CMP_HIDDEN), jnp.float32)]),
        out_shape=[jax.ShapeDtypeStruct((db, N_KV_HEADS, rows, LANES), jnp.float32),
                   jax.ShapeDtypeStruct((db, N_KV_HEADS, rows, width), jnp.float32)],
        compiler_params=pltpu.CompilerParams(dimension_semantics=("arbitrary", "arbitrary"),
                                             vmem_limit_bytes=56 * 1024 * 1024),
        name="sample_cmp_select",
    )(page_table, *([feature_major(cache_cmp)] * pg), new_seg, w1_bd, b1_r, w2_g, b2_t, qct)

    gl = jnp.pad(gt.astype(jnp.float32).reshape(db, s_len, N_KV_HEADS, GQA_RATIO, 3).transpose(0, 2, 3, 1, 4)
                 .reshape(db, N_KV_HEADS, rows, 3), ((0, 0), (0, 0), (0, 0), (0, LANES - 3)))
    wb = win_buf.shape[1]
    out = pl.pallas_call(
        functools.partial(_sample_attn_kernel, pg=pg, n_pages=n_pages, s_len=s_len),
        grid_spec=pltpu.PrefetchScalarGridSpec(
            num_scalar_prefetch=1, grid=(db, n_pages // pg),
            in_specs=page_specs(page_block) + [
                per_seq((N_KV_HEADS, rows, LANES)), per_seq((N_KV_HEADS, rows, width)), per_seq((8, KV_ROW)),
                per_seq((KV_ROW, wb)), per_seq((8, KV_ROW)), per_seq((N_KV_HEADS, rows, LANES)),
                per_seq((N_KV_HEADS, rows, LANES))],
            out_specs=per_seq((N_KV_HEADS, rows, LANES)),
            scratch_shapes=[pltpu.VMEM((LANES, pg * PAGE_SIZE), jnp.bfloat16),
                            pltpu.VMEM((LANES, pg * PAGE_SIZE), jnp.bfloat16),
                            pltpu.VMEM((N_KV_HEADS, rows, 1), jnp.float32),
                            pltpu.VMEM((N_KV_HEADS, rows, 1), jnp.float32),
                            pltpu.VMEM((N_KV_HEADS, rows, LANES), jnp.float32)]),
        out_shape=jax.ShapeDtypeStruct((db, N_KV_HEADS, rows, LANES), jnp.float32),
        compiler_params=pltpu.CompilerParams(dimension_semantics=("arbitrary", "arbitrary")),
        name="sample_sel_win_attn",
    )(page_table, *([feature_major(cache_sel)] * pg), qr, sel, pad8(kv_sel),
      feature_major(win_buf), pad8(kv_win), o_c, gl)
    out = out.reshape(db, N_KV_HEADS, GQA_RATIO, s_len, N_KV_HEADS, HEAD_DIM)
    out = jnp.stack([out[:, g, :, :, g, :] for g in range(N_KV_HEADS)], axis=1)
    return out.transpose(0, 3, 1, 2, 4).reshape(db, s_len, NSA_WIDTH)


ROUTER_TT = 256
MOE_BM = 256


def _router_kernel(x_ref, mix_ref, wout_ref, g_ref, wrt_ref, br_ref,
                   x1_ref, h_ref, eidx_ref, gate_ref, rank_ref, cnt_ref, wout_bf, run_scr):
    tt = x_ref.shape[0]
    n_e = wrt_ref.shape[0]

    @pl.when(pl.program_id(0) == 0)
    def _():
        wout_bf[...] = wout_ref[...].astype(jnp.bfloat16)
        run_scr[...] = jnp.zeros_like(run_scr)

    x1 = x_ref[...] + jnp.dot(mix_ref[...].astype(jnp.bfloat16), wout_bf[...], preferred_element_type=jnp.float32)
    x1_ref[...] = x1
    hn = x1 * lax.rsqrt(jnp.mean(x1 * x1, axis=-1, keepdims=True) + RMS_EPS) * g_ref[...]
    h_ref[...] = hn

    score = _dot_nt(wrt_ref[...], hn, HIGHEST) + br_ref[...]
    ef = lax.broadcasted_iota(jnp.int32, (n_e, tt), 0).astype(jnp.float32)
    vals, hits = [], []
    for k in range(TOP_K):
        best = jnp.max(score, axis=0, keepdims=True)
        first = jnp.min(jnp.where(score == best, ef, float(n_e)), axis=0, keepdims=True)
        hit = ef == first
        vals.append(best)
        hits.append(hit)
        eidx_ref[k:k + 1, :] = first.astype(jnp.int32)
        score = jnp.where(hit, -3e38, score)
    exps = [jnp.exp(v - vals[0]) for v in vals]
    denom = sum(exps[1:], exps[0])
    for k in range(TOP_K):
        gate_ref[k:k + 1, :] = exps[k] / denom

    chosen = functools.reduce(jnp.logical_or, hits)
    before = (lax.broadcasted_iota(jnp.int32, (tt, tt), 0) < lax.broadcasted_iota(jnp.int32, (tt, tt), 1))
    earlier = jnp.dot(chosen.astype(jnp.bfloat16), before.astype(jnp.bfloat16), preferred_element_type=jnp.float32)
    pos = earlier + run_scr[...]
    for k in range(TOP_K):
        rank_ref[k:k + 1, :] = jnp.sum(jnp.where(hits[k], pos, 0.0), axis=0, keepdims=True).astype(jnp.int32)
    run_scr[...] = run_scr[...] + jnp.sum(chosen.astype(jnp.float32), axis=1, keepdims=True)
    cnt_ref[...] = jnp.broadcast_to(run_scr[...], cnt_ref.shape).astype(jnp.int32)


def _router_pallas(x, mix, w_out, g_ffn, w_router, b_router):
    n, d = x.shape
    c = mix.shape[1]
    n_e = w_router.shape[1]
    tt = ROUTER_TT
    assert n % tt == 0

    def row(w):
        return pl.BlockSpec((tt, w), lambda i: (i, 0))

    def full(a, b):
        return pl.BlockSpec((a, b), lambda i: (0, 0))

    k4 = pl.BlockSpec((TOP_K, tt), lambda i: (0, i))
    return pl.pallas_call(
        _router_kernel,
        grid=(n // tt,),
        in_specs=[row(d), row(c), full(c, d), full(1, d), full(n_e, d), full(n_e, 1)],
        out_specs=[row(d), row(d), k4, k4, k4, full(n_e, LANES)],
        out_shape=[jax.ShapeDtypeStruct((n, d), jnp.float32), jax.ShapeDtypeStruct((n, d), jnp.float32),
                   jax.ShapeDtypeStruct((TOP_K, n), jnp.int32), jax.ShapeDtypeStruct((TOP_K, n), jnp.float32),
                   jax.ShapeDtypeStruct((TOP_K, n), jnp.int32), jax.ShapeDtypeStruct((n_e, LANES), jnp.int32)],
        scratch_shapes=[pltpu.VMEM((c, d), jnp.bfloat16), pltpu.VMEM((n_e, 1), jnp.float32)],
        compiler_params=pltpu.CompilerParams(dimension_semantics=("arbitrary",)),
        name="outproj_router",
    )(x, mix, w_out, g_ffn.reshape(1, d), w_router.T, b_router.reshape(n_e, 1))


def _expert_kernel(blk_e_ref, n_used_ref, x_ref, wup_ref, bup_ref, wdn_ref, bdn_ref, o_ref, wup_bf, wdn_bf):
    i = pl.program_id(0)
    d_ff = wdn_ref.shape[1]

    @pl.when(i < n_used_ref[0])
    def _():
        e = blk_e_ref[i]
        prev = blk_e_ref[jnp.maximum(i - 1, 0)]

        @pl.when((i == 0) | (e != prev))
        def _():
            wup_bf[...] = wup_ref[0].astype(jnp.bfloat16)
            wdn_bf[...] = wdn_ref[0].astype(jnp.bfloat16)

        u = jnp.dot(x_ref[...].astype(jnp.bfloat16), wup_bf[...], preferred_element_type=jnp.float32) + bup_ref[0]
        glu = jnp.minimum(u[:, :d_ff], SWIGLU_LIMIT)
        lin = jnp.clip(u[:, d_ff:], -SWIGLU_LIMIT, SWIGLU_LIMIT)
        a = glu * jax.nn.sigmoid(SWIGLU_ALPHA * glu) * (lin + 1.0)
        o_ref[...] = jnp.dot(a.astype(jnp.bfloat16), wdn_bf[...], preferred_element_type=jnp.float32) + bdn_ref[0]

    @pl.when(i >= n_used_ref[0])
    def _():
        o_ref[...] = jnp.zeros_like(o_ref)


def _expert_vmem_bytes(bm, d, f2, d_ff):
    weights = 2 * 4 * (d * f2 + d_ff * d) + 2 * (d * f2 + d_ff * d)
    rows = 2 * bm * d * (4 + 4)
    temps = bm * f2 * 4 * 2 + bm * d_ff * (4 + 2)
    return weights + rows + temps


def _experts_pallas(xs, blk_e, n_used, w_up, b_up, w_down, b_down, bm):
    n_slots, d = xs.shape
    n_e, _, f2 = w_up.shape
    d_ff = w_down.shape[1]
    grid_spec = pltpu.PrefetchScalarGridSpec(
        num_scalar_prefetch=2,
        grid=(n_slots // bm,),
        in_specs=[pl.BlockSpec((bm, d), lambda i, be, nu: (i, 0)),
                  pl.BlockSpec((1, d, f2), lambda i, be, nu: (be[i], 0, 0)),
                  pl.BlockSpec((1, 1, f2), lambda i, be, nu: (be[i], 0, 0)),
                  pl.BlockSpec((1, d_ff, d), lambda i, be, nu: (be[i], 0, 0)),
                  pl.BlockSpec((1, 1, d), lambda i, be, nu: (be[i], 0, 0))],
        out_specs=pl.BlockSpec((bm, d), lambda i, be, nu: (i, 0)),
        scratch_shapes=[pltpu.VMEM((d, f2), jnp.bfloat16), pltpu.VMEM((d_ff, d), jnp.bfloat16)],
    )
    vmem_limit = _expert_vmem_bytes(bm, d, f2, d_ff) * 5 // 4
    return pl.pallas_call(
        _expert_kernel,
        grid_spec=grid_spec,
        out_shape=jax.ShapeDtypeStruct((n_slots, d), jnp.float32),
        compiler_params=pltpu.CompilerParams(dimension_semantics=("arbitrary",), vmem_limit_bytes=vmem_limit),
        name="expert_mlp",
    )(blk_e, n_used, xs, w_up, b_up.reshape(n_e, 1, f2), w_down, b_down.reshape(n_e, 1, d))


def _combine_norm_kernel(x1_ref, og_ref, gate_ref, g_ref, o_ref):
    y = x1_ref[...]
    gates = gate_ref[...]
    for k in range(TOP_K):
        y = y + gates[:, k:k + 1] * og_ref[k]
    o_ref[...] = y * lax.rsqrt(jnp.mean(y * y, axis=-1, keepdims=True) + RMS_EPS) * g_ref[...]


def _combine_norm_pallas(x1, og, gate_t, g_final):
    n, d = x1.shape
    tt = ROUTER_TT
    return pl.pallas_call(
        _combine_norm_kernel,
        grid=(n // tt,),
        in_specs=[pl.BlockSpec((tt, d), lambda i: (i, 0)), pl.BlockSpec((TOP_K, tt, d), lambda i: (0, i, 0)),
                  pl.BlockSpec((tt, TOP_K), lambda i: (i, 0)), pl.BlockSpec((1, d), lambda i: (0, 0))],
        out_specs=pl.BlockSpec((tt, d), lambda i: (i, 0)),
        out_shape=jax.ShapeDtypeStruct((n, d), jnp.float32),
        compiler_params=pltpu.CompilerParams(dimension_semantics=("arbitrary",)),
        name="combine_final_norm",
    )(x1, og, gate_t, g_final.reshape(1, d))


def _finish_pallas(x, mix, w_out, g_ffn, w_router, b_router, w_up, b_up, w_down, b_down, g_final):
    n, d = x.shape
    n_e = w_router.shape[1]
    bm = MOE_BM
    x1, h, eidx, gate, rank, cnt = _router_pallas(x, mix, w_out, g_ffn, w_router, b_router)
    counts = cnt[:, 0]
    padded = (counts + bm - 1) // bm * bm
    pad_end = jnp.cumsum(padded)
    gstart = pad_end - padded
    pick = eidx[None] == jnp.arange(n_e, dtype=jnp.int32)[:, None, None]
    dest = jnp.sum(jnp.where(pick, gstart[:, None, None], 0), axis=0) + rank
    nb = -(-(n * TOP_K) // bm) + n_e
    blk_start = jnp.arange(nb, dtype=jnp.int32) * bm
    blk_e = jnp.minimum(jnp.sum((pad_end[None, :] <= blk_start[:, None]).astype(jnp.int32), axis=1), n_e - 1)
    n_used = (pad_end[-1] // bm).astype(jnp.int32).reshape(1)
    tok = jnp.broadcast_to(jnp.arange(n, dtype=jnp.int32)[None, :], (TOP_K, n))
    src = jnp.zeros((nb * bm,), jnp.int32).at[dest.reshape(-1)].set(tok.reshape(-1))
    out = _experts_pallas(h[src], blk_e, n_used, w_up, b_up, w_down, b_down, bm)
    return _combine_norm_pallas(x1, out[dest], gate.T, g_final)


def kernel(x_prompt, x_sample, cache_cmp_kv, cache_sel_kv, cache_win_kv, state_ret, page_table, g_attn, w_in, w_cmp1, b_cmp1, w_cmp2, b_cmp2, w_out, g_ffn, w_router, b_router, w_up, b_up, w_down, b_down, g_final):
    seq = x_prompt.shape[1]
    past = page_table.shape[1] * PAGE_SIZE
    pos_p = jnp.arange(seq, dtype=jnp.int32)
    pos_s = past + jnp.arange(x_sample.shape[1], dtype=jnp.int32)
    log_g = _ret_log_decay()
    assert DEPTH == 1
    l = 0
    d = x_prompt.shape[-1]
    q, kv_cmp_p, kv_sel_p, kv_win_p, gt, r_q, r_k, r_v, r_g = _mixer_inputs(x_prompt, pos_p, g_attn[l], w_in[l])
    blocks = _compress_blocks(_segment_proj(kv_cmp_p, w_cmp1[l]), b_cmp1[l], w_cmp2[l], b_cmp2[l])
    o_nsa = _nsa_prompt_pallas(q, _rope(q, pos_p), blocks, kv_sel_p, kv_win_p, gt)
    o_ret, ret_p = _ret_prompt_pallas(r_q, r_k, r_v, r_g, log_g)
    mix_p = jnp.concatenate([o_nsa, o_ret], axis=-1)
    q, kv_cmp_s, kv_sel_s, kv_win_s, gt, r_q, r_k, r_v, r_g = _mixer_inputs(x_sample, pos_s, g_attn[l], w_in[l])
    o_nsa_s = _nsa_sample_pallas(q, _rope(q, pos_s), kv_cmp_s, kv_sel_s, kv_win_s, gt, cache_cmp_kv[:, l],
                                 cache_sel_kv[:, l], cache_win_kv[:, l], page_table,
                                 w_cmp1[l], b_cmp1[l], w_cmp2[l], b_cmp2[l])
    win_s = jnp.concatenate([cache_win_kv[:, l], kv_win_s], axis=1)[:, x_sample.shape[1]:]
    ret_s, o_r = _ret_chunk(state_ret[:, l].astype(jnp.float32), r_q.astype(jnp.float32), r_k.astype(jnp.float32), r_v.astype(jnp.float32), log_g)
    mix_s = jnp.concatenate([o_nsa_s, _ret_out(o_r, r_g)], axis=-1)
    n_p = x_prompt.shape[0] * seq
    x_all = jnp.concatenate([x_prompt.reshape(n_p, d), x_sample.reshape(-1, d)], axis=0)
    mix_all = jnp.concatenate([mix_p.reshape(n_p, MIX_WIDTH), mix_s.reshape(-1, MIX_WIDTH)], axis=0)
    y = _finish_pallas(x_all, mix_all, w_out[l], g_ffn[l], w_router[l], b_router[l], w_up[l], b_up[l], w_down[l], b_down[l], g_final)
    y_prompt = y[:n_p].reshape(x_prompt.shape)
    y_sample = y[n_p:].reshape(x_sample.shape)
    win_p = kv_win_p[:, seq - min(WINDOW, seq):]
    return (y_prompt, y_sample, kv_cmp_p[:, None], kv_sel_p[:, None], win_p[:, None], ret_p[:, None],
            kv_cmp_s[:, None], kv_sel_s[:, None], win_s[:, None], ret_s[:, None])
```

```python
import functools
import jax, jax.numpy as jnp
from jax import lax
import numpy as np
from jax.experimental import pallas as pl
from jax.experimental.pallas import tpu as pltpu

DEPTH = 1
PAGE_SIZE = 128

HEAD_DIM = 64
N_NSA_HEADS = 8
N_KV_HEADS = 2
GQA_RATIO = N_NSA_HEADS // N_KV_HEADS
CMP_BLOCK = 32
CMP_STRIDE = 16
N_HALF = CMP_BLOCK // CMP_STRIDE
CMP_HIDDEN = 256
SEL_BLOCK = 64
N_SEL = 16
WINDOW = 512
N_RET_HEADS = 4
RET_DK = 64
RET_DV = 128
RET_CHUNK = 128
TOP_K = 4
SWIGLU_ALPHA = 1.702
SWIGLU_LIMIT = 7.0
ROPE_THETA = 10000.0
RMS_EPS = 1e-5
NEG_INF = -1e30
FORCED_SCORE = 1e6
INVALID_SCORE = -1e9
NSA_WIDTH = N_NSA_HEADS * HEAD_DIM
KV_WIDTH = N_KV_HEADS * HEAD_DIM
RET_QK_WIDTH = N_RET_HEADS * RET_DK
RET_WIDTH = N_RET_HEADS * RET_DV
IN_SIZES = (NSA_WIDTH, KV_WIDTH, KV_WIDTH, KV_WIDTH, KV_WIDTH, KV_WIDTH, KV_WIDTH, 3 * N_NSA_HEADS, RET_QK_WIDTH, RET_QK_WIDTH, RET_WIDTH, RET_WIDTH)
MIX_WIDTH = NSA_WIDTH + RET_WIDTH
ATTN_SCALE = HEAD_DIM ** -0.5

LANES = 128
HIGHEST = lax.Precision.HIGHEST


def _rmsnorm(x, g):
    xf = x.astype(jnp.float32)
    y = xf * lax.rsqrt(jnp.mean(xf * xf, axis=-1, keepdims=True) + RMS_EPS)
    return (y * g.astype(jnp.float32)).astype(x.dtype)


def _rope(x, pos):
    half = x.shape[-1] // 2
    inv = ROPE_THETA ** (-jnp.arange(half, dtype=jnp.float32) / half)
    ang = pos.astype(jnp.float32)[:, None] * inv[None, :]
    cos = jnp.cos(ang)[None, :, None, :]
    sin = jnp.sin(ang)[None, :, None, :]
    xf = x.astype(jnp.float32)
    x1, x2 = xf[..., :half], xf[..., half:]
    return jnp.concatenate([x1 * cos - x2 * sin, x2 * cos + x1 * sin], axis=-1).astype(x.dtype)


def _mixer_inputs(x, pos, g_attn, w_in):
    b, t = x.shape[0], x.shape[1]
    h = _rmsnorm(x, g_attn)
    proj = jnp.einsum('btd,dc->btc', h, w_in)
    offs = [int(o) for o in np.cumsum(np.array(IN_SIZES))[:-1]]
    q, kc, vc, ks, vs, kw, vw, gt, rq, rk, rv, rg = jnp.split(proj, offs, axis=-1)

    def heads(a, n, d):
        return a.reshape(b, t, n, d)

    def kvh(a):
        return heads(a, N_KV_HEADS, HEAD_DIM)

    kv_cmp = jnp.stack([kvh(kc), kvh(vc)], axis=2)
    kv_sel = jnp.stack([_rope(kvh(ks), pos), kvh(vs)], axis=2)
    kv_win = jnp.stack([_rope(kvh(kw), pos), kvh(vw)], axis=2)
    r_q = _rope(heads(rq, N_RET_HEADS, RET_DK), pos)
    r_k = _rope(heads(rk, N_RET_HEADS, RET_DK), pos) * (RET_DK ** -0.5)
    r_v = heads(rv, N_RET_HEADS, RET_DV)
    return heads(q, N_NSA_HEADS, HEAD_DIM), kv_cmp, kv_sel, kv_win, gt, r_q, r_k, r_v, rg


def _segment_proj(kv_rows, w1):
    b, l = kv_rows.shape[0], kv_rows.shape[1]
    seg = kv_rows.reshape(b, l // CMP_STRIDE, CMP_STRIDE, 2, N_KV_HEADS, HEAD_DIM)
    return jnp.einsum('bsrcgd,chrdk->bscghk', seg, w1)


def _compress_blocks(p_seg, b1, w2, b2):
    n_cmp = p_seg.shape[1] - N_HALF + 1
    hid = sum(p_seg[:, h:h + n_cmp, :, :, h, :] for h in range(N_HALF)) + b1[None, None, :, None, :]
    hid = jax.nn.gelu(hid.astype(jnp.float32))
    out = jnp.einsum('bncgk,ckd->bncgd', hid, w2.astype(jnp.float32))
    return out + b2.astype(jnp.float32)[None, None, :, None, :]


def _ret_log_decay():
    return jnp.log1p(-jnp.exp2(-5.0 - jnp.arange(N_RET_HEADS, dtype=jnp.float32)))


def _ret_chunk(state, q, k, v, log_g):
    c = q.shape[1]
    i = jnp.arange(c, dtype=jnp.float32)
    diff = i[:, None] - i[None, :]
    decay = jnp.where(diff[None] >= 0, jnp.exp(log_g[:, None, None] * jnp.maximum(diff, 0.0)[None]), 0.0)
    scores = jnp.einsum('bihd,bjhd->bhij', q, k) * decay[None]
    inner = jnp.einsum('bhij,bjhv->bihv', scores, v)
    q_dec = jnp.exp(log_g[None, :] * (i[:, None] + 1.0))
    cross = jnp.einsum('bihd,bhdv->bihv', q * q_dec[None, :, :, None], state)
    k_dec = jnp.exp(log_g[None, :] * (c - 1.0 - i[:, None]))
    new_state = jnp.exp(log_g * c)[None, :, None, None] * state + jnp.einsum('bjhd,bjhv->bhdv', k * k_dec[None, :, :, None], v)
    return new_state, inner + cross


def _ret_prompt_kernel(lg_ref, q_ref, k_ref, kt_ref, v_ref, rg_ref, o_ref, s_ref, state):
    c = q_ref.shape[2]

    @pl.when(pl.program_id(1) == 0)
    def _():
        state[...] = jnp.zeros_like(state)

    ii = lax.broadcasted_iota(jnp.int32, (c, c), 0)
    jj = lax.broadcasted_iota(jnp.int32, (c, c), 1)
    diff = (ii - jj).astype(jnp.float32)
    pos_col = lax.broadcasted_iota(jnp.int32, (c, 1), 0).astype(jnp.float32)
    pos_row = lax.broadcasted_iota(jnp.int32, (1, c), 1).astype(jnp.float32)
    for h in range(N_RET_HEADS):
        lg = lg_ref[h]
        q, k, v = q_ref[0, h], k_ref[0, h], v_ref[0, h].astype(jnp.bfloat16)
        decay = jnp.where(diff >= 0, jnp.exp(lg * jnp.maximum(diff, 0.0)), 0.0)
        scores = _dot_nt(q.astype(jnp.bfloat16), k.astype(jnp.bfloat16)) * decay
        inner = jnp.dot(scores.astype(jnp.bfloat16), v, preferred_element_type=jnp.float32)
        q_dec = q * jnp.exp(lg * (pos_col + 1.0))
        cross = jnp.dot(q_dec.astype(jnp.bfloat16), state[h].astype(jnp.bfloat16), preferred_element_type=jnp.float32)
        kt_dec = kt_ref[0, h] * jnp.exp(lg * (c - 1.0 - pos_row))
        state[h] = jnp.exp(lg * c) * state[h] + jnp.dot(kt_dec.astype(jnp.bfloat16), v,
                                                        preferred_element_type=jnp.float32)
        o = inner + cross
        o = o * lax.rsqrt(jnp.mean(o * o, axis=-1, keepdims=True) + RMS_EPS)
        o_ref[0, :, h * RET_DV:(h + 1) * RET_DV] = o * jax.nn.silu(rg_ref[0, :, h * RET_DV:(h + 1) * RET_DV])
    s_ref[0] = state[...]


def _ret_prompt_pallas(r_q, r_k, r_v, r_g, log_g):
    b, t = r_q.shape[0], r_q.shape[1]
    c = RET_CHUNK
    assert t % c == 0
    qh = r_q.transpose(0, 2, 1, 3)
    kh = r_k.transpose(0, 2, 1, 3)
    kt = r_k.transpose(0, 2, 3, 1)
    vh = r_v.transpose(0, 2, 1, 3)

    def chunk(d):
        return pl.BlockSpec((1, N_RET_HEADS, c, d), lambda i, j, lg: (i, 0, j, 0))

    return pl.pallas_call(
        _ret_prompt_kernel,
        grid_spec=pltpu.PrefetchScalarGridSpec(
            num_scalar_prefetch=1, grid=(b, t // c),
            in_specs=[chunk(RET_DK), chunk(RET_DK),
                      pl.BlockSpec((1, N_RET_HEADS, RET_DK, c), lambda i, j, lg: (i, 0, 0, j)),
                      chunk(RET_DV), pl.BlockSpec((1, c, RET_WIDTH), lambda i, j, lg: (i, j, 0))],
            out_specs=[pl.BlockSpec((1, c, RET_WIDTH), lambda i, j, lg: (i, j, 0)),
                       pl.BlockSpec((1, N_RET_HEADS, RET_DK, RET_DV), lambda i, j, lg: (i, 0, 0, 0))],
            scratch_shapes=[pltpu.VMEM((N_RET_HEADS, RET_DK, RET_DV), jnp.float32)]),
        out_shape=[jax.ShapeDtypeStruct((b, t, RET_WIDTH), jnp.float32),
                   jax.ShapeDtypeStruct((b, N_RET_HEADS, RET_DK, RET_DV), jnp.float32)],
        compiler_params=pltpu.CompilerParams(dimension_semantics=("arbitrary", "arbitrary")),
        name="retention_prompt",
    )(log_g, qh, kh, kt, vh, r_g)


def _ret_out(o, r_g):
    b, t = o.shape[0], o.shape[1]
    o = o * lax.rsqrt(jnp.mean(o * o, axis=-1, keepdims=True) + RMS_EPS)
    return o.reshape(b, t, RET_WIDTH) * jax.nn.silu(r_g.astype(jnp.float32))


def _dot_nt(a, b, precision=None):
    return lax.dot_general(a, b, (((1,), (1,)), ((), ())), precision=precision, preferred_element_type=jnp.float32)


def _topk_cols(score, n_valid, k):
    height, width = score.shape
    jf = lax.broadcasted_iota(jnp.int32, (height, width), 0).astype(jnp.float32)
    score = jnp.where(jf < n_valid, score, -3e38)
    sel = jnp.zeros((height, width), jnp.float32)
    for _ in range(k):
        best = jnp.max(score, axis=0, keepdims=True)
        first = jnp.min(jnp.where(score == best, jf, float(height)), axis=0, keepdims=True)
        hit = jf == first
        sel = jnp.where(hit, 1.0, sel)
        score = jnp.where(hit, -3e38, score)
    return sel


def _block_scores(p_grp, n_blocks, q_pos):
    n_cmp = p_grp.shape[0]
    ratio = SEL_BLOCK // CMP_STRIDE
    jj = lax.broadcasted_iota(jnp.int32, (n_blocks, n_cmp), 0)
    nn = lax.broadcasted_iota(jnp.int32, (n_blocks, n_cmp), 1)
    overlap = ((nn >= ratio * jj - (N_HALF - 1)) & (nn <= ratio * jj + ratio - 1)).astype(jnp.float32)
    score = jnp.dot(overlap, p_grp, precision=HIGHEST, preferred_element_type=jnp.float32)
    jb = lax.broadcasted_iota(jnp.int32, q_pos.shape, 0)
    cur = q_pos // SEL_BLOCK
    forced = (jb == 0) | (jb == cur) | (jb == cur - 1)
    valid = jb * SEL_BLOCK <= q_pos
    score = jnp.where(forced, FORCED_SCORE, score)
    return jnp.where(valid, score, INVALID_SCORE)


NSA_TQ = 128
NSA_TK = 512


def _nsa_prompt_kernel(qc_ref, qr_ref, kc_ref, vct_ref, ks_ref, vst_ref, kw_ref, vwt_ref, gt_ref, o_ref,
                       m_scr, l_scr, acc_scr, oc_scr, *, seq):
    tq, tk = NSA_TQ, NSA_TK
    cols = GQA_RATIO * tq
    qt = pl.program_id(1)
    q0 = qt * tq
    ncp = kc_ref.shape[1]
    n_sb = seq // SEL_BLOCK
    gates = jax.nn.sigmoid(gt_ref[0, 0])

    def heads(a):
        return jnp.concatenate([a] * GQA_RATIO, axis=1)

    kpos_i = lax.broadcasted_iota(jnp.int32, (tk, tq), 0)
    qpos = q0 + lax.broadcasted_iota(jnp.int32, (tk, tq), 1)

    def attend(k_ref, vt_ref, lo, hi, keep_fn):
        m_scr[...] = jnp.full(m_scr.shape, NEG_INF, jnp.float32)
        l_scr[...] = jnp.zeros(l_scr.shape, jnp.float32)
        acc_scr[...] = jnp.zeros(acc_scr.shape, jnp.float32)

        def body(kt, carry):
            k0 = pl.multiple_of(kt * tk, tk)
            k = k_ref[0, pl.ds(k0, tk), :]
            vt = vt_ref[0, kt]
            keeps = keep_fn(k0)
            for g in range(N_KV_HEADS):
                keep = keeps[g]
                s = jnp.dot(k, qr_ref[0, g, 0], preferred_element_type=jnp.float32)
                s = s + heads((keep - 1.0) * -NEG_INF)
                m_old = m_scr[g]
                m_new = jnp.maximum(m_old, jnp.max(s, axis=0, keepdims=True))
                alpha = jnp.exp(m_old - m_new)
                p = jnp.exp(s - m_new) * heads(keep)
                l_scr[g] = alpha * l_scr[g] + jnp.sum(p, axis=0, keepdims=True)
                acc_scr[g] = alpha * acc_scr[g] + jnp.dot(vt, p.astype(jnp.bfloat16),
                                                          preferred_element_type=jnp.float32)
                m_scr[g] = m_new
            return carry

        lax.fori_loop(lo, hi, body, 0)
        return [acc_scr[g, g * HEAD_DIM:(g + 1) * HEAD_DIM, :] / jnp.maximum(l_scr[g], 1e-30)
                for g in range(N_KV_HEADS)]

    sel_bf = []
    for g in range(N_KV_HEADS):
        s = jnp.dot(kc_ref[0], qc_ref[0, g, 0], precision=HIGHEST, preferred_element_type=jnp.float32)
        blk_end = lax.broadcasted_iota(jnp.int32, (ncp, tq), 0) * CMP_STRIDE + (CMP_BLOCK - 1)
        ckeep = heads(jnp.where(blk_end <= q0 + lax.broadcasted_iota(jnp.int32, (ncp, tq), 1), 1.0, 0.0)) > 0.5
        s = jnp.where(ckeep, s, NEG_INF)
        mx = jnp.max(s, axis=0, keepdims=True)
        p = jnp.where(ckeep, jnp.exp(s - mx), 0.0)
        p = p / jnp.maximum(jnp.sum(p, axis=0, keepdims=True), 1e-30)
        o_c = jnp.dot(vct_ref[0].astype(jnp.bfloat16), p.astype(jnp.bfloat16),
                      preferred_element_type=jnp.float32)
        p_grp = sum(p[:, r * tq:(r + 1) * tq] for r in range(GQA_RATIO))

        score = _block_scores(p_grp, n_sb, q0 + lax.broadcasted_iota(jnp.int32, (n_sb, tq), 1))
        sel_t = _topk_cols(score, n_sb, min(N_SEL, n_sb))
        if n_sb < LANES:
            sel_t = jnp.concatenate([sel_t, jnp.zeros((LANES - n_sb, tq), jnp.float32)], axis=0)
        sel_bf.append(sel_t.astype(jnp.bfloat16))
        oc_scr[g * HEAD_DIM:(g + 1) * HEAD_DIM, :] = o_c[g * HEAD_DIM:(g + 1) * HEAD_DIM, :]

    def sel_keep(k0):
        blk_of_key = (k0 + lax.broadcasted_iota(jnp.int32, (tk, LANES), 0)) // SEL_BLOCK
        expand = (blk_of_key == lax.broadcasted_iota(jnp.int32, (tk, LANES), 1)).astype(jnp.bfloat16)
        causal = k0 + kpos_i <= qpos
        return [jnp.where((jnp.dot(expand, sel_bf[g], preferred_element_type=jnp.float32) > 0.5) & causal, 1.0, 0.0)
                for g in range(N_KV_HEADS)]

    def win_keep(k0):
        kpos = k0 + kpos_i
        return [jnp.where((kpos <= qpos) & (kpos > qpos - WINDOW), 1.0, 0.0)] * N_KV_HEADS

    hi = (q0 + tq + tk - 1) // tk
    o_s = attend(ks_ref, vst_ref, 0, hi, sel_keep)
    o_w = attend(kw_ref, vwt_ref, jnp.maximum(q0 - WINDOW, 0) // tk, hi, win_keep)

    for g in range(N_KV_HEADS):
        rows_g = slice(g * HEAD_DIM, (g + 1) * HEAD_DIM)
        for r in range(GQA_RATIO):
            col = (g * GQA_RATIO + r) * 3
            sl = slice(r * tq, (r + 1) * tq)
            comb = (gates[col:col + 1, :] * oc_scr[rows_g, sl] + gates[col + 1:col + 2, :] * o_s[g][:, sl]
                    + gates[col + 2:col + 3, :] * o_w[g][:, sl])
            o_ref[0, 0, r, rows_g, :] = comb


def _nsa_prompt_pallas(q, q_rot, blocks, kv_sel, kv_win, gt):
    b, t = q.shape[0], q.shape[1]
    tq, tk = NSA_TQ, NSA_TK
    assert t % tq == 0 and t % tk == 0 and t // SEL_BLOCK <= LANES and gt.shape[-1] % 8 == 0
    ncp = t // CMP_STRIDE
    n_qt, n_kt = t // tq, t // tk
    cols = GQA_RATIO * tq

    def group_pad_t(a, dtype):
        a = (a * ATTN_SCALE).reshape(b, n_qt, tq, N_KV_HEADS, GQA_RATIO, HEAD_DIM).transpose(0, 3, 1, 5, 4, 2)
        eye = jnp.eye(N_KV_HEADS, dtype=a.dtype)[None, :, None, :, None, None, None]
        return (a[:, :, :, None] * eye).reshape(b, N_KV_HEADS, n_qt, LANES, cols).astype(dtype)

    def values_t(kv):
        v = kv[:, :, 1].astype(jnp.bfloat16).reshape(b, n_kt, tk, LANES)
        return v.transpose(0, 1, 3, 2)

    qc = group_pad_t(q, jnp.float32)
    qr = group_pad_t(q_rot, jnp.bfloat16)
    blk = jnp.pad(blocks, ((0, 0), (0, ncp - blocks.shape[1]), (0, 0), (0, 0), (0, 0))).reshape(b, ncp, 2, LANES)
    kc = blk[:, :, 0]
    vct = blk[:, :, 1].transpose(0, 2, 1)
    ks = kv_sel[:, :, 0].astype(jnp.bfloat16).reshape(b, t, LANES)
    kw = kv_win[:, :, 0].astype(jnp.bfloat16).reshape(b, t, LANES)
    gtt = gt.astype(jnp.float32).reshape(b, n_qt, tq, gt.shape[-1]).transpose(0, 1, 3, 2)

    q_spec = pl.BlockSpec((1, N_KV_HEADS, 1, LANES, cols), lambda i, j: (i, 0, j, 0, 0))

    def whole(shape):
        nd = len(shape)
        return pl.BlockSpec((1,) + shape, lambda i, j: (i,) + (0,) * nd)

    out = pl.pallas_call(
        functools.partial(_nsa_prompt_kernel, seq=t),
        grid=(b, n_qt),
        in_specs=[q_spec, q_spec, whole((ncp, LANES)), whole((LANES, ncp)), whole((t, LANES)),
                  whole((n_kt, LANES, tk)), whole((t, LANES)), whole((n_kt, LANES, tk)),
                  pl.BlockSpec((1, 1, gt.shape[-1], tq), lambda i, j: (i, j, 0, 0))],
        out_specs=pl.BlockSpec((1, 1, GQA_RATIO, LANES, tq), lambda i, j: (i, j, 0, 0, 0)),
        out_shape=jax.ShapeDtypeStruct((b, n_qt, GQA_RATIO, LANES, tq), jnp.float32),
        scratch_shapes=[pltpu.VMEM((N_KV_HEADS, 1, cols), jnp.float32), pltpu.VMEM((N_KV_HEADS, 1, cols), jnp.float32),
                        pltpu.VMEM((N_KV_HEADS, LANES, cols), jnp.float32), pltpu.VMEM((LANES, cols), jnp.float32)],
        compiler_params=pltpu.CompilerParams(dimension_semantics=("arbitrary", "arbitrary")),
        name="nsa_prompt",
    )(qc, qr, kc, vct, ks, values_t(kv_sel), kw, values_t(kv_win), gtt)
    out = out.reshape(b, n_qt, GQA_RATIO, N_KV_HEADS, HEAD_DIM, tq).transpose(0, 1, 5, 3, 2, 4)
    return out.reshape(b, t, NSA_WIDTH)


KV_ROW = 2 * N_KV_HEADS * HEAD_DIM
SEG_W = CMP_STRIDE * KV_ROW
SAMPLE_PG = 64


def _sample_cmp_kernel(pt_ref, *refs, pg, n_pages, s_len):
    page_refs = refs[:pg]
    (new_ref, w1_ref, b1_ref, w2_ref, b2_ref, qct_ref, oc_ref, sel_ref, t_scr, pseg_scr) = refs[pg:]
    j = pl.program_id(1)
    n_steps = n_pages // pg
    seg_per_page = PAGE_SIZE // CMP_STRIDE
    m_rows = pg * seg_per_page
    past = n_pages * PAGE_SIZE
    n_seg = past // CMP_STRIDE
    rows = GQA_RATIO * s_len

    for i in range(pg):
        for c in range(2):
            t_scr[c, i * PAGE_SIZE:(i + 1) * PAGE_SIZE, :] = page_refs[i][0, c * LANES:(c + 1) * LANES, :].T
    for c in range(2):
        xc = jnp.concatenate([t_scr[c, pl.ds(r, m_rows, stride=CMP_STRIDE), :]
                              for r in range(CMP_STRIDE)], axis=1)
        xn = jnp.concatenate([new_ref[0, :, r * KV_ROW + c * LANES:r * KV_ROW + (c + 1) * LANES]
                              for r in range(CMP_STRIDE)], axis=1)
        xc = jnp.concatenate([xc, xn], axis=0).astype(jnp.bfloat16)
        pseg_scr[c, pl.ds(pl.multiple_of(j * m_rows, m_rows), m_rows + 8), :] = jnp.dot(
            xc, w1_ref[c], preferred_element_type=jnp.float32)

    @pl.when(j == n_steps - 1)
    def _():
        kv = []
        for c in range(2):
            acc = jnp.zeros((n_seg, LANES), jnp.float32) + b2_ref[c]
            for g in range(N_KV_HEADS):
                lo = g * N_HALF * CMP_HIDDEN
                hid = (pseg_scr[c, 0:n_seg, lo:lo + CMP_HIDDEN]
                       + pseg_scr[c, 1:n_seg + 1, lo + CMP_HIDDEN:lo + 2 * CMP_HIDDEN] + b1_ref[c])
                hid = jax.nn.gelu(hid)
                acc = acc + jnp.dot(hid.astype(jnp.bfloat16), w2_ref[c, g], preferred_element_type=jnp.float32)
            kv.append(acc)
        k_c, v_c = kv
        n_sb = past // SEL_BLOCK + -(-s_len // SEL_BLOCK)
        width = sel_ref.shape[-1]
        tok = lax.broadcasted_iota(jnp.int32, (n_seg, LANES), 1) % s_len
        blk_end = lax.broadcasted_iota(jnp.int32, (n_seg, LANES), 0) * CMP_STRIDE + (CMP_BLOCK - 1)
        cmask = blk_end <= past + tok
        s = jnp.dot(k_c, qct_ref[0], precision=HIGHEST, preferred_element_type=jnp.float32)
        s = jnp.where(cmask, s, NEG_INF)
        mx = jnp.max(s, axis=0, keepdims=True)
        p_t = jnp.where(cmask, jnp.exp(s - mx), 0.0)
        p_t = p_t / jnp.maximum(jnp.sum(p_t, axis=0, keepdims=True), 1e-30)
        o_c = jnp.dot(p_t.T.astype(jnp.bfloat16), v_c.astype(jnp.bfloat16), preferred_element_type=jnp.float32)
        ci = lax.broadcasted_iota(jnp.int32, (LANES, LANES), 0)
        cj = lax.broadcasted_iota(jnp.int32, (LANES, LANES), 1)
        same = ((ci // rows == cj // rows) & (ci % s_len == cj % s_len)).astype(jnp.float32)
        p_grp = jnp.dot(p_t, same, precision=HIGHEST, preferred_element_type=jnp.float32)
        score = _block_scores(p_grp, width, past + lax.broadcasted_iota(jnp.int32, (width, LANES), 1) % s_len)
        sel = _topk_cols(score, n_sb, min(N_SEL, n_sb)).T
        for g in range(N_KV_HEADS):
            oc_ref[0, g] = o_c[g * rows:(g + 1) * rows]
            sel_ref[0, g] = sel[g * rows:(g + 1) * rows]


def _sample_attn_kernel(pt_ref, *refs, pg, n_pages, s_len):
    page_refs = refs[:pg]
    (qr_ref, sel_ref, tail_ref, win_ref, wnew_ref, oc_ref, gt_ref, o_ref,
     k_scr, v_scr, m_scr, l_scr, acc_scr) = refs[pg:]
    j = pl.program_id(1)
    n_steps = n_pages // pg
    rows = GQA_RATIO * s_len
    keys = pg * PAGE_SIZE
    past = n_pages * PAGE_SIZE
    width = sel_ref.shape[-1]

    @pl.when(j == 0)
    def _():
        m_scr[...] = jnp.full(m_scr.shape, NEG_INF, jnp.float32)
        l_scr[...] = jnp.zeros(l_scr.shape, jnp.float32)
        acc_scr[...] = jnp.zeros(acc_scr.shape, jnp.float32)

    for i in range(pg):
        k_scr[:, i * PAGE_SIZE:(i + 1) * PAGE_SIZE] = page_refs[i][0, 0:LANES, :].astype(jnp.bfloat16)
        v_scr[:, i * PAGE_SIZE:(i + 1) * PAGE_SIZE] = page_refs[i][0, LANES:2 * LANES, :].astype(jnp.bfloat16)
    blk_of_key = (j * keys + lax.broadcasted_iota(jnp.int32, (width, keys), 1)) // SEL_BLOCK
    expand = (blk_of_key == lax.broadcasted_iota(jnp.int32, (width, keys), 0)).astype(jnp.bfloat16)
    for g in range(N_KV_HEADS):
        q = qr_ref[0, g]
        s = jnp.dot(q, k_scr[...], preferred_element_type=jnp.float32)
        mask = jnp.dot(sel_ref[0, g].astype(jnp.bfloat16), expand, preferred_element_type=jnp.float32) > 0.5
        s = jnp.where(mask, s, NEG_INF)
        m_old = m_scr[g]
        m_new = jnp.maximum(m_old, jnp.max(s, axis=-1, keepdims=True))
        alpha = jnp.exp(m_old - m_new)
        p = jnp.where(mask, jnp.exp(s - m_new), 0.0)
        l_scr[g] = alpha * l_scr[g] + jnp.sum(p, axis=-1, keepdims=True)
        acc_scr[g] = alpha * acc_scr[g] + _dot_nt(p.astype(jnp.bfloat16), v_scr[...])
        m_scr[g] = m_new

    @pl.when(j == n_steps - 1)
    def _():
        tok8 = lax.broadcasted_iota(jnp.int32, (rows, 8), 0) % s_len
        new_ok = lax.broadcasted_iota(jnp.int32, (rows, 8), 1) <= tok8
        wb = win_ref.shape[2]
        tokw = lax.broadcasted_iota(jnp.int32, (rows, wb), 0) % s_len
        kpos = past - wb + lax.broadcasted_iota(jnp.int32, (rows, wb), 1)
        win_ok = (kpos > past + tokw - WINDOW) & (kpos >= 0)
        gates = jax.nn.sigmoid(gt_ref[0])
        for g in range(N_KV_HEADS):
            q = qr_ref[0, g]
            tail_sel = sel_ref[0, g][:, past // SEL_BLOCK:past // SEL_BLOCK + 1] > 0.5
            t_mask = new_ok & tail_sel
            s_t = jnp.where(t_mask, _dot_nt(q, tail_ref[0, :, 0:LANES].astype(jnp.bfloat16)), NEG_INF)
            m_old = m_scr[g]
            m_new = jnp.maximum(m_old, jnp.max(s_t, axis=-1, keepdims=True))
            alpha = jnp.exp(m_old - m_new)
            p_t = jnp.where(t_mask, jnp.exp(s_t - m_new), 0.0)
            l_s = alpha * l_scr[g] + jnp.sum(p_t, axis=-1, keepdims=True)
            o_s = (alpha * acc_scr[g] + jnp.dot(p_t.astype(jnp.bfloat16),
                                                tail_ref[0, :, LANES:2 * LANES].astype(jnp.bfloat16),
                                                preferred_element_type=jnp.float32)) / jnp.maximum(l_s, 1e-30)
            s_w = jnp.where(win_ok, jnp.dot(q, win_ref[0, 0:LANES, :].astype(jnp.bfloat16),
                                            preferred_element_type=jnp.float32), NEG_INF)
            s_n = jnp.where(new_ok, _dot_nt(q, wnew_ref[0, :, 0:LANES].astype(jnp.bfloat16)), NEG_INF)
            mw = jnp.maximum(jnp.max(s_w, axis=-1, keepdims=True), jnp.max(s_n, axis=-1, keepdims=True))
            p_w = jnp.where(win_ok, jnp.exp(s_w - mw), 0.0)
            p_n = jnp.where(new_ok, jnp.exp(s_n - mw), 0.0)
            l_w = jnp.sum(p_w, axis=-1, keepdims=True) + jnp.sum(p_n, axis=-1, keepdims=True)
            o_w = (_dot_nt(p_w.astype(jnp.bfloat16), win_ref[0, LANES:2 * LANES, :].astype(jnp.bfloat16))
                   + jnp.dot(p_n.astype(jnp.bfloat16), wnew_ref[0, :, LANES:2 * LANES].astype(jnp.bfloat16),
                             preferred_element_type=jnp.float32)) / jnp.maximum(l_w, 1e-30)
            gl = gates[g]
            o_ref[0, g] = gl[:, 0:1] * oc_ref[0, g] + gl[:, 1:2] * o_s + gl[:, 2:3] * o_w


def _nsa_sample_pallas(q, q_rot, kv_cmp, kv_sel, kv_win, gt, cache_cmp, cache_sel, win_buf, page_table,
                       w1, b1, w2, b2):
    pg = SAMPLE_PG
    db, s_len = q.shape[0], q.shape[1]
    n_pages = page_table.shape[1]
    assert n_pages % pg == 0 and s_len <= 8 and s_len <= CMP_STRIDE and N_NSA_HEADS * s_len <= LANES
    past = n_pages * PAGE_SIZE
    n_seg = past // CMP_STRIDE
    n_sb = past // SEL_BLOCK + 1
    width = -(-n_sb // LANES) * LANES
    rows = GQA_RATIO * s_len
    seg_per_page = PAGE_SIZE // CMP_STRIDE

    def group_rows(a, dtype):
        a = (a * ATTN_SCALE).reshape(db, s_len, N_KV_HEADS, GQA_RATIO, HEAD_DIM).transpose(0, 2, 3, 1, 4)
        eye = jnp.eye(N_KV_HEADS, dtype=a.dtype)[None, :, None, None, :, None]
        return (a[:, :, :, :, None, :] * eye).reshape(db, N_KV_HEADS, rows, LANES).astype(dtype)

    def pad8(a):
        return jnp.pad(a.reshape(db, s_len, KV_ROW), ((0, 0), (0, 8 - s_len), (0, 0)))

    qg = (q * ATTN_SCALE).reshape(db, s_len, N_KV_HEADS, GQA_RATIO, HEAD_DIM).transpose(0, 2, 4, 3, 1)
    eye_q = jnp.eye(N_KV_HEADS, dtype=qg.dtype)[None, :, None, :, None]
    qct = (qg.reshape(db, N_KV_HEADS, HEAD_DIM, 1, rows) * eye_q).reshape(db, LANES, N_KV_HEADS * rows)
    qct = jnp.pad(qct, ((0, 0), (0, 0), (0, LANES - N_KV_HEADS * rows)))
    qr = group_rows(q_rot, jnp.bfloat16)
    new_seg = jnp.pad(kv_cmp.reshape(db, 1, s_len * KV_ROW), ((0, 0), (0, 7), (0, SEG_W - s_len * KV_ROW)))
    eye_g = jnp.eye(N_KV_HEADS, dtype=w1.dtype)
    w1t = w1.transpose(0, 2, 3, 1, 4)
    w1_bd = (w1t[:, :, None, :, None, :, :] * eye_g[None, None, :, None, :, None, None]).reshape(
        2, CMP_STRIDE * N_KV_HEADS * HEAD_DIM, N_KV_HEADS * N_HALF * CMP_HIDDEN).astype(jnp.bfloat16)
    w2_g = (w2[:, None, :, None, :] * eye_g[None, :, None, :, None]).reshape(
        2, N_KV_HEADS, CMP_HIDDEN, LANES).astype(jnp.bfloat16)
    b2_t = jnp.tile(b2, (1, N_KV_HEADS)).reshape(2, 1, LANES)
    b1_r = b1.reshape(2, 1, CMP_HIDDEN)

    def page_specs(block):
        return [pl.BlockSpec(block, functools.partial(lambda b, j, pt, i: (pt[b, j * pg + i], 0, 0), i=i))
                for i in range(pg)]

    def per_seq(shape):
        nd = len(shape)
        return pl.BlockSpec((1,) + shape, lambda b, j, pt: (b,) + (0,) * nd)

    def const(shape):
        nd = len(shape)
        return pl.BlockSpec(shape, lambda b, j, pt: (0,) * nd, pipeline_mode=pl.Buffered(1))

    def feature_major(a):
        return a.transpose(0, 2, 3, 4, 1).reshape(a.shape[0], KV_ROW, a.shape[1])

    page_block = (1, KV_ROW, PAGE_SIZE)
    o_c, sel = pl.pallas_call(
        functools.partial(_sample_cmp_kernel, pg=pg, n_pages=n_pages, s_len=s_len),
        grid_spec=pltpu.PrefetchScalarGridSpec(
            num_scalar_prefetch=1, grid=(db, n_pages // pg),
            in_specs=page_specs(page_block) + [
                per_seq((8, SEG_W)), const(w1_bd.shape), const(b1_r.shape), const(w2_g.shape), const(b2_t.shape),
                per_seq((LANES, LANES))],
            out_specs=[per_seq((N_KV_HEADS, rows, LANES)), per_seq((N_KV_HEADS, rows, width))],
            scratch_shapes=[pltpu.VMEM((2, pg * PAGE_SIZE, LANES), jnp.float32),
                            pltpu.VMEM((2, n_seg + 8, N_KV_HEADS * N_HALF * CMP_HIDDEN), jnp.float32)]),
        out_shape=[jax.ShapeDtypeStruct((db, N_KV_HEADS, rows, LANES), jnp.float32),
                   jax.ShapeDtypeStruct((db, N_KV_HEADS, rows, width), jnp.float32)],
        compiler_params=pltpu.CompilerParams(dimension_semantics=("arbitrary", "arbitrary"),
                                             vmem_limit_bytes=56 * 1024 * 1024),
        name="sample_cmp_select",
    )(page_table, *([feature_major(cache_cmp)] * pg), new_seg, w1_bd, b1_r, w2_g, b2_t, qct)

    gl = jnp.pad(gt.astype(jnp.float32).reshape(db, s_len, N_KV_HEADS, GQA_RATIO, 3).transpose(0, 2, 3, 1, 4)
                 .reshape(db, N_KV_HEADS, rows, 3), ((0, 0), (0, 0), (0, 0), (0, LANES - 3)))
    wb = win_buf.shape[1]
    out = pl.pallas_call(
        functools.partial(_sample_attn_kernel, pg=pg, n_pages=n_pages, s_len=s_len),
        grid_spec=pltpu.PrefetchScalarGridSpec(
            num_scalar_prefetch=1, grid=(db, n_pages // pg),
            in_specs=page_specs(page_block) + [
                per_seq((N_KV_HEADS, rows, LANES)), per_seq((N_KV_HEADS, rows, width)), per_seq((8, KV_ROW)),
                per_seq((KV_ROW, wb)), per_seq((8, KV_ROW)), per_seq((N_KV_HEADS, rows, LANES)),
                per_seq((N_KV_HEADS, rows, LANES))],
            out_specs=per_seq((N_KV_HEADS, rows, LANES)),
            scratch_shapes=[pltpu.VMEM((LANES, pg * PAGE_SIZE), jnp.bfloat16),
                            pltpu.VMEM((LANES, pg * PAGE_SIZE), jnp.bfloat16),
                            pltpu.VMEM((N_KV_HEADS, rows, 1), jnp.float32),
                            pltpu.VMEM((N_KV_HEADS, rows, 1), jnp.float32),
                            pltpu.VMEM((N_KV_HEADS, rows, LANES), jnp.float32)]),
        out_shape=jax.ShapeDtypeStruct((db, N_KV_HEADS, rows, LANES), jnp.float32),
        compiler_params=pltpu.CompilerParams(dimension_semantics=("arbitrary", "arbitrary")),
        name="sample_sel_win_attn",
    )(page_table, *([feature_major(cache_sel)] * pg), qr, sel, pad8(kv_sel),
      feature_major(win_buf), pad8(kv_win), o_c, gl)
    out = out.reshape(db, N_KV_HEADS, GQA_RATIO, s_len, N_KV_HEADS, HEAD_DIM)
    out = jnp.stack([out[:, g, :, :, g, :] for g in range(N_KV_HEADS)], axis=1)
    return out.transpose(0, 3, 1, 2, 4).reshape(db, s_len, NSA_WIDTH)


ROUTER_TT = 256
MOE_BM = 256


def _router_kernel(x_ref, mix_ref, wout_ref, g_ref, wrt_ref, br_ref,
                   x1_ref, h_ref, eidx_ref, gate_ref, rank_ref, cnt_ref, wout_bf, run_scr):
    tt = x_ref.shape[0]
    n_e = wrt_ref.shape[0]

    @pl.when(pl.program_id(0) == 0)
    def _():
        wout_bf[...] = wout_ref[...].astype(jnp.bfloat16)
        run_scr[...] = jnp.zeros_like(run_scr)

    x1 = x_ref[...] + jnp.dot(mix_ref[...].astype(jnp.bfloat16), wout_bf[...], preferred_element_type=jnp.float32)
    x1_ref[...] = x1
    hn = x1 * lax.rsqrt(jnp.mean(x1 * x1, axis=-1, keepdims=True) + RMS_EPS) * g_ref[...]
    h_ref[...] = hn

    score = _dot_nt(wrt_ref[...], hn, HIGHEST) + br_ref[...]
    ef = lax.broadcasted_iota(jnp.int32, (n_e, tt), 0).astype(jnp.float32)
    vals, hits = [], []
    for k in range(TOP_K):
        best = jnp.max(score, axis=0, keepdims=True)
        first = jnp.min(jnp.where(score == best, ef, float(n_e)), axis=0, keepdims=True)
        hit = ef == first
        vals.append(best)
        hits.append(hit)
        eidx_ref[k:k + 1, :] = first.astype(jnp.int32)
        score = jnp.where(hit, -3e38, score)
    exps = [jnp.exp(v - vals[0]) for v in vals]
    denom = sum(exps[1:], exps[0])
    for k in range(TOP_K):
        gate_ref[k:k + 1, :] = exps[k] / denom

    chosen = functools.reduce(jnp.logical_or, hits)
    before = (lax.broadcasted_iota(jnp.int32, (tt, tt), 0) < lax.broadcasted_iota(jnp.int32, (tt, tt), 1))
    earlier = jnp.dot(chosen.astype(jnp.bfloat16), before.astype(jnp.bfloat16), preferred_element_type=jnp.float32)
    pos = earlier + run_scr[...]
    for k in range(TOP_K):
        rank_ref[k:k + 1, :] = jnp.sum(jnp.where(hits[k], pos, 0.0), axis=0, keepdims=True).astype(jnp.int32)
    run_scr[...] = run_scr[...] + jnp.sum(chosen.astype(jnp.float32), axis=1, keepdims=True)
    cnt_ref[...] = jnp.broadcast_to(run_scr[...], cnt_ref.shape).astype(jnp.int32)


def _router_pallas(x, mix, w_out, g_ffn, w_router, b_router):
    n, d = x.shape
    c = mix.shape[1]
    n_e = w_router.shape[1]
    tt = ROUTER_TT
    assert n % tt == 0

    def row(w):
        return pl.BlockSpec((tt, w), lambda i: (i, 0))

    def full(a, b):
        return pl.BlockSpec((a, b), lambda i: (0, 0))

    k4 = pl.BlockSpec((TOP_K, tt), lambda i: (0, i))
    return pl.pallas_call(
        _router_kernel,
        grid=(n // tt,),
        in_specs=[row(d), row(c), full(c, d), full(1, d), full(n_e, d), full(n_e, 1)],
        out_specs=[row(d), row(d), k4, k4, k4, full(n_e, LANES)],
        out_shape=[jax.ShapeDtypeStruct((n, d), jnp.float32), jax.ShapeDtypeStruct((n, d), jnp.float32),
                   jax.ShapeDtypeStruct((TOP_K, n), jnp.int32), jax.ShapeDtypeStruct((TOP_K, n), jnp.float32),
                   jax.ShapeDtypeStruct((TOP_K, n), jnp.int32), jax.ShapeDtypeStruct((n_e, LANES), jnp.int32)],
        scratch_shapes=[pltpu.VMEM((c, d), jnp.bfloat16), pltpu.VMEM((n_e, 1), jnp.float32)],
        compiler_params=pltpu.CompilerParams(dimension_semantics=("arbitrary",)),
        name="outproj_router",
    )(x, mix, w_out, g_ffn.reshape(1, d), w_router.T, b_router.reshape(n_e, 1))


def _expert_kernel(blk_e_ref, n_used_ref, x_ref, wup_ref, bup_ref, wdn_ref, bdn_ref, o_ref, wup_bf, wdn_bf):
    i = pl.program_id(0)
    d_ff = wdn_ref.shape[1]

    @pl.when(i < n_used_ref[0])
    def _():
        e = blk_e_ref[i]
        prev = blk_e_ref[jnp.maximum(i - 1, 0)]

        @pl.when((i == 0) | (e != prev))
        def _():
            wup_bf[...] = wup_ref[0].astype(jnp.bfloat16)
            wdn_bf[...] = wdn_ref[0].astype(jnp.bfloat16)

        u = jnp.dot(x_ref[...].astype(jnp.bfloat16), wup_bf[...], preferred_element_type=jnp.float32) + bup_ref[0]
        glu = jnp.minimum(u[:, :d_ff], SWIGLU_LIMIT)
        lin = jnp.clip(u[:, d_ff:], -SWIGLU_LIMIT, SWIGLU_LIMIT)
        a = glu * jax.nn.sigmoid(SWIGLU_ALPHA * glu) * (lin + 1.0)
        o_ref[...] = jnp.dot(a.astype(jnp.bfloat16), wdn_bf[...], preferred_element_type=jnp.float32) + bdn_ref[0]

    @pl.when(i >= n_used_ref[0])
    def _():
        o_ref[...] = jnp.zeros_like(o_ref)


def _expert_vmem_bytes(bm, d, f2, d_ff):
    weights = 2 * 4 * (d * f2 + d_ff * d) + 2 * (d * f2 + d_ff * d)
    rows = 2 * bm * d * (4 + 4)
    temps = bm * f2 * 4 * 2 + bm * d_ff * (4 + 2)
    return weights + rows + temps


def _experts_pallas(xs, blk_e, n_used, w_up, b_up, w_down, b_down, bm):
    n_slots, d = xs.shape
    n_e, _, f2 = w_up.shape
    d_ff = w_down.shape[1]
    grid_spec = pltpu.PrefetchScalarGridSpec(
        num_scalar_prefetch=2,
        grid=(n_slots // bm,),
        in_specs=[pl.BlockSpec((bm, d), lambda i, be, nu: (i, 0)),
                  pl.BlockSpec((1, d, f2), lambda i, be, nu: (be[i], 0, 0)),
                  pl.BlockSpec((1, 1, f2), lambda i, be, nu: (be[i], 0, 0)),
                  pl.BlockSpec((1, d_ff, d), lambda i, be, nu: (be[i], 0, 0)),
                  pl.BlockSpec((1, 1, d), lambda i, be, nu: (be[i], 0, 0))],
        out_specs=pl.BlockSpec((bm, d), lambda i, be, nu: (i, 0)),
        scratch_shapes=[pltpu.VMEM((d, f2), jnp.bfloat16), pltpu.VMEM((d_ff, d), jnp.bfloat16)],
    )
    vmem_limit = _expert_vmem_bytes(bm, d, f2, d_ff) * 5 // 4
    return pl.pallas_call(
        _expert_kernel,
        grid_spec=grid_spec,
        out_shape=jax.ShapeDtypeStruct((n_slots, d), jnp.float32),
        compiler_params=pltpu.CompilerParams(dimension_semantics=("arbitrary",), vmem_limit_bytes=vmem_limit),
        name="expert_mlp",
    )(blk_e, n_used, xs, w_up, b_up.reshape(n_e, 1, f2), w_down, b_down.reshape(n_e, 1, d))


def _combine_norm_kernel(x1_ref, og_ref, gate_ref, g_ref, o_ref):
    y = x1_ref[...]
    gates = gate_ref[...]
    for k in range(TOP_K):
        y = y + gates[:, k:k + 1] * og_ref[k]
    o_ref[...] = y * lax.rsqrt(jnp.mean(y * y, axis=-1, keepdims=True) + RMS_EPS) * g_ref[...]


def _combine_norm_pallas(x1, og, gate_t, g_final):
    n, d = x1.shape
    tt = ROUTER_TT
    return pl.pallas_call(
        _combine_norm_kernel,
        grid=(n // tt,),
        in_specs=[pl.BlockSpec((tt, d), lambda i: (i, 0)), pl.BlockSpec((TOP_K, tt, d), lambda i: (0, i, 0)),
                  pl.BlockSpec((tt, TOP_K), lambda i: (i, 0)), pl.BlockSpec((1, d), lambda i: (0, 0))],
        out_specs=pl.BlockSpec((tt, d), lambda i: (i, 0)),
        out_shape=jax.ShapeDtypeStruct((n, d), jnp.float32),
        compiler_params=pltpu.CompilerParams(dimension_semantics=("arbitrary",)),
        name="combine_final_norm",
    )(x1, og, gate_t, g_final.reshape(1, d))


def _finish_pallas(x, mix, w_out, g_ffn, w_router, b_router, w_up, b_up, w_down, b_down, g_final):
    n, d = x.shape
    n_e = w_router.shape[1]
    bm = MOE_BM
    x1, h, eidx, gate, rank, cnt = _router_pallas(x, mix, w_out, g_ffn, w_router, b_router)
    counts = cnt[:, 0]
    padded = (counts + bm - 1) // bm * bm
    pad_end = jnp.cumsum(padded)
    gstart = pad_end - padded
    pick = eidx[None] == jnp.arange(n_e, dtype=jnp.int32)[:, None, None]
    dest = jnp.sum(jnp.where(pick, gstart[:, None, None], 0), axis=0) + rank
    nb = -(-(n * TOP_K) // bm) + n_e
    blk_start = jnp.arange(nb, dtype=jnp.int32) * bm
    blk_e = jnp.minimum(jnp.sum((pad_end[None, :] <= blk_start[:, None]).astype(jnp.int32), axis=1), n_e - 1)
    n_used = (pad_end[-1] // bm).astype(jnp.int32).reshape(1)
    tok = jnp.broadcast_to(jnp.arange(n, dtype=jnp.int32)[None, :], (TOP_K, n))
    src = jnp.zeros((nb * bm,), jnp.int32).at[dest.reshape(-1)].set(tok.reshape(-1))
    out = _experts_pallas(h[src], blk_e, n_used, w_up, b_up, w_down, b_down, bm)
    return _combine_norm_pallas(x1, out[dest], gate.T, g_final)


def kernel(x_prompt, x_sample, cache_cmp_kv, cache_sel_kv, cache_win_kv, state_ret, page_table, g_attn, w_in, w_cmp1, b_cmp1, w_cmp2, b_cmp2, w_out, g_ffn, w_router, b_router, w_up, b_up, w_down, b_down, g_final):
    seq = x_prompt.shape[1]
    past = page_table.shape[1] * PAGE_SIZE
    pos_p = jnp.arange(seq, dtype=jnp.int32)
    pos_s = past + jnp.arange(x_sample.shape[1], dtype=jnp.int32)
    log_g = _ret_log_decay()
    assert DEPTH == 1
    l = 0
    d = x_prompt.shape[-1]
    q, kv_cmp_p, kv_sel_p, kv_win_p, gt, r_q, r_k, r_v, r_g = _mixer_inputs(x_prompt, pos_p, g_attn[l], w_in[l])
    blocks = _compress_blocks(_segment_proj(kv_cmp_p, w_cmp1[l]), b_cmp1[l], w_cmp2[l], b_cmp2[l])
    o_nsa = _nsa_prompt_pallas(q, _rope(q, pos_p), blocks, kv_sel_p, kv_win_p, gt)
    o_ret, ret_p = _ret_prompt_pallas(r_q, r_k, r_v, r_g, log_g)
    mix_p = jnp.concatenate([o_nsa, o_ret], axis=-1)
    q, kv_cmp_s, kv_sel_s, kv_win_s, gt, r_q, r_k, r_v, r_g = _mixer_inputs(x_sample, pos_s, g_attn[l], w_in[l])
    o_nsa_s = _nsa_sample_pallas(q, _rope(q, pos_s), kv_cmp_s, kv_sel_s, kv_win_s, gt, cache_cmp_kv[:, l],
                                 cache_sel_kv[:, l], cache_win_kv[:, l], page_table,
                                 w_cmp1[l], b_cmp1[l], w_cmp2[l], b_cmp2[l])
    win_s = jnp.concatenate([cache_win_kv[:, l], kv_win_s], axis=1)[:, x_sample.shape[1]:]
    ret_s, o_r = _ret_chunk(state_ret[:, l].astype(jnp.float32), r_q.astype(jnp.float32), r_k.astype(jnp.float32), r_v.astype(jnp.float32), log_g)
    mix_s = jnp.concatenate([o_nsa_s, _ret_out(o_r, r_g)], axis=-1)
    n_p = x_prompt.shape[0] * seq
    x_all = jnp.concatenate([x_prompt.reshape(n_p, d), x_sample.reshape(-1, d)], axis=0)
    mix_all = jnp.concatenate([mix_p.reshape(n_p, MIX_WIDTH), mix_s.reshape(-1, MIX_WIDTH)], axis=0)
    y = _finish_pallas(x_all, mix_all, w_out[l], g_ffn[l], w_router[l], b_router[l], w_up[l], b_up[l], w_down[l], b_down[l], g_final)
    y_prompt = y[:n_p].reshape(x_prompt.shape)
    y_sample = y[n_p:].reshape(x_sample.shape)
    win_p = kv_win_p[:, seq - min(WINDOW, seq):]
    return (y_prompt, y_sample, kv_cmp_p[:, None], kv_sel_p[:, None], win_p[:, None], ret_p[:, None],
            kv_cmp_s[:, None], kv_sel_s[:, None], win_s[:, None], ret_s[:, None])
```
